```python
import jax, jax.numpy as jnp
from jax import lax
import numpy as np

D_MODEL = 1024
BATCH = 4
SEQ = 8192
DEPTH = 4

GRID_W = 64
CTX_LEN = 256
RET_HEADS = 4
RET_DK = 128
RET_DV = 256
RET_QK = RET_HEADS * RET_DK
RET_WIDTH = RET_HEADS * RET_DV
CONV_WIDTH = D_MODEL // 2
CONV_K = 3
MIX_WIDTH = RET_WIDTH + CONV_WIDTH
IN_WIDTH = 2 * RET_QK + 2 * RET_WIDTH + 3 * CONV_WIDTH
CHUNK = 128
N_EXPERTS = 32
TOP_K = 4
D_FF = D_MODEL
SWIGLU_LIMIT = 7.0
SWIGLU_ALPHA = 1.702
EXPERT_BLOCK = 256
ROPE_BASE = 10000.0
EPS = 1e-6

kernel_name = "hybrid_retention_shortconv_moe_dit"


def _rmsnorm(x, g):
    xf = x.astype(jnp.float32)
    y = xf * lax.rsqrt(jnp.mean(xf * xf, axis=-1, keepdims=True) + EPS)
    return (y * g.astype(jnp.float32)).astype(x.dtype)


def _modulate(x, g, shift, scale):
    return _rmsnorm(x, g) * (1 + scale) + shift


def _rope_1d(x, pos):
    m = x.shape[-1] // 2
    freqs = ROPE_BASE ** (-jnp.arange(m, dtype=jnp.float32) / m)
    ang = pos.astype(jnp.float32)[:, None] * freqs[None, :]
    cos, sin = jnp.cos(ang).astype(x.dtype), jnp.sin(ang).astype(x.dtype)
    x1, x2 = x[..., :m], x[..., m:]
    return jnp.concatenate([x1 * cos - x2 * sin, x1 * sin + x2 * cos], axis=-1)


def _rope_2d(x, n_tokens):
    t = jnp.arange(n_tokens, dtype=jnp.int32)
    half = x.shape[-1] // 2
    return jnp.concatenate([_rope_1d(x[..., :half], t // GRID_W),
                            _rope_1d(x[..., half:], t % GRID_W)], axis=-1)


def _heads(a, d):
    b, l, _ = a.shape
    return a.reshape(b, l, RET_HEADS, d).transpose(0, 2, 1, 3)


def _project(p):
    q, k, v, g, cb, cc, cx = jnp.split(
        p, [RET_QK, 2 * RET_QK, 2 * RET_QK + RET_WIDTH, 2 * RET_QK + 2 * RET_WIDTH,
            2 * RET_QK + 2 * RET_WIDTH + CONV_WIDTH,
            2 * RET_QK + 2 * RET_WIDTH + 2 * CONV_WIDTH], axis=-1)
    return _heads(q, RET_DK), _heads(k, RET_DK), _heads(v, RET_DV), g, cb, cc, cx


def _chunk_states(kc, vc, lg, s0):
    i = jnp.arange(CHUNK, dtype=jnp.float32)
    k_decay = jnp.exp((CHUNK - 1 - i)[None, :] * lg[:, None]).astype(kc.dtype)
    chunk_decay = jnp.exp(CHUNK * lg)[None, :, None, None]
    kv = jnp.einsum('bhncd,hc,bhnce->nbhde', kc, k_decay, vc).astype(jnp.float32)

    def step(s, kv_n):
        return chunk_decay * s + kv_n, s

    s_final, s_prev = lax.scan(step, s0, kv)
    return s_prev, s_final


def _retention_dir(q, k, v, lg, s0, strict):
    b, h, n, dk = q.shape
    nc = n // CHUNK
    qc = q.reshape(b, h, nc, CHUNK, dk)
    kc = k.reshape(b, h, nc, CHUNK, dk)
    vc = v.reshape(b, h, nc, CHUNK, v.shape[-1])
    s_prev, s_final = _chunk_states(kc, vc, lg, s0)
    i = jnp.arange(CHUNK, dtype=jnp.float32)
    diff = i[:, None] - i[None, :]
    keep = (diff > 0) if strict else (diff >= 0)
    dmask = jnp.where(keep[None], jnp.exp(jnp.where(keep, diff, 0.0)[None] * lg[:, None, None]),
                      0.0).astype(q.dtype)
    scores = jnp.einsum('bhncd,bhnsd->bhncs', qc, kc) * dmask[:, None]
    inner = jnp.einsum('bhncs,bhnse->bhnce', scores, vc).astype(jnp.float32)
    q_decay = jnp.exp((i + 1)[None, :] * lg[:, None])
    cross = jnp.einsum('bhncd,hc,nbhde->bhnce', qc.astype(jnp.float32), q_decay, s_prev)
    return (inner + cross).reshape(b, h, n, -1), s_final


def _flip(a):
    return a[:, :, ::-1]


def _bidir_retention(q, k, v, lg_f, lg_b, s0_f, s0_b):
    out_f, s_f = _retention_dir(q, k, v, lg_f, s0_f, False)
    out_b, s_b = _retention_dir(_flip(q), _flip(k), _flip(v), lg_b, s0_b, True)
    return out_f + _flip(out_b), s_f, s_b


def _context_states(k, v, lg_f, lg_b, s0):
    b, h, l, dk = k.shape
    nc = l // CHUNK
    kc = k.reshape(b, h, nc, CHUNK, dk)
    vc = v.reshape(b, h, nc, CHUNK, v.shape[-1])
    _, s_f = _chunk_states(kc, vc, lg_f, s0)
    _, s_b = _chunk_states(kc[:, :, ::-1, ::-1], vc[:, :, ::-1, ::-1], lg_b, s0)
    return s_f, s_b


def _retention_readout(y, g):
    y = y * lax.rsqrt(jnp.mean(y * y, axis=-1, keepdims=True) + EPS)
    b, h, n, dv = y.shape
    y = y.transpose(0, 2, 1, 3).reshape(b, n, h * dv).astype(g.dtype)
    return jax.nn.silu(g) * y


def _conv3(u, w):
    pad = [(0, 0)] * (u.ndim - 2) + [(1, 1), (0, 0)]
    up = jnp.pad(u, pad)
    l = u.shape[-2]
    return up[..., :l, :] * w[0] + up[..., 1:l + 1, :] * w[1] + up[..., 2:, :] * w[2]


def _moe(h, router_w, router_b, w_up, b_up, w_down, b_down):
    t, d = h.shape
    logits = (h @ router_w + router_b).astype(jnp.float32)
    top_v, top_i = lax.top_k(logits, TOP_K)
    wts = jax.nn.softmax(top_v, axis=-1).astype(h.dtype)
    a = t * TOP_K
    e_flat = top_i.reshape(a).astype(jnp.int32)
    tok_flat = jnp.repeat(jnp.arange(t, dtype=jnp.int32), TOP_K)
    w_flat = wts.reshape(a)
    order = jnp.argsort(e_flat)
    e_s, tok_s, w_s = e_flat[order], tok_flat[order], w_flat[order]
    counts = jnp.bincount(e_flat, length=N_EXPERTS)
    starts = jnp.cumsum(counts) - counts
    padded = (counts + EXPERT_BLOCK - 1) // EXPERT_BLOCK * EXPERT_BLOCK
    pend = jnp.cumsum(padded)
    pstarts = pend - padded
    dest = pstarts[e_s] + (jnp.arange(a, dtype=jnp.int32) - starts[e_s])
    nb = -(-a // EXPERT_BLOCK) + N_EXPERTS
    p = nb * EXPERT_BLOCK
    buf_tok = jnp.full((p,), t, dtype=jnp.int32).at[dest].set(tok_s)
    buf_w = jnp.zeros((p,), h.dtype).at[dest].set(w_s)
    block_e = jnp.clip(jnp.searchsorted(pend, jnp.arange(nb, dtype=jnp.int32) * EXPERT_BLOCK,
                                        side='right'), 0, N_EXPERTS - 1)
    h_pad = jnp.concatenate([h, jnp.zeros((1, d), h.dtype)], axis=0)
    xb = h_pad[buf_tok].reshape(nb, EXPERT_BLOCK, d)

    def expert_block(args):
        xblk, e = args
        u = xblk @ w_up[e] + b_up[e]
        gate = jnp.minimum(u[:, :D_FF], SWIGLU_LIMIT)
        lin = jnp.clip(u[:, D_FF:], -SWIGLU_LIMIT, SWIGLU_LIMIT)
        act = gate * jax.nn.sigmoid(SWIGLU_ALPHA * gate) * (lin + 1)
        return act @ w_down[e] + b_down[e]

    yb = lax.map(expert_block, (xb, block_e)).reshape(p, d)
    y = jax.ops.segment_sum(yb * buf_w[:, None], buf_tok, num_segments=t + 1)
    return y[:t]


def setup_inputs(seed: int = 0) -> dict:
    key = jax.random.key(seed)
    ks = jax.random.split(key, 20)
    D = D_MODEL
    f32 = jnp.float32

    def nrm(k, shape, scale):
        return jax.random.normal(k, shape, f32) * scale

    gamma0 = 1.0 - 2.0 ** (-5.0 - jnp.arange(RET_HEADS, dtype=f32))
    logit0 = jnp.log(gamma0) - jnp.log1p(-gamma0)
    return {
        "x": nrm(ks[0], (BATCH, SEQ, D), 1.0),
        "c": nrm(ks[1], (BATCH, D), 1.0),
        "ctx": nrm(ks[2], (BATCH, CTX_LEN, D), 1.0),
        "c_ctx": nrm(ks[3], (D,), 1.0),
        "w_mod": nrm(ks[4], (DEPTH, D, 6 * D), 0.5 * D ** -0.5),
        "b_mod": nrm(ks[5], (DEPTH, 6 * D), 0.02),
        "norm_mix": 1.0 + nrm(ks[6], (DEPTH, D), 0.02),
        "norm_ffn": 1.0 + nrm(ks[7], (DEPTH, D), 0.02),
        "w_in": nrm(ks[8], (DEPTH, D, IN_WIDTH), D ** -0.5),
        "ret_decay_f": logit0[None, :] + nrm(ks[9], (DEPTH, RET_HEADS), 0.1),
        "ret_decay_b": logit0[None, :] + nrm(ks[10], (DEPTH, RET_HEADS), 0.1),
        "conv_w": nrm(ks[11], (DEPTH, CONV_K, CONV_WIDTH), CONV_K ** -0.5),
        "w_out": nrm(ks[12], (DEPTH, MIX_WIDTH, D), MIX_WIDTH ** -0.5),
        "router_w": nrm(ks[13], (DEPTH, D, N_EXPERTS), D ** -0.5),
        "router_b": nrm(ks[14], (DEPTH, N_EXPERTS), 0.01),
        "w_up": nrm(ks[15], (DEPTH, N_EXPERTS, D, 2 * D_FF), D ** -0.5),
        "b_up": nrm(ks[16], (DEPTH, N_EXPERTS, 2 * D_FF), 0.02),
        "w_down": nrm(ks[17], (DEPTH, N_EXPERTS, D_FF, D), D_FF ** -0.5),
        "b_down": nrm(ks[18], (DEPTH, N_EXPERTS, D), 0.02),
        "norm_final": 1.0 + nrm(ks[19], (D,), 0.02),
    }


def reference(x, c, ctx, c_ctx, w_mod, b_mod, norm_mix, norm_ffn, w_in, ret_decay_f,
              ret_decay_b, conv_w, w_out, router_w, router_b, w_up, b_up, w_down, b_down,
              norm_final):
    b, n, d = x.shape
    rows = n // GRID_W
    lc = ctx.shape[1]
    silu_c = jax.nn.silu(c)
    silu_cc = jax.nn.silu(c_ctx)
    q_scale = RET_DK ** -0.5
    s_zero = jnp.zeros((b, RET_HEADS, RET_DK, RET_DV), jnp.float32)
    for layer in range(DEPTH):
        last = layer == DEPTH - 1
        mod_lat = (silu_c @ w_mod[layer] + b_mod[layer])[:, None, :]
        mod_ctx = silu_cc @ w_mod[layer] + b_mod[layer]
        sh1, sc1, g1, sh2, sc2, g2 = jnp.split(mod_lat, 6, axis=-1)
        csh1, csc1, cg1, csh2, csc2, cg2 = jnp.split(mod_ctx, 6, axis=-1)
        lg_f = jax.nn.log_sigmoid(ret_decay_f[layer].astype(jnp.float32))
        lg_b = jax.nn.log_sigmoid(ret_decay_b[layer].astype(jnp.float32))

        h = _modulate(x, norm_mix[layer], sh1, sc1)
        hc = _modulate(ctx, norm_mix[layer], csh1, csc1)
        q, k, v, g, cb, cc, cx = _project(h @ w_in[layer])
        q = _rope_2d(q, n) * q_scale
        k = _rope_2d(k, n)
        if last:
            kv_c = hc @ w_in[layer][:, RET_QK:2 * RET_QK + RET_WIDTH]
            k_c = _heads(kv_c[..., :RET_QK], RET_DK)
            v_c = _heads(kv_c[..., RET_QK:], RET_DV)
            s_f, s_b = _context_states(k_c, v_c, lg_f, lg_b, s_zero)
        else:
            q_c, k_c, v_c, g_c, cb_c, cc_c, cx_c = _project(hc @ w_in[layer])
            y_c, s_f, s_b = _bidir_retention(q_c * q_scale, k_c, v_c, lg_f, lg_b, s_zero, s_zero)
            ret_c = _retention_readout(y_c, g_c)
            conv_c = cb_c * _conv3(cc_c * cx_c, conv_w[layer])
            mix_c = jnp.concatenate([ret_c, conv_c], axis=-1) @ w_out[layer]
        y_l, _, _ = _bidir_retention(q, k, v, lg_f, lg_b, s_f, s_b)
        ret_l = _retention_readout(y_l, g)
        u = (cc * cx).reshape(b, rows, GRID_W, CONV_WIDTH)
        conv_l = cb * _conv3(u, conv_w[layer]).reshape(b, n, CONV_WIDTH)
        mix_l = jnp.concatenate([ret_l, conv_l], axis=-1) @ w_out[layer]
        x = x + g1 * mix_l
        if not last:
            ctx = ctx + cg1 * mix_c

        h2 = _modulate(x, norm_ffn[layer], sh2, sc2).reshape(b * n, d)
        if last:
            y2 = _moe(h2, router_w[layer], router_b[layer], w_up[layer], b_up[layer],
                      w_down[layer], b_down[layer])
            x = x + g2 * y2.reshape(b, n, d)
        else:
            hc2 = _modulate(ctx, norm_ffn[layer], csh2, csc2).reshape(b * lc, d)
            y2 = _moe(jnp.concatenate([h2, hc2], axis=0), router_w[layer], router_b[layer],
                      w_up[layer], b_up[layer], w_down[layer], b_down[layer])
            x = x + g2 * y2[:b * n].reshape(b, n, d)
            ctx = ctx + cg2 * y2[b * n:].reshape(b, lc, d)
    return _rmsnorm(x, norm_final)
```

```python
import functools

import jax
import jax.numpy as jnp
from jax import lax
from jax.experimental import pallas as pl
from jax.experimental.pallas import tpu as pltpu

GRID_W = 64
RET_HEADS = 4
RET_DK = 128
RET_DV = 256
CHUNK = 128
N_EXPERTS = 32
TOP_K = 4
SWIGLU_LIMIT = 7.0
SWIGLU_ALPHA = 1.702
ROPE_BASE = 10000.0
EPS = 1e-6

SUBLANES = 8
LANES = 128
VMEM_LIMIT = 56 * 1024 * 1024

EXPERT_TILE = 256
ROW_ALIGN = SUBLANES

F32 = jnp.float32
BF16 = jnp.bfloat16
NT_DIMS = (((1,), (1,)), ((), ()))
TN_DIMS = (((0,), (0,)), ((), ()))


def _params(n_axes=1):
    return pltpu.CompilerParams(dimension_semantics=("arbitrary",) * n_axes, vmem_limit_bytes=VMEM_LIMIT)


def _modulated_norm(x, gain, shift, scale):
    ms = jnp.mean(x * x, axis=-1, keepdims=True)
    return x * lax.rsqrt(ms + EPS) * gain * (1.0 + scale) + shift


def _sigmoid(z):
    return 1.0 / (1.0 + jnp.exp(-z))


def _split_bf16(a):
    hi = a.astype(BF16)
    lo = (a - hi.astype(F32)).astype(BF16)
    return hi, lo


def _modulation_kernel(c_ref, w_ref, b_ref, o_ref):
    c = c_ref[...]
    s = c * _sigmoid(c)
    s_hi, s_lo = _split_bf16(s)
    w_hi, w_lo = _split_bf16(w_ref[0])
    acc = jnp.dot(s_hi, w_hi, preferred_element_type=F32)
    acc += jnp.dot(s_hi, w_lo, preferred_element_type=F32)
    acc += jnp.dot(s_lo, w_hi, preferred_element_type=F32)
    o_ref[0] = acc + b_ref[0]


def _modulation(cond, w_mod, b_mod):
    depth, d, six_d = w_mod.shape
    nblk = six_d // d
    return pl.pallas_call(
        _modulation_kernel,
        grid=(depth, nblk),
        in_specs=[
            pl.BlockSpec((SUBLANES, d), lambda l, j: (0, 0)),
            pl.BlockSpec((1, d, d), lambda l, j: (l, 0, j)),
            pl.BlockSpec((1, 1, d), lambda l, j: (l, 0, j)),
        ],
        out_specs=pl.BlockSpec((1, SUBLANES, d), lambda l, j: (l, 0, j)),
        out_shape=jax.ShapeDtypeStruct((depth, SUBLANES, six_d), F32),
        compiler_params=_params(2),
        name="modulation",
    )(cond, w_mod, b_mod.reshape(depth, 1, six_d))


def _inproj_kernel(x_ref, mod_ref, gain_ref, w_ref, cos_ref, sin_ref, qkv_ref, rest_ref, *, qk_width, q_scale):
    h = _modulated_norm(x_ref[...], gain_ref[...], mod_ref[0, 0:1, :], mod_ref[0, 1:2, :]).astype(BF16)
    reps = qk_width // RET_DK
    cos = jnp.concatenate([cos_ref[...]] * reps, axis=1)
    sin = jnp.concatenate([sin_ref[...]] * reps, axis=1)
    lane = lax.broadcasted_iota(jnp.int32, cos.shape, 1)
    first_half = (lane % (RET_DK // 2)) < (RET_DK // 4)
    for part, scale in ((0, q_scale), (1, 1.0)):
        c0 = part * qk_width
        p = jnp.dot(h, w_ref[:, c0:c0 + qk_width], preferred_element_type=F32)
        partner = jnp.where(first_half, pltpu.roll(p, qk_width - RET_DK // 4, 1), pltpu.roll(p, RET_DK // 4, 1))
        r = p * cos + partner * sin
        if scale != 1.0:
            r = r * scale
        qkv_ref[:, c0:c0 + qk_width] = r.astype(BF16)
    qkv_width = qkv_ref.shape[1]
    step = 512
    for c0 in range(2 * qk_width, qkv_width, step):
        qkv_ref[:, c0:c0 + step] = jnp.dot(h, w_ref[:, c0:c0 + step], preferred_element_type=F32).astype(BF16)
    for c0 in range(0, rest_ref.shape[1], step):
        rest_ref[:, c0:c0 + step] = jnp.dot(
            h, w_ref[:, qkv_width + c0:qkv_width + c0 + step], preferred_element_type=F32)


def _inproj(xs, mods, gain, w_bf16, cos_t, sin_t, *, tm, tpb, n_batch, qk_width, qkv_width):
    t, d = xs.shape
    in_width = w_bf16.shape[1]
    rest_width = in_width - qkv_width
    pos_blocks = cos_t.shape[0] // tm - 1

    def seg(i):
        return jnp.where(i % tpb == 0, n_batch, i // tpb)

    def pos(i):
        j = i % tpb
        return jnp.where(j == 0, pos_blocks, j - 1)

    return pl.pallas_call(
        functools.partial(_inproj_kernel, qk_width=qk_width, q_scale=RET_DK ** -0.5),
        grid=(t // tm,),
        in_specs=[
            pl.BlockSpec((tm, d), lambda i: (i, 0)),
            pl.BlockSpec((1, SUBLANES, d), lambda i: (seg(i), 0, 0)),
            pl.BlockSpec((1, d), lambda i: (0, 0)),
            pl.BlockSpec((d, in_width), lambda i: (0, 0)),
            pl.BlockSpec((tm, RET_DK), lambda i: (pos(i), 0)),
            pl.BlockSpec((tm, RET_DK), lambda i: (pos(i), 0)),
        ],
        out_specs=[
            pl.BlockSpec((tm, qkv_width), lambda i: (i, 0)),
            pl.BlockSpec((tm, rest_width), lambda i: (i, 0)),
        ],
        out_shape=[
            jax.ShapeDtypeStruct((t, qkv_width), BF16),
            jax.ShapeDtypeStruct((t, rest_width), F32),
        ],
        compiler_params=_params(1),
        name="inproj",
    )(xs, mods, gain, w_bf16, cos_t, sin_t)


def _retention_kernel(gc_ref, q_ref, k_ref, v_ref, dec_ref, y_ref, sf_ref, sb_ref, *, lc, n):
    head = pl.program_id(1)
    qdf, kdf, qdb, kdb, dmat = (dec_ref[0, j] for j in range(5))
    gcf = gc_ref[head, 0]
    gcb = gc_ref[head, 1]

    def rows(r0):
        return pl.ds(pl.multiple_of(r0, CHUNK), CHUNK)

    def advance(s_ref, k, kd, v, gc):
        kdec = (k.astype(F32) * kd).astype(BF16)
        s_ref[...] = gc * s_ref[...] + lax.dot_general(kdec, v, TN_DIMS, preferred_element_type=F32)

    def bwd_step(r0):
        q, k, v = q_ref[rows(r0), :], k_ref[rows(r0), :], v_ref[rows(r0), :]
        qdec = (q.astype(F32) * qdb).astype(BF16)
        y_ref[rows(r0), :] = jnp.dot(qdec, sb_ref[...].astype(BF16), preferred_element_type=F32)
        advance(sb_ref, k, kdb, v, gcb)

    def fwd_step(r0):
        q, k, v = q_ref[rows(r0), :], k_ref[rows(r0), :], v_ref[rows(r0), :]
        scores = lax.dot_general(q, k, NT_DIMS, preferred_element_type=F32)
        qdec = (q.astype(F32) * qdf).astype(BF16)
        y = y_ref[rows(r0), :]
        y += jnp.dot((scores * dmat).astype(BF16), v, preferred_element_type=F32)
        y += jnp.dot(qdec, sf_ref[...].astype(BF16), preferred_element_type=F32)
        y_ref[rows(r0), :] = y * lax.rsqrt(jnp.mean(y * y, axis=-1, keepdims=True) + EPS)
        advance(sf_ref, k, kdf, v, gcf)

    def sweep(row0, nchunks):
        def bwd(i, carry):
            bwd_step(row0 + (nchunks - 1 - i) * CHUNK)
            return carry

        def fwd(i, carry):
            fwd_step(row0 + i * CHUNK)
            return carry

        lax.fori_loop(0, nchunks, bwd, 0)
        lax.fori_loop(0, nchunks, fwd, 0)

    sf_ref[...] = jnp.zeros_like(sf_ref)
    sb_ref[...] = jnp.zeros_like(sb_ref)
    sweep(0, lc // CHUNK)
    sweep(lc, n // CHUNK)


def _retention(qkv, dec, gc, *, n_batch, lc, n, qk_width):
    s = lc + n
    t = qkv.shape[0]
    k_blk0 = qk_width // RET_DK
    v_blk0 = 2 * qk_width // RET_DV
    return pl.pallas_call(
        functools.partial(_retention_kernel, lc=lc, n=n),
        grid_spec=pltpu.PrefetchScalarGridSpec(
            num_scalar_prefetch=1,
            grid=(n_batch, RET_HEADS),
            in_specs=[
                pl.BlockSpec((s, RET_DK), lambda b, h, gc: (b, h)),
                pl.BlockSpec((s, RET_DK), lambda b, h, gc: (b, k_blk0 + h)),
                pl.BlockSpec((s, RET_DV), lambda b, h, gc: (b, v_blk0 + h)),
                pl.BlockSpec((1, 5, CHUNK, LANES), lambda b, h, gc: (h, 0, 0, 0)),
            ],
            out_specs=pl.BlockSpec((s, RET_DV), lambda b, h, gc: (b, h)),
            scratch_shapes=[pltpu.VMEM((RET_DK, RET_DV), F32), pltpu.VMEM((RET_DK, RET_DV), F32)],
        ),
        out_shape=jax.ShapeDtypeStruct((t, RET_HEADS * RET_DV), F32),
        compiler_params=_params(2),
        name="retention",
    )(gc, qkv, qkv, qkv, dec)


def _decay_tables(logit_f, logit_b):
    lg_f = jax.nn.log_sigmoid(logit_f.astype(F32))[:, None]
    lg_b = jax.nn.log_sigmoid(logit_b.astype(F32))[:, None]
    i = jnp.arange(CHUNK, dtype=F32)[None, :]
    vecs = jnp.stack([
        jnp.exp((i + 1) * lg_f), jnp.exp((CHUNK - 1 - i) * lg_f),
        jnp.exp((CHUNK - i) * lg_b), jnp.exp(i * lg_b)], axis=1)
    vecs = jnp.broadcast_to(vecs[..., None], vecs.shape + (LANES,))
    diff = i[0][:, None] - i[0][None, :]
    lower = jnp.where(diff >= 0, jnp.exp(jnp.where(diff >= 0, diff, 0.0)[None] * lg_f[:, :, None]), 0.0)
    upper = jnp.where(diff < 0, jnp.exp(jnp.where(diff < 0, -diff, 0.0)[None] * lg_b[:, :, None]), 0.0)
    dec = jnp.concatenate([vecs, (lower + upper)[:, None]], axis=1)
    gc = jnp.concatenate([jnp.exp(CHUNK * lg_f), jnp.exp(CHUNK * lg_b)], axis=1)
    return dec, gc


def _mix_kernel(x_ref, y_ref, rest_ref, mod_ref, cw_ref, wo_ref, o_ref, *, tpb, ret_width, conv_width):
    is_ctx = (pl.program_id(0) % tpb) == 0
    tm = x_ref.shape[0]
    g = rest_ref[:, 0:ret_width]
    cb = rest_ref[:, ret_width:ret_width + conv_width]
    cc = rest_ref[:, ret_width + conv_width:ret_width + 2 * conv_width]
    cx = rest_ref[:, ret_width + 2 * conv_width:ret_width + 3 * conv_width]
    ret = (g * _sigmoid(g) * y_ref[...]).astype(BF16)
    u = cc * cx
    row = lax.broadcasted_iota(jnp.int32, u.shape, 0)
    pos = jnp.where(is_ctx, row, row % GRID_W)
    last = jnp.where(is_ctx, tm - 1, GRID_W - 1)
    u_prev = jnp.where(pos == 0, 0.0, pltpu.roll(u, 1, 0))
    u_next = jnp.where(pos == last, 0.0, pltpu.roll(u, tm - 1, 0))
    conv = (cb * (u_prev * cw_ref[0:1, :] + u * cw_ref[1:2, :] + u_next * cw_ref[2:3, :])).astype(BF16)
    mix = jnp.dot(ret, wo_ref[0:ret_width, :], preferred_element_type=F32)
    mix += jnp.dot(conv, wo_ref[ret_width:ret_width + conv_width, :], preferred_element_type=F32)
    o_ref[...] = x_ref[...] + mod_ref[0, 2:3, :] * mix


def _mix(xs, y, rest, mods, conv_w8, wo_bf16, *, tm, tpb, n_batch):
    t, d = xs.shape
    ret_width = y.shape[1]
    conv_width = conv_w8.shape[1]

    def seg(i):
        return jnp.where(i % tpb == 0, n_batch, i // tpb)

    return pl.pallas_call(
        functools.partial(_mix_kernel, tpb=tpb, ret_width=ret_width, conv_width=conv_width),
        grid=(t // tm,),
        in_specs=[
            pl.BlockSpec((tm, d), lambda i: (i, 0)),
            pl.BlockSpec((tm, ret_width), lambda i: (i, 0)),
            pl.BlockSpec((tm, rest.shape[1]), lambda i: (i, 0)),
            pl.BlockSpec((1, SUBLANES, d), lambda i: (seg(i), 0, 0)),
            pl.BlockSpec((SUBLANES, conv_width), lambda i: (0, 0)),
            pl.BlockSpec(wo_bf16.shape, lambda i: (0, 0)),
        ],
        out_specs=pl.BlockSpec((tm, d), lambda i: (i, 0)),
        out_shape=jax.ShapeDtypeStruct((t, d), F32),
        compiler_params=_params(1),
        name="mix",
    )(xs, y, rest, mods, conv_w8, wo_bf16)


def _router_kernel(x_ref, mod_ref, gain_ref, rw_ref, rb_ref, ei_ref, wt_ref, cnt_ref):
    h2 = _modulated_norm(x_ref[...], gain_ref[...], mod_ref[0, 3:4, :], mod_ref[0, 4:5, :])
    h_hi, h_lo = _split_bf16(h2)
    w_hi, w_lo = _split_bf16(rw_ref[...])
    logits = lax.dot_general(w_hi, h_hi, NT_DIMS, preferred_element_type=F32)
    logits += lax.dot_general(w_hi, h_lo, NT_DIMS, preferred_element_type=F32)
    logits += lax.dot_general(w_lo, h_hi, NT_DIMS, preferred_element_type=F32)
    logits += rb_ref[:, 0:1]
    n_exp, tt = logits.shape
    expert = lax.broadcasted_iota(jnp.int32, (n_exp, tt), 0).astype(F32)
    slot = lax.broadcasted_iota(jnp.int32, (SUBLANES, tt), 0)
    ei = jnp.zeros((SUBLANES, tt), F32)
    ev = jnp.zeros((SUBLANES, tt), F32)
    taken = jnp.zeros((n_exp, tt), F32)
    top = None
    denom = jnp.zeros((1, tt), F32)
    for k in range(TOP_K):
        m = jnp.max(logits, axis=0, keepdims=True)
        idx = jnp.min(jnp.where(logits == m, expert, float(n_exp)), axis=0, keepdims=True)
        hit = expert == idx
        taken += hit.astype(F32)
        logits = jnp.where(hit, -jnp.inf, logits)
        if k == 0:
            top = m
        e = jnp.exp(m - top)
        denom += e
        ei = jnp.where(slot == k, idx, ei)
        ev = jnp.where(slot == k, e, ev)
    ei_ref[0] = ei.astype(jnp.int32)
    wt_ref[0] = jnp.where(slot < TOP_K, ev / denom, 0.0)
    cnt = jnp.sum(taken, axis=1, keepdims=True)
    cnt_ref[0] = jnp.broadcast_to(cnt, (n_exp, LANES)).astype(jnp.int32)


def _router(xs, mods, gain, rw_t, rb_col, *, tm, tpb, n_batch):
    t, d = xs.shape
    ntiles = t // tm
    n_exp = rw_t.shape[0]

    def seg(i):
        return jnp.where(i % tpb == 0, n_batch, i // tpb)

    return pl.pallas_call(
        _router_kernel,
        grid=(ntiles,),
        in_specs=[
            pl.BlockSpec((tm, d), lambda i: (i, 0)),
            pl.BlockSpec((1, SUBLANES, d), lambda i: (seg(i), 0, 0)),
            pl.BlockSpec((1, d), lambda i: (0, 0)),
            pl.BlockSpec((n_exp, d), lambda i: (0, 0)),
            pl.BlockSpec((n_exp, LANES), lambda i: (0, 0)),
        ],
        out_specs=[
            pl.BlockSpec((1, SUBLANES, tm), lambda i: (i, 0, 0)),
            pl.BlockSpec((1, SUBLANES, tm), lambda i: (i, 0, 0)),
            pl.BlockSpec((1, n_exp, LANES), lambda i: (i, 0, 0)),
        ],
        out_shape=[
            jax.ShapeDtypeStruct((ntiles, SUBLANES, tm), jnp.int32),
            jax.ShapeDtypeStruct((ntiles, SUBLANES, tm), F32),
            jax.ShapeDtypeStruct((ntiles, n_exp, LANES), jnp.int32),
        ],
        compiler_params=_params(1),
        name="router",
    )(xs, mods, gain, rw_t, rb_col)


def _routing_plan(cnt, *, tm, sorted_rows):
    ntiles, n_exp = cnt.shape
    seg_len = (cnt + ROW_ALIGN - 1) // ROW_ALIGN * ROW_ALIGN
    off = jnp.cumsum(seg_len, axis=1) - seg_len
    tile_rows = jnp.sum(seg_len, axis=1)
    before = jnp.cumsum(seg_len, axis=0) - seg_len
    filled = jnp.sum(seg_len, axis=0)
    region = (filled + EXPERT_TILE - 1) // EXPERT_TILE * EXPERT_TILE
    region_end = jnp.cumsum(region)
    region_start = region_end - region
    base = region_start[None, :] + before

    nb_max = _max_blocks(ntiles, sorted_rows, n_exp)
    dump0 = nb_max * EXPERT_TILE
    nch = sorted_rows // ROW_ALIGN
    q = jnp.arange(nch, dtype=jnp.int32) * ROW_ALIGN
    seg_end = off + seg_len
    seg_of = jnp.sum((seg_end[:, None, :] <= q[None, :, None]).astype(jnp.int32), axis=2)
    seg_of = jnp.minimum(seg_of, n_exp - 1)
    valid = q[None, :] < tile_rows[:, None]
    row = jnp.take_along_axis(base, seg_of, axis=1) + q[None, :] - jnp.take_along_axis(off, seg_of, axis=1)
    chunk_dst = jnp.where(valid, row, dump0 + q[None, :]).astype(jnp.int32)
    chunk_src = jnp.where(valid, row, 0).astype(jnp.int32)

    fill_q = jnp.arange(EXPERT_TILE // ROW_ALIGN, dtype=jnp.int32) * ROW_ALIGN
    fill_row = filled[:, None] + fill_q[None, :]
    fill_dump = dump0 + sorted_rows + (jnp.arange(n_exp, dtype=jnp.int32)[:, None] * EXPERT_TILE + fill_q[None, :])
    fill_dst = jnp.where(fill_row < region[:, None], region_start[:, None] + fill_row, fill_dump)
    fill_dst = fill_dst.reshape(-1).astype(jnp.int32)

    blk = jnp.arange(nb_max, dtype=jnp.int32) * EXPERT_TILE
    block_e = jnp.sum((region_end[None, :] <= blk[:, None]).astype(jnp.int32), axis=1)
    block_e = jnp.minimum(block_e, n_exp - 1).astype(jnp.int32)
    n_used = (region_end[-1] // EXPERT_TILE).astype(jnp.int32).reshape(1)
    return off, chunk_dst, chunk_src, fill_dst, block_e, n_used


def _max_blocks(ntiles, sorted_rows, n_exp):
    return -(-(ntiles * sorted_rows + n_exp * EXPERT_TILE) // EXPERT_TILE)


def _sorted_positions(ei, off_col, tt):
    n_exp = off_col.shape[0]
    expert = lax.broadcasted_iota(jnp.int32, (n_exp, tt), 0)
    hits = [expert == ei[k:k + 1, :] for k in range(TOP_K)]
    chosen = sum(h.astype(F32) for h in hits).astype(BF16)
    t_row = lax.broadcasted_iota(jnp.int32, (tt, tt), 0)
    t_col = lax.broadcasted_iota(jnp.int32, (tt, tt), 1)
    earlier = (t_row < t_col).astype(BF16)
    rank = jnp.dot(chosen, earlier, preferred_element_type=F32) + off_col
    return [jnp.sum(jnp.where(h, rank, 0.0), axis=0, keepdims=True) for h in hits]


def _dispatch_kernel(cd_ref, fd_ref, x_ref, mod_ref, gain_ref, ei_ref, off_ref, xb_ref, pos_ref,
                     buf, zeros, sem, fill_sem):
    step = pl.program_id(0)
    tt = x_ref.shape[0]
    sorted_rows = buf.shape[0]
    nch = sorted_rows // ROW_ALIGN

    def fill_copy(j):
        dst = pl.multiple_of(fd_ref[j], ROW_ALIGN)
        return pltpu.make_async_copy(zeros, xb_ref.at[pl.ds(dst, ROW_ALIGN), :], fill_sem)

    @pl.when(step == 0)
    def _():
        zeros[...] = jnp.zeros_like(zeros)

        def start(j, c):
            fill_copy(j).start()
            return c

        def wait(j, c):
            fill_copy(j).wait()
            return c

        lax.fori_loop(0, fd_ref.shape[0], start, 0)
        lax.fori_loop(0, fd_ref.shape[0], wait, 0)

    h2 = _modulated_norm(x_ref[...], gain_ref[...], mod_ref[0, 3:4, :], mod_ref[0, 4:5, :]).astype(BF16)
    pos = _sorted_positions(ei_ref[0], off_ref[0][:, 0:1], tt)
    slot = lax.broadcasted_iota(jnp.int32, (SUBLANES, tt), 0)
    pos8 = jnp.zeros((SUBLANES, tt), F32)
    for k in range(TOP_K):
        pos8 = jnp.where(slot == k, pos[k], pos8)
    pos_ref[0] = pos8
    r = lax.broadcasted_iota(jnp.int32, (sorted_rows, tt), 0).astype(F32)
    perm = sum((r == pos[k]).astype(F32) for k in range(TOP_K)).astype(BF16)
    buf[...] = jnp.dot(perm, h2, preferred_element_type=F32)

    def chunk_copy(j):
        src = pl.multiple_of(j * ROW_ALIGN, ROW_ALIGN)
        dst = pl.multiple_of(cd_ref[0, 0, j], ROW_ALIGN)
        return pltpu.make_async_copy(buf.at[pl.ds(src, ROW_ALIGN), :], xb_ref.at[pl.ds(dst, ROW_ALIGN), :], sem)

    def start(j, c):
        chunk_copy(j).start()
        return c

    def wait(j, c):
        chunk_copy(j).wait()
        return c

    lax.fori_loop(0, nch, start, 0)
    lax.fori_loop(0, nch, wait, 0)


def _dispatch(xs, mods, gain, ei, off_col, chunk_dst, fill_dst, *, tm, tpb, n_batch, sorted_rows, total_rows):
    t, d = xs.shape
    ntiles = t // tm
    n_exp = off_col.shape[1]
    nch = sorted_rows // ROW_ALIGN

    def seg(i):
        return jnp.where(i % tpb == 0, n_batch, i // tpb)

    return pl.pallas_call(
        _dispatch_kernel,
        grid=(ntiles,),
        in_specs=[
            pl.BlockSpec((1, 1, nch), lambda i: (i, 0, 0), memory_space=pltpu.SMEM),
            pl.BlockSpec(memory_space=pltpu.SMEM),
            pl.BlockSpec((tm, d), lambda i: (i, 0)),
            pl.BlockSpec((1, SUBLANES, d), lambda i: (seg(i), 0, 0)),
            pl.BlockSpec((1, d), lambda i: (0, 0)),
            pl.BlockSpec((1, SUBLANES, tm), lambda i: (i, 0, 0)),
            pl.BlockSpec((1, n_exp, LANES), lambda i: (i, 0, 0)),
        ],
        out_specs=[
            pl.BlockSpec(memory_space=pl.ANY),
            pl.BlockSpec((1, SUBLANES, tm), lambda i: (i, 0, 0)),
        ],
        out_shape=[
            jax.ShapeDtypeStruct((total_rows, d), F32),
            jax.ShapeDtypeStruct((ntiles, SUBLANES, tm), F32),
        ],
        scratch_shapes=[
            pltpu.VMEM((sorted_rows, d), F32),
            pltpu.VMEM((ROW_ALIGN, d), F32),
            pltpu.SemaphoreType.DMA(()),
            pltpu.SemaphoreType.DMA(()),
        ],
        compiler_params=_params(1),
        name="dispatch",
    )(chunk_dst.reshape(ntiles, 1, nch), fill_dst, xs, mods, gain, ei, off_col)


def _experts_kernel(be_ref, nu_ref, xb_ref, wu_ref, bu_ref, wd_ref, bd_ref, yb_ref, wu_s, wd_s, *, d_ff):
    i = pl.program_id(0)
    prev = be_ref[jnp.maximum(i - 1, 0)]
    fresh = jnp.logical_or(i == 0, be_ref[i] != prev)

    @pl.when(jnp.logical_and(fresh, i < nu_ref[0]))
    def _():
        wu_s[...] = wu_ref[0].astype(BF16)
        wd_s[...] = wd_ref[0].astype(BF16)

    @pl.when(i < nu_ref[0])
    def _():
        u = jnp.dot(xb_ref[...].astype(BF16), wu_s[...], preferred_element_type=F32) + bu_ref[0]
        gate = jnp.minimum(u[:, :d_ff], SWIGLU_LIMIT)
        lin = jnp.clip(u[:, d_ff:], -SWIGLU_LIMIT, SWIGLU_LIMIT)
        act = (gate * _sigmoid(SWIGLU_ALPHA * gate) * (lin + 1.0)).astype(BF16)
        yb_ref[...] = jnp.dot(act, wd_s[...], preferred_element_type=F32) + bd_ref[0]


def _experts(xb, block_e, n_used, w_up, b_up, w_down, b_down, *, nb_max):
    d = xb.shape[1]
    n_exp, _, two_ff = w_up.shape
    d_ff = two_ff // 2

    def blk(i, be, nu):
        return jnp.minimum(i, nu[0] - 1)

    def exp(i, be, nu):
        return be[jnp.minimum(i, nu[0] - 1)]

    return pl.pallas_call(
        functools.partial(_experts_kernel, d_ff=d_ff),
        grid_spec=pltpu.PrefetchScalarGridSpec(
            num_scalar_prefetch=2,
            grid=(nb_max,),
            in_specs=[
                pl.BlockSpec((EXPERT_TILE, d), lambda i, be, nu: (blk(i, be, nu), 0)),
                pl.BlockSpec((1, d, two_ff), lambda i, be, nu: (exp(i, be, nu), 0, 0)),
                pl.BlockSpec((1, 1, two_ff), lambda i, be, nu: (exp(i, be, nu), 0, 0)),
                pl.BlockSpec((1, d_ff, d), lambda i, be, nu: (exp(i, be, nu), 0, 0)),
                pl.BlockSpec((1, 1, d), lambda i, be, nu: (exp(i, be, nu), 0, 0)),
            ],
            out_specs=pl.BlockSpec((EXPERT_TILE, d), lambda i, be, nu: (blk(i, be, nu), 0)),
            scratch_shapes=[pltpu.VMEM((d, two_ff), BF16), pltpu.VMEM((d_ff, d), BF16)],
        ),
        out_shape=jax.ShapeDtypeStruct((nb_max * EXPERT_TILE, d), F32),
        compiler_params=_params(1),
        name="experts",
    )(block_e, n_used, xb, w_up, b_up.reshape(n_exp, 1, two_ff), w_down, b_down.reshape(n_exp, 1, d))


def _combine_kernel(cs_ref, x_ref, mod_ref, pos_ref, wt_ref, yb_ref, o_ref, buf, sem):
    tt = x_ref.shape[0]
    sorted_rows = buf.shape[0]
    nch = sorted_rows // ROW_ALIGN

    def chunk_copy(j):
        src = pl.multiple_of(cs_ref[0, 0, j], ROW_ALIGN)
        dst = pl.multiple_of(j * ROW_ALIGN, ROW_ALIGN)
        return pltpu.make_async_copy(yb_ref.at[pl.ds(src, ROW_ALIGN), :], buf.at[pl.ds(dst, ROW_ALIGN), :], sem)

    def start(j, c):
        chunk_copy(j).start()
        return c

    def wait(j, c):
        chunk_copy(j).wait()
        return c

    lax.fori_loop(0, nch, start, 0)
    r = lax.broadcasted_iota(jnp.int32, (sorted_rows, tt), 0).astype(F32)
    pos = pos_ref[0]
    wt = wt_ref[0]
    sel = sum(jnp.where(r == pos[k:k + 1, :], wt[k:k + 1, :], 0.0) for k in range(TOP_K)).astype(BF16)
    lax.fori_loop(0, nch, wait, 0)
    y2 = lax.dot_general(sel, buf[...].astype(BF16), TN_DIMS, preferred_element_type=F32)
    o_ref[...] = x_ref[...] + mod_ref[0, 5:6, :] * y2


def _combine(xs, mods, pos, wt, chunk_src, yb, *, tm, tpb, n_batch, sorted_rows):
    t, d = xs.shape
    ntiles = t // tm
    nch = sorted_rows // ROW_ALIGN

    def seg(i):
        return jnp.where(i % tpb == 0, n_batch, i // tpb)

    return pl.pallas_call(
        _combine_kernel,
        grid=(ntiles,),
        in_specs=[
            pl.BlockSpec((1, 1, nch), lambda i: (i, 0, 0), memory_space=pltpu.SMEM),
            pl.BlockSpec((tm, d), lambda i: (i, 0)),
            pl.BlockSpec((1, SUBLANES, d), lambda i: (seg(i), 0, 0)),
            pl.BlockSpec((1, SUBLANES, tm), lambda i: (i, 0, 0)),
            pl.BlockSpec((1, SUBLANES, tm), lambda i: (i, 0, 0)),
            pl.BlockSpec(memory_space=pl.ANY),
        ],
        out_specs=pl.BlockSpec((tm, d), lambda i: (i, 0)),
        out_shape=jax.ShapeDtypeStruct((t, d), F32),
        scratch_shapes=[pltpu.VMEM((sorted_rows, d), F32), pltpu.SemaphoreType.DMA(())],
        compiler_params=_params(1),
        name="combine",
    )(chunk_src.reshape(ntiles, 1, nch), xs, mods, pos, wt, yb)


def _final_norm_kernel(x_ref, gain_ref, o_ref):
    x = x_ref[...]
    o_ref[0] = x * lax.rsqrt(jnp.mean(x * x, axis=-1, keepdims=True) + EPS) * gain_ref[...]


def _final_norm(xs, gain, *, tm, tpb, n_batch, n):
    d = xs.shape[1]
    lat_tiles = n // tm
    return pl.pallas_call(
        _final_norm_kernel,
        grid=(n_batch, lat_tiles),
        in_specs=[
            pl.BlockSpec((tm, d), lambda b, j: (b * tpb + 1 + j, 0)),
            pl.BlockSpec((1, d), lambda b, j: (0, 0)),
        ],
        out_specs=pl.BlockSpec((1, tm, d), lambda b, j: (b, j, 0)),
        out_shape=jax.ShapeDtypeStruct((n_batch, n, d), F32),
        compiler_params=_params(2),
        name="final_norm",
    )(xs, gain)


def _rope_tables(n, tm):
    t = jnp.arange(n, dtype=jnp.int32)
    m = RET_DK // 4
    freqs = ROPE_BASE ** (-jnp.arange(m, dtype=F32) / m)
    ang_r = (t // GRID_W).astype(F32)[:, None] * freqs[None, :]
    ang_c = (t % GRID_W).astype(F32)[:, None] * freqs[None, :]
    cos = jnp.concatenate([jnp.cos(ang_r)] * 2 + [jnp.cos(ang_c)] * 2, axis=1)
    sin = jnp.concatenate([-jnp.sin(ang_r), jnp.sin(ang_r), -jnp.sin(ang_c), jnp.sin(ang_c)], axis=1)
    cos = jnp.concatenate([cos, jnp.ones((tm, RET_DK), F32)], axis=0)
    sin = jnp.concatenate([sin, jnp.zeros((tm, RET_DK), F32)], axis=0)
    return cos, sin


def kernel(x, c, ctx, c_ctx, w_mod, b_mod, norm_mix, norm_ffn, w_in, ret_decay_f, ret_decay_b, conv_w, w_out,
           router_w, router_b, w_up, b_up, w_down, b_down, norm_final):
    n_batch, n, d = x.shape
    lc = ctx.shape[1]
    depth = w_mod.shape[0]
    tm = lc
    assert tm % 256 == 0 and n % tm == 0 and n % GRID_W == 0 and tm % GRID_W == 0
    assert n % CHUNK == 0 and lc % CHUNK == 0 and n_batch + 1 <= SUBLANES
    tpb = (lc + n) // tm
    qk_width = RET_HEADS * RET_DK
    qkv_width = 2 * qk_width + RET_HEADS * RET_DV
    n_exp = router_w.shape[2]
    sorted_rows = TOP_K * tm + n_exp * ROW_ALIGN
    ntiles = n_batch * tpb
    nb_max = _max_blocks(ntiles, sorted_rows, n_exp)
    total_rows = nb_max * EXPERT_TILE + sorted_rows + n_exp * EXPERT_TILE

    xs = jnp.concatenate([ctx, x], axis=1).reshape(n_batch * (lc + n), d)
    cond = jnp.zeros((SUBLANES, d), F32).at[:n_batch].set(c).at[n_batch].set(c_ctx)
    mod_all = _modulation(cond, w_mod, b_mod).reshape(depth, SUBLANES, 6, d)
    mod_all = jnp.pad(mod_all, ((0, 0), (0, 0), (0, SUBLANES - 6), (0, 0)))
    cos_t, sin_t = _rope_tables(n, tm)

    for layer in range(depth):
        mods = mod_all[layer]
        dec, gc = _decay_tables(ret_decay_f[layer], ret_decay_b[layer])
        qkv, rest = _inproj(xs, mods, norm_mix[layer][None, :], w_in[layer].astype(BF16), cos_t, sin_t,
                            tm=tm, tpb=tpb, n_batch=n_batch, qk_width=qk_width, qkv_width=qkv_width)
        y = _retention(qkv, dec, gc, n_batch=n_batch, lc=lc, n=n, qk_width=qk_width)
        conv_w8 = jnp.pad(conv_w[layer], ((0, SUBLANES - conv_w.shape[1]), (0, 0)))
        xs = _mix(xs, y, rest, mods, conv_w8, w_out[layer].astype(BF16), tm=tm, tpb=tpb, n_batch=n_batch)

        gain2 = norm_ffn[layer][None, :]
        rb_col = jnp.broadcast_to(router_b[layer][:, None], (n_exp, LANES))
        ei, wt, cnt = _router(xs, mods, gain2, router_w[layer].T, rb_col, tm=tm, tpb=tpb, n_batch=n_batch)
        off, chunk_dst, chunk_src, fill_dst, block_e, n_used = _routing_plan(
            cnt[:, :, 0], tm=tm, sorted_rows=sorted_rows)
        off_col = jnp.broadcast_to(off.astype(F32)[:, :, None], (ntiles, n_exp, LANES))
        xb, pos = _dispatch(xs, mods, gain2, ei, off_col, chunk_dst, fill_dst, tm=tm, tpb=tpb, n_batch=n_batch,
                            sorted_rows=sorted_rows, total_rows=total_rows)
        yb = _experts(xb, block_e, n_used, w_up[layer], b_up[layer], w_down[layer], b_down[layer], nb_max=nb_max)
        xs = _combine(xs, mods, pos, wt, chunk_src, yb, tm=tm, tpb=tpb, n_batch=n_batch, sorted_rows=sorted_rows)

    return _final_norm(xs, norm_final[None, :], tm=tm, tpb=tpb, n_batch=n_batch, n=n)
```

```python
import functools

import jax
import jax.numpy as jnp
from jax import lax
from jax.experimental import pallas as pl
from jax.experimental.pallas import tpu as pltpu

GRID_W = 64
RET_HEADS = 4
RET_DK = 128
RET_DV = 256
RET_CHUNK = 256
N_EXPERTS = 32
TOP_K = 4
SWIGLU_LIMIT = 7.0
SWIGLU_ALPHA = 1.702
ROPE_BASE = 10000.0
EPS = 1e-6

SUBLANES = 8
LANES = 128
VMEM_LIMIT = 56 * 1024 * 1024

EXPERT_TILE = 256
ROW_ALIGN = SUBLANES

F32 = jnp.float32
BF16 = jnp.bfloat16
NT_DIMS = (((1,), (1,)), ((), ()))
TN_DIMS = (((0,), (0,)), ((), ()))


def _params(n_axes=1):
    return pltpu.CompilerParams(dimension_semantics=("arbitrary",) * n_axes, vmem_limit_bytes=VMEM_LIMIT)


def _modulated_norm(x, gain, shift, scale):
    ms = jnp.mean(x * x, axis=-1, keepdims=True)
    return x * lax.rsqrt(ms + EPS) * gain * (1.0 + scale) + shift


def _sigmoid(z):
    return 1.0 / (1.0 + jnp.exp(-z))


def _split_bf16(a):
    hi = a.astype(BF16)
    lo = (a - hi.astype(F32)).astype(BF16)
    return hi, lo


def _modulation_kernel(c_ref, w_ref, b_ref, o_ref):
    c = c_ref[...]
    s = c * _sigmoid(c)
    s_hi, s_lo = _split_bf16(s)
    w_hi, w_lo = _split_bf16(w_ref[0])
    acc = jnp.dot(s_hi, w_hi, preferred_element_type=F32)
    acc += jnp.dot(s_hi, w_lo, preferred_element_type=F32)
    acc += jnp.dot(s_lo, w_hi, preferred_element_type=F32)
    o_ref[0] = acc + b_ref[0]


def _modulation(cond, w_mod, b_mod):
    depth, d, six_d = w_mod.shape
    nblk = six_d // d
    return pl.pallas_call(
        _modulation_kernel,
        grid=(depth, nblk),
        in_specs=[
            pl.BlockSpec((SUBLANES, d), lambda l, j: (0, 0)),
            pl.BlockSpec((1, d, d), lambda l, j: (l, 0, j)),
            pl.BlockSpec((1, 1, d), lambda l, j: (l, 0, j)),
        ],
        out_specs=pl.BlockSpec((1, SUBLANES, d), lambda l, j: (l, 0, j)),
        out_shape=jax.ShapeDtypeStruct((depth, SUBLANES, six_d), F32),
        compiler_params=_params(2),
        name="modulation",
    )(cond, w_mod, b_mod.reshape(depth, 1, six_d))


def _inproj_kernel(x_ref, mod_ref, gain_ref, w_ref, cos_ref, sin_ref, qkv_ref, rest_ref, *, qk_width, q_scale):
    h = _modulated_norm(x_ref[...], gain_ref[...], mod_ref[0, 0:1, :], mod_ref[0, 1:2, :]).astype(BF16)
    reps = qk_width // RET_DK
    cos = jnp.concatenate([cos_ref[...]] * reps, axis=1)
    sin = jnp.concatenate([sin_ref[...]] * reps, axis=1)
    lane = lax.broadcasted_iota(jnp.int32, cos.shape, 1)
    first_half = (lane % (RET_DK // 2)) < (RET_DK // 4)
    for part, scale in ((0, q_scale), (1, 1.0)):
        c0 = part * qk_width
        p = jnp.dot(h, w_ref[:, c0:c0 + qk_width], preferred_element_type=F32)
        partner = jnp.where(first_half, pltpu.roll(p, qk_width - RET_DK // 4, 1), pltpu.roll(p, RET_DK // 4, 1))
        r = p * cos + partner * sin
        if scale != 1.0:
            r = r * scale
        qkv_ref[:, c0:c0 + qk_width] = r.astype(BF16)
    qkv_width = qkv_ref.shape[1]
    step = 512
    for c0 in range(2 * qk_width, qkv_width, step):
        qkv_ref[:, c0:c0 + step] = jnp.dot(h, w_ref[:, c0:c0 + step], preferred_element_type=F32).astype(BF16)
    for c0 in range(0, rest_ref.shape[1], step):
        rest_ref[:, c0:c0 + step] = jnp.dot(
            h, w_ref[:, qkv_width + c0:qkv_width + c0 + step], preferred_element_type=F32)


def _inproj(xs, mods, gain, w_bf16, cos_t, sin_t, *, tm, tpb, n_batch, qk_width, qkv_width):
    t, d = xs.shape
    in_width = w_bf16.shape[1]
    rest_width = in_width - qkv_width
    pos_blocks = cos_t.shape[0] // tm - 1

    def seg(i):
        return jnp.where(i % tpb == 0, n_batch, i // tpb)

    def pos(i):
        j = i % tpb
        return jnp.where(j == 0, pos_blocks, j - 1)

    return pl.pallas_call(
        functools.partial(_inproj_kernel, qk_width=qk_width, q_scale=RET_DK ** -0.5),
        grid=(t // tm,),
        in_specs=[
            pl.BlockSpec((tm, d), lambda i: (i, 0)),
            pl.BlockSpec((1, SUBLANES, d), lambda i: (seg(i), 0, 0)),
            pl.BlockSpec((1, d), lambda i: (0, 0)),
            pl.BlockSpec((d, in_width), lambda i: (0, 0)),
            pl.BlockSpec((tm, RET_DK), lambda i: (pos(i), 0)),
            pl.BlockSpec((tm, RET_DK), lambda i: (pos(i), 0)),
        ],
        out_specs=[
            pl.BlockSpec((tm, qkv_width), lambda i: (i, 0)),
            pl.BlockSpec((tm, rest_width), lambda i: (i, 0)),
        ],
        out_shape=[
            jax.ShapeDtypeStruct((t, qkv_width), BF16),
            jax.ShapeDtypeStruct((t, rest_width), F32),
        ],
        compiler_params=_params(1),
        name="inproj",
    )(xs, mods, gain, w_bf16, cos_t, sin_t)


def _retention_kernel(gc_ref, q_ref, k_ref, v_ref, dec_ref, dmat_ref, y_ref, sf_ref, sb_ref, st_ref, *, lc, n):
    head = pl.program_id(1)
    c = RET_CHUNK
    qd = dec_ref[0, :, 0:2 * RET_DK]
    kdf = dec_ref[0, :, 2 * RET_DK:3 * RET_DK]
    kdb = dec_ref[0, :, 3 * RET_DK:4 * RET_DK]
    gcf = gc_ref[head, 0]
    gcb = gc_ref[head, 1]

    def rows(chunk):
        return pl.ds(pl.multiple_of(chunk * c, c), c)

    def advance(s_ref, chunk, kd, gc):
        kdec = (k_ref[rows(chunk), :].astype(F32) * kd).astype(BF16)
        kv = lax.dot_general(kdec, v_ref[rows(chunk), :], TN_DIMS, preferred_element_type=F32)
        s_ref[...] = gc * s_ref[...] + kv

    def sweep(chunk0, nchunks):
        def states(i, carry):
            cf = chunk0 + i
            cb = chunk0 + nchunks - 1 - i
            st_ref[cf, 0:RET_DK, :] = sf_ref[...].astype(BF16)
            st_ref[cb, RET_DK:2 * RET_DK, :] = sb_ref[...].astype(BF16)
            advance(sf_ref, cf, kdf, gcf)
            advance(sb_ref, cb, kdb, gcb)
            return carry

        def outputs(i, carry):
            r = rows(chunk0 + i)
            q, k, v = q_ref[r, :], k_ref[r, :], v_ref[r, :]
            scores = lax.dot_general(q, k, NT_DIMS, preferred_element_type=F32)
            qdec = (jnp.concatenate([q, q], axis=1).astype(F32) * qd).astype(BF16)
            y = jnp.dot((scores * dmat_ref[0]).astype(BF16), v, preferred_element_type=F32)
            y += jnp.dot(qdec, st_ref[chunk0 + i], preferred_element_type=F32)
            y_ref[r, :] = y * lax.rsqrt(jnp.mean(y * y, axis=-1, keepdims=True) + EPS)
            return carry

        lax.fori_loop(0, nchunks, states, 0)
        lax.fori_loop(0, nchunks, outputs, 0)

    sf_ref[...] = jnp.zeros_like(sf_ref)
    sb_ref[...] = jnp.zeros_like(sb_ref)
    sweep(0, lc // c)
    sweep(lc // c, n // c)


def _retention(qkv, dec, dmat, gc, *, n_batch, lc, n, qk_width):
    s = lc + n
    t = qkv.shape[0]
    k_blk0 = qk_width // RET_DK
    v_blk0 = 2 * qk_width // RET_DV
    return pl.pallas_call(
        functools.partial(_retention_kernel, lc=lc, n=n),
        grid_spec=pltpu.PrefetchScalarGridSpec(
            num_scalar_prefetch=1,
            grid=(n_batch, RET_HEADS),
            in_specs=[
                pl.BlockSpec((s, RET_DK), lambda b, h, gc: (b, h)),
                pl.BlockSpec((s, RET_DK), lambda b, h, gc: (b, k_blk0 + h)),
                pl.BlockSpec((s, RET_DV), lambda b, h, gc: (b, v_blk0 + h)),
                pl.BlockSpec((1, RET_CHUNK, 4 * RET_DK), lambda b, h, gc: (h, 0, 0)),
                pl.BlockSpec((1, RET_CHUNK, RET_CHUNK), lambda b, h, gc: (h, 0, 0)),
            ],
            out_specs=pl.BlockSpec((s, RET_DV), lambda b, h, gc: (b, h)),
            scratch_shapes=[
                pltpu.VMEM((RET_DK, RET_DV), F32),
                pltpu.VMEM((RET_DK, RET_DV), F32),
                pltpu.VMEM((s // RET_CHUNK, 2 * RET_DK, RET_DV), BF16),
            ],
        ),
        out_shape=jax.ShapeDtypeStruct((t, RET_HEADS * RET_DV), F32),
        compiler_params=_params(2),
        name="retention",
    )(gc, qkv, qkv, qkv, dec, dmat)


def _decay_tables(logit_f, logit_b):
    c = RET_CHUNK
    lg_f = jax.nn.log_sigmoid(logit_f.astype(F32))[:, None]
    lg_b = jax.nn.log_sigmoid(logit_b.astype(F32))[:, None]
    i = jnp.arange(c, dtype=F32)[None, :]
    vecs = jnp.stack([
        jnp.exp((i + 1) * lg_f), jnp.exp((c - i) * lg_b),
        jnp.exp((c - 1 - i) * lg_f), jnp.exp(i * lg_b)], axis=2)
    dec = jnp.broadcast_to(vecs[..., None], vecs.shape + (RET_DK,)).reshape(vecs.shape[0], c, 4 * RET_DK)
    diff = i[0][:, None] - i[0][None, :]
    lower = jnp.where(diff >= 0, jnp.exp(jnp.where(diff >= 0, diff, 0.0)[None] * lg_f[:, :, None]), 0.0)
    upper = jnp.where(diff < 0, jnp.exp(jnp.where(diff < 0, -diff, 0.0)[None] * lg_b[:, :, None]), 0.0)
    gc = jnp.concatenate([jnp.exp(c * lg_f), jnp.exp(c * lg_b)], axis=1)
    return dec, lower + upper, gc


def _mix_kernel(x_ref, y_ref, rest_ref, mod_ref, cw_ref, wo_ref, o_ref, *, tpb, ret_width, conv_width):
    is_ctx = (pl.program_id(0) % tpb) == 0
    tm = x_ref.shape[0]
    g = rest_ref[:, 0:ret_width]
    cb = rest_ref[:, ret_width:ret_width + conv_width]
    cc = rest_ref[:, ret_width + conv_width:ret_width + 2 * conv_width]
    cx = rest_ref[:, ret_width + 2 * conv_width:ret_width + 3 * conv_width]
    ret = (g * _sigmoid(g) * y_ref[...]).astype(BF16)
    u = cc * cx
    row = lax.broadcasted_iota(jnp.int32, u.shape, 0)
    pos = jnp.where(is_ctx, row, row % GRID_W)
    last = jnp.where(is_ctx, tm - 1, GRID_W - 1)
    u_prev = jnp.where(pos == 0, 0.0, pltpu.roll(u, 1, 0))
    u_next = jnp.where(pos == last, 0.0, pltpu.roll(u, tm - 1, 0))
    conv = (cb * (u_prev * cw_ref[0:1, :] + u * cw_ref[1:2, :] + u_next * cw_ref[2:3, :])).astype(BF16)
    mix = jnp.dot(ret, wo_ref[0:ret_width, :], preferred_element_type=F32)
    mix += jnp.dot(conv, wo_ref[ret_width:ret_width + conv_width, :], preferred_element_type=F32)
    o_ref[...] = x_ref[...] + mod_ref[0, 2:3, :] * mix


def _mix(xs, y, rest, mods, conv_w8, wo_bf16, *, tm, tpb, n_batch):
    t, d = xs.shape
    ret_width = y.shape[1]
    conv_width = conv_w8.shape[1]

    def seg(i):
        return jnp.where(i % tpb == 0, n_batch, i // tpb)

    return pl.pallas_call(
        functools.partial(_mix_kernel, tpb=tpb, ret_width=ret_width, conv_width=conv_width),
        grid=(t // tm,),
        in_specs=[
            pl.BlockSpec((tm, d), lambda i: (i, 0)),
            pl.BlockSpec((tm, ret_width), lambda i: (i, 0)),
            pl.BlockSpec((tm, rest.shape[1]), lambda i: (i, 0)),
            pl.BlockSpec((1, SUBLANES, d), lambda i: (seg(i), 0, 0)),
            pl.BlockSpec((SUBLANES, conv_width), lambda i: (0, 0)),
            pl.BlockSpec(wo_bf16.shape, lambda i: (0, 0)),
        ],
        out_specs=pl.BlockSpec((tm, d), lambda i: (i, 0)),
        out_shape=jax.ShapeDtypeStruct((t, d), F32),
        compiler_params=_params(1),
        name="mix",
    )(xs, y, rest, mods, conv_w8, wo_bf16)


def _router_kernel(x_ref, mod_ref, gain_ref, rw_ref, rb_ref, ei_ref, wt_ref, cnt_ref):
    h2 = _modulated_norm(x_ref[...], gain_ref[...], mod_ref[0, 3:4, :], mod_ref[0, 4:5, :])
    h_hi, h_lo = _split_bf16(h2)
    w_hi, w_lo = _split_bf16(rw_ref[...])
    logits = lax.dot_general(w_hi, h_hi, NT_DIMS, preferred_element_type=F32)
    logits += lax.dot_general(w_hi, h_lo, NT_DIMS, preferred_element_type=F32)
    logits += lax.dot_general(w_lo, h_hi, NT_DIMS, preferred_element_type=F32)
    logits += rb_ref[:, 0:1]
    n_exp, tt = logits.shape
    expert = lax.broadcasted_iota(jnp.int32, (n_exp, tt), 0).astype(F32)
    slot = lax.broadcasted_iota(jnp.int32, (SUBLANES, tt), 0)
    ei = jnp.zeros((SUBLANES, tt), F32)
    ev = jnp.zeros((SUBLANES, tt), F32)
    taken = jnp.zeros((n_exp, tt), F32)
    top = None
    denom = jnp.zeros((1, tt), F32)
    for k in range(TOP_K):
        m = jnp.max(logits, axis=0, keepdims=True)
        idx = jnp.min(jnp.where(logits == m, expert, float(n_exp)), axis=0, keepdims=True)
        hit = expert == idx
        taken += hit.astype(F32)
        logits = jnp.where(hit, -jnp.inf, logits)
        if k == 0:
            top = m
        e = jnp.exp(m - top)
        denom += e
        ei = jnp.where(slot == k, idx, ei)
        ev = jnp.where(slot == k, e, ev)
    ei_ref[0] = ei.astype(jnp.int32)
    wt_ref[0] = jnp.where(slot < TOP_K, ev / denom, 0.0)
    cnt = jnp.sum(taken, axis=1, keepdims=True)
    cnt_ref[0] = jnp.broadcast_to(cnt, (n_exp, LANES)).astype(jnp.int32)


def _router(xs, mods, gain, rw_t, rb_col, *, tm, tpb, n_batch):
    t, d = xs.shape
    ntiles = t // tm
    n_exp = rw_t.shape[0]

    def seg(i):
        return jnp.where(i % tpb == 0, n_batch, i // tpb)

    return pl.pallas_call(
        _router_kernel,
        grid=(ntiles,),
        in_specs=[
            pl.BlockSpec((tm, d), lambda i: (i, 0)),
            pl.BlockSpec((1, SUBLANES, d), lambda i: (seg(i), 0, 0)),
            pl.BlockSpec((1, d), lambda i: (0, 0)),
            pl.BlockSpec((n_exp, d), lambda i: (0, 0)),
            pl.BlockSpec((n_exp, LANES), lambda i: (0, 0)),
        ],
        out_specs=[
            pl.BlockSpec((1, SUBLANES, tm), lambda i: (i, 0, 0)),
            pl.BlockSpec((1, SUBLANES, tm), lambda i: (i, 0, 0)),
            pl.BlockSpec((1, n_exp, LANES), lambda i: (i, 0, 0)),
        ],
        out_shape=[
            jax.ShapeDtypeStruct((ntiles, SUBLANES, tm), jnp.int32),
            jax.ShapeDtypeStruct((ntiles, SUBLANES, tm), F32),
            jax.ShapeDtypeStruct((ntiles, n_exp, LANES), jnp.int32),
        ],
        compiler_params=_params(1),
        name="router",
    )(xs, mods, gain, rw_t, rb_col)


def _routing_plan(cnt, *, tm, sorted_rows):
    ntiles, n_exp = cnt.shape
    seg_len = (cnt + ROW_ALIGN - 1) // ROW_ALIGN * ROW_ALIGN
    off = jnp.cumsum(seg_len, axis=1) - seg_len
    tile_rows = jnp.sum(seg_len, axis=1)
    before = jnp.cumsum(seg_len, axis=0) - seg_len
    filled = jnp.sum(seg_len, axis=0)
    region = (filled + EXPERT_TILE - 1) // EXPERT_TILE * EXPERT_TILE
    region_end = jnp.cumsum(region)
    region_start = region_end - region
    base = region_start[None, :] + before

    nb_max = _max_blocks(ntiles, sorted_rows, n_exp)
    dump0 = nb_max * EXPERT_TILE
    nch = sorted_rows // ROW_ALIGN
    q = jnp.arange(nch, dtype=jnp.int32) * ROW_ALIGN
    seg_end = off + seg_len
    inside = jnp.logical_and(off[:, None, :] <= q[None, :, None], q[None, :, None] < seg_end[:, None, :])
    valid = q[None, :] < tile_rows[:, None]
    row = q[None, :] + jnp.sum(jnp.where(inside, (base - off)[:, None, :], 0), axis=2)
    parity = (jnp.arange(ntiles, dtype=jnp.int32) % 2)[:, None]
    chunk_dst = jnp.where(valid, row, dump0 + parity * sorted_rows + q[None, :]).astype(jnp.int32)
    chunk_src = jnp.where(valid, row, 0).astype(jnp.int32)

    fill_q = jnp.arange(EXPERT_TILE // ROW_ALIGN, dtype=jnp.int32) * ROW_ALIGN
    fill_row = filled[:, None] + fill_q[None, :]
    fill_dump = dump0 + 2 * sorted_rows + (jnp.arange(n_exp, dtype=jnp.int32)[:, None] * EXPERT_TILE + fill_q[None, :])
    fill_dst = jnp.where(fill_row < region[:, None], region_start[:, None] + fill_row, fill_dump)
    fill_dst = fill_dst.reshape(-1).astype(jnp.int32)

    blk = jnp.arange(nb_max, dtype=jnp.int32) * EXPERT_TILE
    block_e = jnp.sum((region_end[None, :] <= blk[:, None]).astype(jnp.int32), axis=1)
    block_e = jnp.minimum(block_e, n_exp - 1).astype(jnp.int32)
    n_used = (region_end[-1] // EXPERT_TILE).astype(jnp.int32).reshape(1)
    return off, chunk_dst, chunk_src, fill_dst, block_e, n_used


def _max_blocks(ntiles, sorted_rows, n_exp):
    return -(-(ntiles * sorted_rows + n_exp * EXPERT_TILE) // EXPERT_TILE)


def _sorted_positions(ei, off_col, tt):
    n_exp = off_col.shape[0]
    expert = lax.broadcasted_iota(jnp.int32, (n_exp, tt), 0)
    hits = [expert == ei[k:k + 1, :] for k in range(TOP_K)]
    chosen = sum(h.astype(F32) for h in hits).astype(BF16)
    t_row = lax.broadcasted_iota(jnp.int32, (tt, tt), 0)
    t_col = lax.broadcasted_iota(jnp.int32, (tt, tt), 1)
    earlier = (t_row < t_col).astype(BF16)
    rank = jnp.dot(chosen, earlier, preferred_element_type=F32) + off_col
    return [jnp.sum(jnp.where(h, rank, 0.0), axis=0, keepdims=True) for h in hits]


def _dispatch_kernel(cd_ref, fd_ref, x_ref, mod_ref, gain_ref, ei_ref, off_ref, xb_ref, pos_ref,
                     buf, zeros, sem, fill_sem):
    step = pl.program_id(0)
    tt = x_ref.shape[0]
    sorted_rows = buf.shape[1]
    nch = sorted_rows // ROW_ALIGN

    def fill_copy(j):
        dst = pl.multiple_of(fd_ref[j], ROW_ALIGN)
        return pltpu.make_async_copy(zeros, xb_ref.at[pl.ds(dst, ROW_ALIGN), :], fill_sem)

    @pl.when(step == 0)
    def _():
        zeros[...] = jnp.zeros_like(zeros)

        def start(j, c):
            fill_copy(j).start()
            return c

        def wait(j, c):
            fill_copy(j).wait()
            return c

        lax.fori_loop(0, fd_ref.shape[0], start, 0)
        lax.fori_loop(0, fd_ref.shape[0], wait, 0)

    cur = step % 2
    last_step = pl.num_programs(0) - 1

    def chunk_copy(slot, j):
        dst = pl.multiple_of(cd_ref[0, 0, j], ROW_ALIGN)
        return pltpu.make_async_copy(buf.at[slot, pl.ds(j * ROW_ALIGN, ROW_ALIGN), :],
                                     xb_ref.at[pl.ds(dst, ROW_ALIGN), :], sem.at[slot])

    def drain(slot):
        for j in range(nch):
            chunk_copy(slot, j).wait()

    @pl.when(step >= 2)
    def _():
        drain(cur)

    h2 = _modulated_norm(x_ref[...], gain_ref[...], mod_ref[0, 3:4, :], mod_ref[0, 4:5, :]).astype(BF16)
    pos = _sorted_positions(ei_ref[0], off_ref[0][:, 0:1], tt)
    slot_id = lax.broadcasted_iota(jnp.int32, (SUBLANES, tt), 0)
    pos8 = jnp.zeros((SUBLANES, tt), F32)
    for k in range(TOP_K):
        pos8 = jnp.where(slot_id == k, pos[k], pos8)
    pos_ref[0] = pos8
    r = lax.broadcasted_iota(jnp.int32, (sorted_rows, tt), 0).astype(F32)
    perm = sum((r == pos[k]).astype(F32) for k in range(TOP_K)).astype(BF16)
    buf[cur] = jnp.dot(perm, h2, preferred_element_type=F32)
    for j in range(nch):
        chunk_copy(cur, j).start()

    @pl.when(step == last_step)
    def _():
        drain(cur)

    @pl.when(jnp.logical_and(step == last_step, step >= 1))
    def _():
        drain(1 - cur)


def _dispatch(xs, mods, gain, ei, off_col, chunk_dst, fill_dst, *, tm, tpb, n_batch, sorted_rows, total_rows):
    t, d = xs.shape
    ntiles = t // tm
    n_exp = off_col.shape[1]
    nch = sorted_rows // ROW_ALIGN

    def seg(i):
        return jnp.where(i % tpb == 0, n_batch, i // tpb)

    return pl.pallas_call(
        _dispatch_kernel,
        grid=(ntiles,),
        in_specs=[
            pl.BlockSpec((1, 1, nch), lambda i: (i, 0, 0), memory_space=pltpu.SMEM),
            pl.BlockSpec(memory_space=pltpu.SMEM),
            pl.BlockSpec((tm, d), lambda i: (i, 0)),
            pl.BlockSpec((1, SUBLANES, d), lambda i: (seg(i), 0, 0)),
            pl.BlockSpec((1, d), lambda i: (0, 0)),
            pl.BlockSpec((1, SUBLANES, tm), lambda i: (i, 0, 0)),
            pl.BlockSpec((1, n_exp, LANES), lambda i: (i, 0, 0)),
        ],
        out_specs=[
            pl.BlockSpec(memory_space=pl.ANY),
            pl.BlockSpec((1, SUBLANES, tm), lambda i: (i, 0, 0)),
        ],
        out_shape=[
            jax.ShapeDtypeStruct((total_rows, d), F32),
            jax.ShapeDtypeStruct((ntiles, SUBLANES, tm), F32),
        ],
        scratch_shapes=[
            pltpu.VMEM((2, sorted_rows, d), F32),
            pltpu.VMEM((ROW_ALIGN, d), F32),
            pltpu.SemaphoreType.DMA((2,)),
            pltpu.SemaphoreType.DMA(()),
        ],
        compiler_params=_params(1),
        name="dispatch",
    )(chunk_dst.reshape(ntiles, 1, nch), fill_dst, xs, mods, gain, ei, off_col)


def _experts_kernel(be_ref, nu_ref, xb_ref, wu_ref, bu_ref, wd_ref, bd_ref, yb_ref, wu_s, wd_s, *, d_ff):
    i = pl.program_id(0)
    prev = be_ref[jnp.maximum(i - 1, 0)]
    fresh = jnp.logical_or(i == 0, be_ref[i] != prev)

    @pl.when(jnp.logical_and(fresh, i < nu_ref[0]))
    def _():
        wu_s[...] = wu_ref[0, 0].astype(BF16)
        wd_s[...] = wd_ref[0, 0].astype(BF16)

    @pl.when(i < nu_ref[0])
    def _():
        u = jnp.dot(xb_ref[...].astype(BF16), wu_s[...], preferred_element_type=F32) + bu_ref[0, 0]
        gate = jnp.minimum(u[:, :d_ff], SWIGLU_LIMIT)
        lin = jnp.clip(u[:, d_ff:], -SWIGLU_LIMIT, SWIGLU_LIMIT)
        act = (gate * _sigmoid(SWIGLU_ALPHA * gate) * (lin + 1.0)).astype(BF16)
        yb_ref[...] = jnp.dot(act, wd_s[...], preferred_element_type=F32) + bd_ref[0, 0]


def _experts(xb, block_e, n_used, w_up, b_up, w_down, b_down, *, layer, nb_max):
    d = xb.shape[1]
    depth, n_exp, _, two_ff = w_up.shape
    d_ff = two_ff // 2

    def blk(i, be, nu):
        return jnp.minimum(i, nu[0] - 1)

    def exp(i, be, nu):
        return be[jnp.minimum(i, nu[0] - 1)]

    return pl.pallas_call(
        functools.partial(_experts_kernel, d_ff=d_ff),
        grid_spec=pltpu.PrefetchScalarGridSpec(
            num_scalar_prefetch=2,
            grid=(nb_max,),
            in_specs=[
                pl.BlockSpec((EXPERT_TILE, d), lambda i, be, nu: (blk(i, be, nu), 0)),
                pl.BlockSpec((1, 1, d, two_ff), lambda i, be, nu: (layer, exp(i, be, nu), 0, 0)),
                pl.BlockSpec((1, 1, 1, two_ff), lambda i, be, nu: (layer, exp(i, be, nu), 0, 0)),
                pl.BlockSpec((1, 1, d_ff, d), lambda i, be, nu: (layer, exp(i, be, nu), 0, 0)),
                pl.BlockSpec((1, 1, 1, d), lambda i, be, nu: (layer, exp(i, be, nu), 0, 0)),
            ],
            out_specs=pl.BlockSpec((EXPERT_TILE, d), lambda i, be, nu: (blk(i, be, nu), 0)),
            scratch_shapes=[pltpu.VMEM((d, two_ff), BF16), pltpu.VMEM((d_ff, d), BF16)],
        ),
        out_shape=jax.ShapeDtypeStruct((nb_max * EXPERT_TILE, d), F32),
        compiler_params=_params(1),
        name="experts",
    )(block_e, n_used, xb, w_up, b_up.reshape(depth, n_exp, 1, two_ff), w_down, b_down.reshape(depth, n_exp, 1, d))


def _combine_kernel(cs_ref, csn_ref, x_ref, mod_ref, pos_ref, wt_ref, yb_ref, o_ref, buf, sem):
    step = pl.program_id(0)
    cur = step % 2
    tt = x_ref.shape[0]
    sorted_rows = buf.shape[1]
    nch = sorted_rows // ROW_ALIGN

    def chunk_copy(table, slot, j):
        src = pl.multiple_of(table[0, 0, j], ROW_ALIGN)
        return pltpu.make_async_copy(yb_ref.at[pl.ds(src, ROW_ALIGN), :],
                                     buf.at[slot, pl.ds(j * ROW_ALIGN, ROW_ALIGN), :], sem.at[slot])

    @pl.when(step == 0)
    def _():
        for j in range(nch):
            chunk_copy(cs_ref, cur, j).start()

    @pl.when(step + 1 < pl.num_programs(0))
    def _():
        for j in range(nch):
            chunk_copy(csn_ref, 1 - cur, j).start()

    r = lax.broadcasted_iota(jnp.int32, (sorted_rows, tt), 0).astype(F32)
    pos = pos_ref[0]
    wt = wt_ref[0]
    sel = sum(jnp.where(r == pos[k:k + 1, :], wt[k:k + 1, :], 0.0) for k in range(TOP_K)).astype(BF16)
    for j in range(nch):
        chunk_copy(cs_ref, cur, j).wait()
    y2 = lax.dot_general(sel, buf[cur].astype(BF16), TN_DIMS, preferred_element_type=F32)
    o_ref[...] = x_ref[...] + mod_ref[0, 5:6, :] * y2


def _combine(xs, mods, pos, wt, chunk_src, yb, *, tm, tpb, n_batch, sorted_rows):
    t, d = xs.shape
    ntiles = t // tm
    nch = sorted_rows // ROW_ALIGN

    def seg(i):
        return jnp.where(i % tpb == 0, n_batch, i // tpb)

    return pl.pallas_call(
        _combine_kernel,
        grid=(ntiles,),
        in_specs=[
            pl.BlockSpec((1, 1, nch), lambda i: (i, 0, 0), memory_space=pltpu.SMEM),
            pl.BlockSpec((1, 1, nch), lambda i: (jnp.minimum(i + 1, ntiles - 1), 0, 0), memory_space=pltpu.SMEM),
            pl.BlockSpec((tm, d), lambda i: (i, 0)),
            pl.BlockSpec((1, SUBLANES, d), lambda i: (seg(i), 0, 0)),
            pl.BlockSpec((1, SUBLANES, tm), lambda i: (i, 0, 0)),
            pl.BlockSpec((1, SUBLANES, tm), lambda i: (i, 0, 0)),
            pl.BlockSpec(memory_space=pl.ANY),
        ],
        out_specs=pl.BlockSpec((tm, d), lambda i: (i, 0)),
        out_shape=jax.ShapeDtypeStruct((t, d), F32),
        scratch_shapes=[pltpu.VMEM((2, sorted_rows, d), F32), pltpu.SemaphoreType.DMA((2,))],
        compiler_params=_params(1),
        name="combine",
    )(chunk_src.reshape(ntiles, 1, nch), chunk_src.reshape(ntiles, 1, nch), xs, mods, pos, wt, yb)


def _final_norm_kernel(x_ref, gain_ref, o_ref):
    x = x_ref[...]
    o_ref[0] = x * lax.rsqrt(jnp.mean(x * x, axis=-1, keepdims=True) + EPS) * gain_ref[...]


def _final_norm(xs, gain, *, tm, tpb, n_batch, n):
    d = xs.shape[1]
    lat_tiles = n // tm
    return pl.pallas_call(
        _final_norm_kernel,
        grid=(n_batch, lat_tiles),
        in_specs=[
            pl.BlockSpec((tm, d), lambda b, j: (b * tpb + 1 + j, 0)),
            pl.BlockSpec((1, d), lambda b, j: (0, 0)),
        ],
        out_specs=pl.BlockSpec((1, tm, d), lambda b, j: (b, j, 0)),
        out_shape=jax.ShapeDtypeStruct((n_batch, n, d), F32),
        compiler_params=_params(2),
        name="final_norm",
    )(xs, gain)


def _rope_tables(n, tm):
    t = jnp.arange(n, dtype=jnp.int32)
    m = RET_DK // 4
    freqs = ROPE_BASE ** (-jnp.arange(m, dtype=F32) / m)
    ang_r = (t // GRID_W).astype(F32)[:, None] * freqs[None, :]
    ang_c = (t % GRID_W).astype(F32)[:, None] * freqs[None, :]
    cos = jnp.concatenate([jnp.cos(ang_r)] * 2 + [jnp.cos(ang_c)] * 2, axis=1)
    sin = jnp.concatenate([-jnp.sin(ang_r), jnp.sin(ang_r), -jnp.sin(ang_c), jnp.sin(ang_c)], axis=1)
    cos = jnp.concatenate([cos, jnp.ones((tm, RET_DK), F32)], axis=0)
    sin = jnp.concatenate([sin, jnp.zeros((tm, RET_DK), F32)], axis=0)
    return cos, sin


def kernel(x, c, ctx, c_ctx, w_mod, b_mod, norm_mix, norm_ffn, w_in, ret_decay_f, ret_decay_b, conv_w, w_out,
           router_w, router_b, w_up, b_up, w_down, b_down, norm_final):
    n_batch, n, d = x.shape
    lc = ctx.shape[1]
    depth = w_mod.shape[0]
    tm = lc
    assert tm % 256 == 0 and n % tm == 0 and n % GRID_W == 0 and tm % GRID_W == 0
    assert n % RET_CHUNK == 0 and lc % RET_CHUNK == 0 and n_batch + 1 <= SUBLANES
    tpb = (lc + n) // tm
    qk_width = RET_HEADS * RET_DK
    qkv_width = 2 * qk_width + RET_HEADS * RET_DV
    n_exp = router_w.shape[2]
    sorted_rows = TOP_K * tm + n_exp * ROW_ALIGN
    ntiles = n_batch * tpb
    nb_max = _max_blocks(ntiles, sorted_rows, n_exp)
    total_rows = nb_max * EXPERT_TILE + 2 * sorted_rows + n_exp * EXPERT_TILE

    xs = jnp.concatenate([ctx, x], axis=1).reshape(n_batch * (lc + n), d)
    cond = jnp.zeros((SUBLANES, d), F32).at[:n_batch].set(c).at[n_batch].set(c_ctx)
    mod_all = _modulation(cond, w_mod, b_mod).reshape(depth, SUBLANES, 6, d)
    mod_all = jnp.pad(mod_all, ((0, 0), (0, 0), (0, SUBLANES - 6), (0, 0)))
    cos_t, sin_t = _rope_tables(n, tm)

    for layer in range(depth):
        mods = mod_all[layer]
        dec, dmat, gc = _decay_tables(ret_decay_f[layer], ret_decay_b[layer])
        qkv, rest = _inproj(xs, mods, norm_mix[layer][None, :], w_in[layer].astype(BF16), cos_t, sin_t,
                            tm=tm, tpb=tpb, n_batch=n_batch, qk_width=qk_width, qkv_width=qkv_width)
        y = _retention(qkv, dec, dmat, gc, n_batch=n_batch, lc=lc, n=n, qk_width=qk_width)
        conv_w8 = jnp.pad(conv_w[layer], ((0, SUBLANES - conv_w.shape[1]), (0, 0)))
        xs = _mix(xs, y, rest, mods, conv_w8, w_out[layer].astype(BF16), tm=tm, tpb=tpb, n_batch=n_batch)

        gain2 = norm_ffn[layer][None, :]
        rb_col = jnp.broadcast_to(router_b[layer][:, None], (n_exp, LANES))
        ei, wt, cnt = _router(xs, mods, gain2, router_w[layer].T, rb_col, tm=tm, tpb=tpb, n_batch=n_batch)
        off, chunk_dst, chunk_src, fill_dst, block_e, n_used = _routing_plan(
            cnt[:, :, 0], tm=tm, sorted_rows=sorted_rows)
        off_col = jnp.broadcast_to(off.astype(F32)[:, :, None], (ntiles, n_exp, LANES))
        xb, pos = _dispatch(xs, mods, gain2, ei, off_col, chunk_dst, fill_dst, tm=tm, tpb=tpb, n_batch=n_batch,
                            sorted_rows=sorted_rows, total_rows=total_rows)
        yb = _experts(xb, block_e, n_used, w_up, b_up, w_down, b_down, layer=layer, nb_max=nb_max)
        xs = _combine(xs, mods, pos, wt, chunk_src, yb, tm=tm, tpb=tpb, n_batch=n_batch, sorted_rows=sorted_rows)

    return _final_norm(xs, norm_final[None, :], tm=tm, tpb=tpb, n_batch=n_batch, n=n)
```

```python
import functools

import jax
import jax.numpy as jnp
from jax import lax
from jax.experimental import pallas as pl
from jax.experimental.pallas import tpu as pltpu

GRID_W = 64
RET_HEADS = 4
RET_DK = 128
RET_DV = 256
RET_CHUNK = 256
N_EXPERTS = 32
TOP_K = 4
SWIGLU_LIMIT = 7.0
SWIGLU_ALPHA = 1.702
ROPE_BASE = 10000.0
EPS = 1e-6

SUBLANES = 8
LANES = 128
VMEM_LIMIT = 56 * 1024 * 1024

EXPERT_TILE = 256
ROW_ALIGN = SUBLANES

F32 = jnp.float32
BF16 = jnp.bfloat16
NT_DIMS = (((1,), (1,)), ((), ()))
TN_DIMS = (((0,), (0,)), ((), ()))


def _params(n_axes=1):
    return pltpu.CompilerParams(dimension_semantics=("arbitrary",) * n_axes, vmem_limit_bytes=VMEM_LIMIT)


def _modulated_norm(x, gain, shift, scale):
    ms = jnp.mean(x * x, axis=-1, keepdims=True)
    return x * lax.rsqrt(ms + EPS) * gain * (1.0 + scale) + shift


def _sigmoid(z):
    return 1.0 / (1.0 + jnp.exp(-z))


def _split_bf16(a):
    hi = a.astype(BF16)
    lo = (a - hi.astype(F32)).astype(BF16)
    return hi, lo


def _pack_bf16_pairs(a):
    w = a.shape[1] // 2
    lo = lax.bitcast_convert_type(a[:, :w], jnp.uint32)
    hi = lax.bitcast_convert_type(a[:, w:], jnp.uint32)
    return (lo >> 16) | (hi & jnp.uint32(0xFFFF0000))


def _unpack_bf16_pairs(p):
    lo = lax.bitcast_convert_type(p << 16, F32)
    hi = lax.bitcast_convert_type(p & jnp.uint32(0xFFFF0000), F32)
    return jnp.concatenate([lo, hi], axis=1).astype(BF16)


def _modulation_kernel(c_ref, w_ref, b_ref, o_ref):
    c = c_ref[...]
    s = c * _sigmoid(c)
    s_hi, s_lo = _split_bf16(s)
    w_hi, w_lo = _split_bf16(w_ref[0])
    acc = jnp.dot(s_hi, w_hi, preferred_element_type=F32)
    acc += jnp.dot(s_hi, w_lo, preferred_element_type=F32)
    acc += jnp.dot(s_lo, w_hi, preferred_element_type=F32)
    o_ref[0] = acc + b_ref[0]


def _modulation(cond, w_mod, b_mod):
    depth, d, six_d = w_mod.shape
    nblk = six_d // d
    return pl.pallas_call(
        _modulation_kernel,
        grid=(depth, nblk),
        in_specs=[
            pl.BlockSpec((SUBLANES, d), lambda l, j: (0, 0)),
            pl.BlockSpec((1, d, d), lambda l, j: (l, 0, j)),
            pl.BlockSpec((1, 1, d), lambda l, j: (l, 0, j)),
        ],
        out_specs=pl.BlockSpec((1, SUBLANES, d), lambda l, j: (l, 0, j)),
        out_shape=jax.ShapeDtypeStruct((depth, SUBLANES, six_d), F32),
        compiler_params=_params(2),
        name="modulation",
    )(cond, w_mod, b_mod.reshape(depth, 1, six_d))


def _inproj_kernel(x_ref, mod_ref, gain_ref, w_ref, cos_ref, sin_ref, qkv_ref, rest_ref, *, qk_width, q_scale):
    h = _modulated_norm(x_ref[...], gain_ref[...], mod_ref[0, 0:1, :], mod_ref[0, 1:2, :]).astype(BF16)
    reps = qk_width // RET_DK
    cos = jnp.concatenate([cos_ref[...]] * reps, axis=1)
    sin = jnp.concatenate([sin_ref[...]] * reps, axis=1)
    lane = lax.broadcasted_iota(jnp.int32, cos.shape, 1)
    first_half = (lane % (RET_DK // 2)) < (RET_DK // 4)
    for part, scale in ((0, q_scale), (1, 1.0)):
        c0 = part * qk_width
        p = jnp.dot(h, w_ref[:, c0:c0 + qk_width], preferred_element_type=F32)
        partner = jnp.where(first_half, pltpu.roll(p, qk_width - RET_DK // 4, 1), pltpu.roll(p, RET_DK // 4, 1))
        r = p * cos + partner * sin
        if scale != 1.0:
            r = r * scale
        qkv_ref[:, c0:c0 + qk_width] = r.astype(BF16)
    qkv_width = qkv_ref.shape[1]
    step = 512
    for c0 in range(2 * qk_width, qkv_width, step):
        qkv_ref[:, c0:c0 + step] = jnp.dot(h, w_ref[:, c0:c0 + step], preferred_element_type=F32).astype(BF16)
    for c0 in range(0, rest_ref.shape[1], step):
        rest_ref[:, c0:c0 + step] = jnp.dot(
            h, w_ref[:, qkv_width + c0:qkv_width + c0 + step], preferred_element_type=F32).astype(BF16)


def _inproj(xs, mods, gain, w_bf16, cos_t, sin_t, *, tm, tpb, n_batch, qk_width, qkv_width):
    t, d = xs.shape
    in_width = w_bf16.shape[1]
    rest_width = in_width - qkv_width
    pos_blocks = cos_t.shape[0] // tm - 1

    def seg(i):
        return jnp.where(i % tpb == 0, n_batch, i // tpb)

    def pos(i):
        j = i % tpb
        return jnp.where(j == 0, pos_blocks, j - 1)

    return pl.pallas_call(
        functools.partial(_inproj_kernel, qk_width=qk_width, q_scale=RET_DK ** -0.5),
        grid=(t // tm,),
        in_specs=[
            pl.BlockSpec((tm, d), lambda i: (i, 0)),
            pl.BlockSpec((1, SUBLANES, d), lambda i: (seg(i), 0, 0)),
            pl.BlockSpec((1, d), lambda i: (0, 0)),
            pl.BlockSpec((d, in_width), lambda i: (0, 0)),
            pl.BlockSpec((tm, RET_DK), lambda i: (pos(i), 0)),
            pl.BlockSpec((tm, RET_DK), lambda i: (pos(i), 0)),
        ],
        out_specs=[
            pl.BlockSpec((tm, qkv_width), lambda i: (i, 0)),
            pl.BlockSpec((tm, rest_width), lambda i: (i, 0)),
        ],
        out_shape=[
            jax.ShapeDtypeStruct((t, qkv_width), BF16),
            jax.ShapeDtypeStruct((t, rest_width), BF16),
        ],
        compiler_params=_params(1),
        name="inproj",
    )(xs, mods, gain, w_bf16, cos_t, sin_t)


def _retention_kernel(gc_ref, q_ref, k_ref, v_ref, dec_ref, dmat_ref, y_ref, sf_ref, sb_ref, st_ref, *, lc, n):
    head = pl.program_id(1)
    c = RET_CHUNK
    qd = dec_ref[0, :, 0:2 * RET_DK]
    kdf = dec_ref[0, :, 2 * RET_DK:3 * RET_DK]
    kdb = dec_ref[0, :, 3 * RET_DK:4 * RET_DK]
    gcf = gc_ref[head, 0]
    gcb = gc_ref[head, 1]

    def rows(chunk):
        return pl.ds(pl.multiple_of(chunk * c, c), c)

    def advance(s_ref, chunk, kd, gc):
        kdec = (k_ref[rows(chunk), :].astype(F32) * kd).astype(BF16)
        kv = lax.dot_general(kdec, v_ref[rows(chunk), :], TN_DIMS, preferred_element_type=F32)
        s_ref[...] = gc * s_ref[...] + kv

    def sweep(chunk0, nchunks):
        def states(i, carry):
            cf = chunk0 + i
            cb = chunk0 + nchunks - 1 - i
            st_ref[cf, 0:RET_DK, :] = sf_ref[...].astype(BF16)
            st_ref[cb, RET_DK:2 * RET_DK, :] = sb_ref[...].astype(BF16)
            advance(sf_ref, cf, kdf, gcf)
            advance(sb_ref, cb, kdb, gcb)
            return carry

        def outputs(i, carry):
            r = rows(chunk0 + i)
            q, k, v = q_ref[r, :], k_ref[r, :], v_ref[r, :]
            scores = lax.dot_general(q, k, NT_DIMS, preferred_element_type=F32)
            qdec = (jnp.concatenate([q, q], axis=1).astype(F32) * qd).astype(BF16)
            y = jnp.dot((scores * dmat_ref[0]).astype(BF16), v, preferred_element_type=F32)
            y += jnp.dot(qdec, st_ref[chunk0 + i], preferred_element_type=F32)
            y_ref[r, :] = (y * lax.rsqrt(jnp.mean(y * y, axis=-1, keepdims=True) + EPS)).astype(BF16)
            return carry

        unroll = 2 if nchunks % 2 == 0 else 1
        lax.fori_loop(0, nchunks, states, 0, unroll=unroll)
        lax.fori_loop(0, nchunks, outputs, 0, unroll=unroll)

    sf_ref[...] = jnp.zeros_like(sf_ref)
    sb_ref[...] = jnp.zeros_like(sb_ref)
    sweep(0, lc // c)
    sweep(lc // c, n // c)


def _retention(qkv, dec, dmat, gc, *, n_batch, lc, n, qk_width):
    s = lc + n
    t = qkv.shape[0]
    k_blk0 = qk_width // RET_DK
    v_blk0 = 2 * qk_width // RET_DV
    return pl.pallas_call(
        functools.partial(_retention_kernel, lc=lc, n=n),
        grid_spec=pltpu.PrefetchScalarGridSpec(
            num_scalar_prefetch=1,
            grid=(n_batch, RET_HEADS),
            in_specs=[
                pl.BlockSpec((s, RET_DK), lambda b, h, gc: (b, h)),
                pl.BlockSpec((s, RET_DK), lambda b, h, gc: (b, k_blk0 + h)),
                pl.BlockSpec((s, RET_DV), lambda b, h, gc: (b, v_blk0 + h)),
                pl.BlockSpec((1, RET_CHUNK, 4 * RET_DK), lambda b, h, gc: (h, 0, 0)),
                pl.BlockSpec((1, RET_CHUNK, RET_CHUNK), lambda b, h, gc: (h, 0, 0)),
            ],
            out_specs=pl.BlockSpec((s, RET_DV), lambda b, h, gc: (b, h)),
            scratch_shapes=[
                pltpu.VMEM((RET_DK, RET_DV), F32),
                pltpu.VMEM((RET_DK, RET_DV), F32),
                pltpu.VMEM((s // RET_CHUNK, 2 * RET_DK, RET_DV), BF16),
            ],
        ),
        out_shape=jax.ShapeDtypeStruct((t, RET_HEADS * RET_DV), BF16),
        compiler_params=_params(2),
        name="retention",
    )(gc, qkv, qkv, qkv, dec, dmat)


def _decay_tables(logit_f, logit_b):
    c = RET_CHUNK
    lg_f = jax.nn.log_sigmoid(logit_f.astype(F32))[:, None]
    lg_b = jax.nn.log_sigmoid(logit_b.astype(F32))[:, None]
    i = jnp.arange(c, dtype=F32)[None, :]
    vecs = jnp.stack([
        jnp.exp((i + 1) * lg_f), jnp.exp((c - i) * lg_b),
        jnp.exp((c - 1 - i) * lg_f), jnp.exp(i * lg_b)], axis=2)
    dec = jnp.broadcast_to(vecs[..., None], vecs.shape + (RET_DK,)).reshape(vecs.shape[0], c, 4 * RET_DK)
    diff = i[0][:, None] - i[0][None, :]
    lower = jnp.where(diff >= 0, jnp.exp(jnp.where(diff >= 0, diff, 0.0)[None] * lg_f[:, :, None]), 0.0)
    upper = jnp.where(diff < 0, jnp.exp(jnp.where(diff < 0, -diff, 0.0)[None] * lg_b[:, :, None]), 0.0)
    gc = jnp.concatenate([jnp.exp(c * lg_f), jnp.exp(c * lg_b)], axis=1)
    return dec, lower + upper, gc


def _mix_kernel(x_ref, y_ref, rest_ref, mod_ref, cw_ref, wo_ref, gain2_ref, rw_ref, rb_ref,
                o_ref, ei_ref, wt_ref, cnt_ref, *, tpb, ret_width, conv_width):
    is_ctx = (pl.program_id(0) % tpb) == 0
    tm = x_ref.shape[0]
    g = rest_ref[:, 0:ret_width].astype(F32)
    cb = rest_ref[:, ret_width:ret_width + conv_width].astype(F32)
    cc = rest_ref[:, ret_width + conv_width:ret_width + 2 * conv_width].astype(F32)
    cx = rest_ref[:, ret_width + 2 * conv_width:ret_width + 3 * conv_width].astype(F32)
    ret = (g * _sigmoid(g) * y_ref[...].astype(F32)).astype(BF16)
    u = cc * cx
    row = lax.broadcasted_iota(jnp.int32, u.shape, 0)
    pos = jnp.where(is_ctx, row, row % GRID_W)
    last = jnp.where(is_ctx, tm - 1, GRID_W - 1)
    u_prev = jnp.where(pos == 0, 0.0, pltpu.roll(u, 1, 0))
    u_next = jnp.where(pos == last, 0.0, pltpu.roll(u, tm - 1, 0))
    conv = (cb * (u_prev * cw_ref[0:1, :] + u * cw_ref[1:2, :] + u_next * cw_ref[2:3, :])).astype(BF16)
    mix = jnp.dot(ret, wo_ref[0:ret_width, :], preferred_element_type=F32)
    mix += jnp.dot(conv, wo_ref[ret_width:ret_width + conv_width, :], preferred_element_type=F32)
    x_new = x_ref[...] + mod_ref[0, 2:3, :] * mix
    o_ref[...] = x_new
    h2 = _modulated_norm(x_new, gain2_ref[...], mod_ref[0, 3:4, :], mod_ref[0, 4:5, :])
    _route(h2, rw_ref, rb_ref, ei_ref, wt_ref, cnt_ref)


def _mix(xs, y, rest, mods, conv_w8, wo_bf16, gain2, rw_t, rb_col, *, tm, tpb, n_batch):
    t, d = xs.shape
    ntiles = t // tm
    ret_width = y.shape[1]
    conv_width = conv_w8.shape[1]
    n_exp = rw_t.shape[0]

    def seg(i):
        return jnp.where(i % tpb == 0, n_batch, i // tpb)

    return pl.pallas_call(
        functools.partial(_mix_kernel, tpb=tpb, ret_width=ret_width, conv_width=conv_width),
        grid=(ntiles,),
        in_specs=[
            pl.BlockSpec((tm, d), lambda i: (i, 0)),
            pl.BlockSpec((tm, ret_width), lambda i: (i, 0)),
            pl.BlockSpec((tm, rest.shape[1]), lambda i: (i, 0)),
            pl.BlockSpec((1, SUBLANES, d), lambda i: (seg(i), 0, 0)),
            pl.BlockSpec((SUBLANES, conv_width), lambda i: (0, 0)),
            pl.BlockSpec(wo_bf16.shape, lambda i: (0, 0)),
            pl.BlockSpec((1, d), lambda i: (0, 0)),
            pl.BlockSpec((n_exp, d), lambda i: (0, 0)),
            pl.BlockSpec((n_exp, LANES), lambda i: (0, 0)),
        ],
        out_specs=[
            pl.BlockSpec((tm, d), lambda i: (i, 0)),
            pl.BlockSpec((1, SUBLANES, tm), lambda i: (i, 0, 0)),
            pl.BlockSpec((1, SUBLANES, tm), lambda i: (i, 0, 0)),
            pl.BlockSpec((1, n_exp, LANES), lambda i: (i, 0, 0)),
        ],
        out_shape=[
            jax.ShapeDtypeStruct((t, d), F32),
            jax.ShapeDtypeStruct((ntiles, SUBLANES, tm), jnp.int32),
            jax.ShapeDtypeStruct((ntiles, SUBLANES, tm), F32),
            jax.ShapeDtypeStruct((ntiles, n_exp, LANES), jnp.int32),
        ],
        compiler_params=_params(1),
        name="mix",
    )(xs, y, rest, mods, conv_w8, wo_bf16, gain2, rw_t, rb_col)


def _route(h2, rw_ref, rb_ref, ei_ref, wt_ref, cnt_ref):
    h_hi, h_lo = _split_bf16(h2)
    w_hi, w_lo = _split_bf16(rw_ref[...])
    logits = lax.dot_general(w_hi, h_hi, NT_DIMS, preferred_element_type=F32)
    logits += lax.dot_general(w_hi, h_lo, NT_DIMS, preferred_element_type=F32)
    logits += lax.dot_general(w_lo, h_hi, NT_DIMS, preferred_element_type=F32)
    logits += rb_ref[:, 0:1]
    n_exp, tt = logits.shape
    expert = lax.broadcasted_iota(jnp.int32, (n_exp, tt), 0).astype(F32)
    slot = lax.broadcasted_iota(jnp.int32, (SUBLANES, tt), 0)
    ei = jnp.zeros((SUBLANES, tt), F32)
    ev = jnp.zeros((SUBLANES, tt), F32)
    taken = jnp.zeros((n_exp, tt), F32)
    top = None
    denom = jnp.zeros((1, tt), F32)
    for k in range(TOP_K):
        m = jnp.max(logits, axis=0, keepdims=True)
        idx = jnp.min(jnp.where(logits == m, expert, float(n_exp)), axis=0, keepdims=True)
        hit = expert == idx
        taken += hit.astype(F32)
        logits = jnp.where(hit, -jnp.inf, logits)
        if k == 0:
            top = m
        e = jnp.exp(m - top)
        denom += e
        ei = jnp.where(slot == k, idx, ei)
        ev = jnp.where(slot == k, e, ev)
    ei_ref[0] = ei.astype(jnp.int32)
    wt_ref[0] = jnp.where(slot < TOP_K, ev / denom, 0.0)
    cnt = jnp.sum(taken, axis=1, keepdims=True)
    cnt_ref[0] = jnp.broadcast_to(cnt, (n_exp, LANES)).astype(jnp.int32)


def _routing_plan(cnt, *, tm, sorted_rows):
    ntiles, n_exp = cnt.shape
    seg_len = (cnt + ROW_ALIGN - 1) // ROW_ALIGN * ROW_ALIGN
    off = jnp.cumsum(seg_len, axis=1) - seg_len
    tile_rows = jnp.sum(seg_len, axis=1)
    before = jnp.cumsum(seg_len, axis=0) - seg_len
    filled = jnp.sum(seg_len, axis=0)
    region = (filled + EXPERT_TILE - 1) // EXPERT_TILE * EXPERT_TILE
    region_end = jnp.cumsum(region)
    region_start = region_end - region
    base = region_start[None, :] + before

    nb_max = _max_blocks(ntiles, sorted_rows, n_exp)
    dump0 = nb_max * EXPERT_TILE
    nch = sorted_rows // ROW_ALIGN
    q = jnp.arange(nch, dtype=jnp.int32) * ROW_ALIGN
    seg_end = off + seg_len
    inside = jnp.logical_and(off[:, None, :] <= q[None, :, None], q[None, :, None] < seg_end[:, None, :])
    valid = q[None, :] < tile_rows[:, None]
    row = q[None, :] + jnp.sum(jnp.where(inside, (base - off)[:, None, :], 0), axis=2)
    parity = (jnp.arange(ntiles, dtype=jnp.int32) % 2)[:, None]
    chunk_dst = jnp.where(valid, row, dump0 + parity * sorted_rows + q[None, :]).astype(jnp.int32)
    chunk_src = jnp.where(valid, row, 0).astype(jnp.int32)

    fill_q = jnp.arange(EXPERT_TILE // ROW_ALIGN, dtype=jnp.int32) * ROW_ALIGN
    fill_row = filled[:, None] + fill_q[None, :]
    fill_dump = dump0 + 2 * sorted_rows + (jnp.arange(n_exp, dtype=jnp.int32)[:, None] * EXPERT_TILE + fill_q[None, :])
    fill_dst = jnp.where(fill_row < region[:, None], region_start[:, None] + fill_row, fill_dump)
    fill_dst = fill_dst.reshape(-1).astype(jnp.int32)

    blk = jnp.arange(nb_max, dtype=jnp.int32) * EXPERT_TILE
    block_e = jnp.sum((region_end[None, :] <= blk[:, None]).astype(jnp.int32), axis=1)
    block_e = jnp.minimum(block_e, n_exp - 1).astype(jnp.int32)
    n_used = (region_end[-1] // EXPERT_TILE).astype(jnp.int32).reshape(1)
    return off, chunk_dst, chunk_src, fill_dst, block_e, n_used


def _max_blocks(ntiles, sorted_rows, n_exp):
    return -(-(ntiles * sorted_rows + n_exp * EXPERT_TILE) // EXPERT_TILE)


def _sorted_positions(ei, off_col, tt):
    n_exp = off_col.shape[0]
    expert = lax.broadcasted_iota(jnp.int32, (n_exp, tt), 0)
    hits = [expert == ei[k:k + 1, :] for k in range(TOP_K)]
    chosen = sum(h.astype(F32) for h in hits).astype(BF16)
    t_row = lax.broadcasted_iota(jnp.int32, (tt, tt), 0)
    t_col = lax.broadcasted_iota(jnp.int32, (tt, tt), 1)
    earlier = (t_row < t_col).astype(BF16)
    rank = jnp.dot(chosen, earlier, preferred_element_type=F32) + off_col
    return [jnp.sum(jnp.where(h, rank, 0.0), axis=0, keepdims=True) for h in hits]


def _dispatch_kernel(cd_ref, fd_ref, x_ref, mod_ref, gain_ref, ei_ref, off_ref, xb_ref, pos_ref,
                     buf, zeros, sem, fill_sem):
    step = pl.program_id(0)
    tt = x_ref.shape[0]
    sorted_rows = buf.shape[1]
    nch = sorted_rows // ROW_ALIGN

    def fill_copy(j):
        dst = pl.multiple_of(fd_ref[j], ROW_ALIGN)
        return pltpu.make_async_copy(zeros, xb_ref.at[pl.ds(dst, ROW_ALIGN), :], fill_sem)

    @pl.when(step == 0)
    def _():
        zeros[...] = jnp.zeros_like(zeros)

        def start(j, c):
            fill_copy(j).start()
            return c

        def wait(j, c):
            fill_copy(j).wait()
            return c

        lax.fori_loop(0, fd_ref.shape[0], start, 0)
        lax.fori_loop(0, fd_ref.shape[0], wait, 0)

    cur = step % 2
    last_step = pl.num_programs(0) - 1

    def chunk_copy(slot, j):
        dst = pl.multiple_of(cd_ref[0, 0, j], ROW_ALIGN)
        return pltpu.make_async_copy(buf.at[slot, pl.ds(j * ROW_ALIGN, ROW_ALIGN), :],
                                     xb_ref.at[pl.ds(dst, ROW_ALIGN), :], sem.at[slot])

    def drain(slot):
        for j in range(nch):
            chunk_copy(slot, j).wait()

    @pl.when(step >= 2)
    def _():
        drain(cur)

    h2 = _modulated_norm(x_ref[...], gain_ref[...], mod_ref[0, 3:4, :], mod_ref[0, 4:5, :]).astype(BF16)
    pos = _sorted_positions(ei_ref[0], off_ref[0][:, 0:1], tt)
    slot_id = lax.broadcasted_iota(jnp.int32, (SUBLANES, tt), 0)
    pos8 = jnp.zeros((SUBLANES, tt), F32)
    for k in range(TOP_K):
        pos8 = jnp.where(slot_id == k, pos[k], pos8)
    pos_ref[0] = pos8
    r = lax.broadcasted_iota(jnp.int32, (sorted_rows, tt), 0).astype(F32)
    perm = sum((r == pos[k]).astype(F32) for k in range(TOP_K)).astype(BF16)
    buf[cur] = _pack_bf16_pairs(jnp.dot(perm, h2, preferred_element_type=F32))
    for j in range(nch):
        chunk_copy(cur, j).start()

    @pl.when(step == last_step)
    def _():
        drain(cur)

    @pl.when(jnp.logical_and(step == last_step, step >= 1))
    def _():
        drain(1 - cur)


def _dispatch(xs, mods, gain, ei, off_col, chunk_dst, fill_dst, *, tm, tpb, n_batch, sorted_rows, total_rows):
    t, d = xs.shape
    ntiles = t // tm
    n_exp = off_col.shape[1]
    nch = sorted_rows // ROW_ALIGN

    def seg(i):
        return jnp.where(i % tpb == 0, n_batch, i // tpb)

    return pl.pallas_call(
        _dispatch_kernel,
        grid=(ntiles,),
        in_specs=[
            pl.BlockSpec((1, 1, nch), lambda i: (i, 0, 0), memory_space=pltpu.SMEM),
            pl.BlockSpec(memory_space=pltpu.SMEM),
            pl.BlockSpec((tm, d), lambda i: (i, 0)),
            pl.BlockSpec((1, SUBLANES, d), lambda i: (seg(i), 0, 0)),
            pl.BlockSpec((1, d), lambda i: (0, 0)),
            pl.BlockSpec((1, SUBLANES, tm), lambda i: (i, 0, 0)),
            pl.BlockSpec((1, n_exp, LANES), lambda i: (i, 0, 0)),
        ],
        out_specs=[
            pl.BlockSpec(memory_space=pl.ANY),
            pl.BlockSpec((1, SUBLANES, tm), lambda i: (i, 0, 0)),
        ],
        out_shape=[
            jax.ShapeDtypeStruct((total_rows, d // 2), jnp.uint32),
            jax.ShapeDtypeStruct((ntiles, SUBLANES, tm), F32),
        ],
        scratch_shapes=[
            pltpu.VMEM((2, sorted_rows, d // 2), jnp.uint32),
            pltpu.VMEM((ROW_ALIGN, d // 2), jnp.uint32),
            pltpu.SemaphoreType.DMA((2,)),
            pltpu.SemaphoreType.DMA(()),
        ],
        compiler_params=_params(1),
        name="dispatch",
    )(chunk_dst.reshape(ntiles, 1, nch), fill_dst, xs, mods, gain, ei, off_col)


def _experts_kernel(be_ref, nu_ref, xb_ref, wu_ref, bu_ref, wd_ref, bd_ref, yb_ref, wu_s, wd_s, *, d_ff):
    i = pl.program_id(0)
    prev = be_ref[jnp.maximum(i - 1, 0)]
    fresh = jnp.logical_or(i == 0, be_ref[i] != prev)

    @pl.when(jnp.logical_and(fresh, i < nu_ref[0]))
    def _():
        wu_s[...] = wu_ref[0, 0].astype(BF16)
        wd_s[...] = wd_ref[0, 0].astype(BF16)

    @pl.when(i < nu_ref[0])
    def _():
        u = jnp.dot(_unpack_bf16_pairs(xb_ref[...]), wu_s[...], preferred_element_type=F32) + bu_ref[0, 0]
        gate = jnp.minimum(u[:, :d_ff], SWIGLU_LIMIT)
        lin = jnp.clip(u[:, d_ff:], -SWIGLU_LIMIT, SWIGLU_LIMIT)
        act = (gate * _sigmoid(SWIGLU_ALPHA * gate) * (lin + 1.0)).astype(BF16)
        y = jnp.dot(act, wd_s[...], preferred_element_type=F32) + bd_ref[0, 0]
        yb_ref[...] = _pack_bf16_pairs(y.astype(BF16).astype(F32))


def _experts(xb, block_e, n_used, w_up, b_up, w_down, b_down, *, layer, nb_max):
    depth, n_exp, d, two_ff = w_up.shape
    d_ff = two_ff // 2
    dp = xb.shape[1]

    def blk(i, be, nu):
        return jnp.minimum(i, nu[0] - 1)

    def exp(i, be, nu):
        return be[jnp.minimum(i, nu[0] - 1)]

    return pl.pallas_call(
        functools.partial(_experts_kernel, d_ff=d_ff),
        grid_spec=pltpu.PrefetchScalarGridSpec(
            num_scalar_prefetch=2,
            grid=(nb_max,),
            in_specs=[
                pl.BlockSpec((EXPERT_TILE, dp), lambda i, be, nu: (blk(i, be, nu), 0)),
                pl.BlockSpec((1, 1, d, two_ff), lambda i, be, nu: (layer, exp(i, be, nu), 0, 0)),
                pl.BlockSpec((1, 1, 1, two_ff), lambda i, be, nu: (layer, exp(i, be, nu), 0, 0)),
                pl.BlockSpec((1, 1, d_ff, d), lambda i, be, nu: (layer, exp(i, be, nu), 0, 0)),
                pl.BlockSpec((1, 1, 1, d), lambda i, be, nu: (layer, exp(i, be, nu), 0, 0)),
            ],
            out_specs=pl.BlockSpec((EXPERT_TILE, dp), lambda i, be, nu: (blk(i, be, nu), 0)),
            scratch_shapes=[pltpu.VMEM((d, two_ff), BF16), pltpu.VMEM((d_ff, d), BF16)],
        ),
        out_shape=jax.ShapeDtypeStruct((nb_max * EXPERT_TILE, dp), jnp.uint32),
        compiler_params=_params(1),
        name="experts",
    )(block_e, n_used, xb, w_up, b_up.reshape(depth, n_exp, 1, two_ff), w_down, b_down.reshape(depth, n_exp, 1, d))


def _combine_kernel(cs_ref, csn_ref, x_ref, mod_ref, pos_ref, wt_ref, yb_ref, o_ref, buf, sem):
    step = pl.program_id(0)
    cur = step % 2
    tt = x_ref.shape[0]
    sorted_rows = buf.shape[1]
    nch = sorted_rows // ROW_ALIGN

    def chunk_copy(table, slot, j):
        src = pl.multiple_of(table[0, 0, j], ROW_ALIGN)
        return pltpu.make_async_copy(yb_ref.at[pl.ds(src, ROW_ALIGN), :],
                                     buf.at[slot, pl.ds(j * ROW_ALIGN, ROW_ALIGN), :], sem.at[slot])

    @pl.when(step == 0)
    def _():
        for j in range(nch):
            chunk_copy(cs_ref, cur, j).start()

    @pl.when(step + 1 < pl.num_programs(0))
    def _():
        for j in range(nch):
            chunk_copy(csn_ref, 1 - cur, j).start()

    r = lax.broadcasted_iota(jnp.int32, (sorted_rows, tt), 0).astype(F32)
    pos = pos_ref[0]
    wt = wt_ref[0]
    sel = sum(jnp.where(r == pos[k:k + 1, :], wt[k:k + 1, :], 0.0) for k in range(TOP_K)).astype(BF16)
    for j in range(nch):
        chunk_copy(cs_ref, cur, j).wait()
    y2 = lax.dot_general(sel, _unpack_bf16_pairs(buf[cur]), TN_DIMS, preferred_element_type=F32)
    o_ref[...] = x_ref[...] + mod_ref[0, 5:6, :] * y2


def _combine(xs, mods, pos, wt, chunk_src, yb, *, tm, tpb, n_batch, sorted_rows):
    t, d = xs.shape
    ntiles = t // tm
    nch = sorted_rows // ROW_ALIGN

    def seg(i):
        return jnp.where(i % tpb == 0, n_batch, i // tpb)

    return pl.pallas_call(
        _combine_kernel,
        grid=(ntiles,),
        in_specs=[
            pl.BlockSpec((1, 1, nch), lambda i: (i, 0, 0), memory_space=pltpu.SMEM),
            pl.BlockSpec((1, 1, nch), lambda i: (jnp.minimum(i + 1, ntiles - 1), 0, 0), memory_space=pltpu.SMEM),
            pl.BlockSpec((tm, d), lambda i: (i, 0)),
            pl.BlockSpec((1, SUBLANES, d), lambda i: (seg(i), 0, 0)),
            pl.BlockSpec((1, SUBLANES, tm), lambda i: (i, 0, 0)),
            pl.BlockSpec((1, SUBLANES, tm), lambda i: (i, 0, 0)),
            pl.BlockSpec(memory_space=pl.ANY),
        ],
        out_specs=pl.BlockSpec((tm, d), lambda i: (i, 0)),
        out_shape=jax.ShapeDtypeStruct((t, d), F32),
        scratch_shapes=[pltpu.VMEM((2, sorted_rows, yb.shape[1]), jnp.uint32), pltpu.SemaphoreType.DMA((2,))],
        compiler_params=_params(1),
        name="combine",
    )(chunk_src.reshape(ntiles, 1, nch), chunk_src.reshape(ntiles, 1, nch), xs, mods, pos, wt, yb)


def _final_norm_kernel(x_ref, gain_ref, o_ref):
    x = x_ref[...]
    o_ref[0] = x * lax.rsqrt(jnp.mean(x * x, axis=-1, keepdims=True) + EPS) * gain_ref[...]


def _final_norm(xs, gain, *, tm, tpb, n_batch, n):
    d = xs.shape[1]
    lat_tiles = n // tm
    return pl.pallas_call(
        _final_norm_kernel,
        grid=(n_batch, lat_tiles),
        in_specs=[
            pl.BlockSpec((tm, d), lambda b, j: (b * tpb + 1 + j, 0)),
            pl.BlockSpec((1, d), lambda b, j: (0, 0)),
        ],
        out_specs=pl.BlockSpec((1, tm, d), lambda b, j: (b, j, 0)),
        out_shape=jax.ShapeDtypeStruct((n_batch, n, d), F32),
        compiler_params=_params(2),
        name="final_norm",
    )(xs, gain)


def _rope_tables(n, tm):
    t = jnp.arange(n, dtype=jnp.int32)
    m = RET_DK // 4
    freqs = ROPE_BASE ** (-jnp.arange(m, dtype=F32) / m)
    ang_r = (t // GRID_W).astype(F32)[:, None] * freqs[None, :]
    ang_c = (t % GRID_W).astype(F32)[:, None] * freqs[None, :]
    cos = jnp.concatenate([jnp.cos(ang_r)] * 2 + [jnp.cos(ang_c)] * 2, axis=1)
    sin = jnp.concatenate([-jnp.sin(ang_r), jnp.sin(ang_r), -jnp.sin(ang_c), jnp.sin(ang_c)], axis=1)
    cos = jnp.concatenate([cos, jnp.ones((tm, RET_DK), F32)], axis=0)
    sin = jnp.concatenate([sin, jnp.zeros((tm, RET_DK), F32)], axis=0)
    return cos, sin


def kernel(x, c, ctx, c_ctx, w_mod, b_mod, norm_mix, norm_ffn, w_in, ret_decay_f, ret_decay_b, conv_w, w_out,
           router_w, router_b, w_up, b_up, w_down, b_down, norm_final):
    n_batch, n, d = x.shape
    lc = ctx.shape[1]
    depth = w_mod.shape[0]
    tm = lc
    assert tm % 256 == 0 and n % tm == 0 and n % GRID_W == 0 and tm % GRID_W == 0
    assert n % RET_CHUNK == 0 and lc % RET_CHUNK == 0 and n_batch + 1 <= SUBLANES
    tpb = (lc + n) // tm
    qk_width = RET_HEADS * RET_DK
    qkv_width = 2 * qk_width + RET_HEADS * RET_DV
    n_exp = router_w.shape[2]
    sorted_rows = TOP_K * tm + n_exp * ROW_ALIGN
    ntiles = n_batch * tpb
    nb_max = _max_blocks(ntiles, sorted_rows, n_exp)
    total_rows = nb_max * EXPERT_TILE + 2 * sorted_rows + n_exp * EXPERT_TILE

    xs = jnp.concatenate([ctx, x], axis=1).reshape(n_batch * (lc + n), d)
    cond = jnp.zeros((SUBLANES, d), F32).at[:n_batch].set(c).at[n_batch].set(c_ctx)
    mod_all = _modulation(cond, w_mod, b_mod).reshape(depth, SUBLANES, 6, d)
    mod_all = jnp.pad(mod_all, ((0, 0), (0, 0), (0, SUBLANES - 6), (0, 0)))
    cos_t, sin_t = _rope_tables(n, tm)

    for layer in range(depth):
        mods = mod_all[layer]
        dec, dmat, gc = _decay_tables(ret_decay_f[layer], ret_decay_b[layer])
        qkv, rest = _inproj(xs, mods, norm_mix[layer][None, :], w_in[layer].astype(BF16), cos_t, sin_t,
                            tm=tm, tpb=tpb, n_batch=n_batch, qk_width=qk_width, qkv_width=qkv_width)
        y = _retention(qkv, dec, dmat, gc, n_batch=n_batch, lc=lc, n=n, qk_width=qk_width)
        conv_w8 = jnp.pad(conv_w[layer], ((0, SUBLANES - conv_w.shape[1]), (0, 0)))
        gain2 = norm_ffn[layer][None, :]
        rb_col = jnp.broadcast_to(router_b[layer][:, None], (n_exp, LANES))
        xs, ei, wt, cnt = _mix(xs, y, rest, mods, conv_w8, w_out[layer].astype(BF16), gain2, router_w[layer].T,
                               rb_col, tm=tm, tpb=tpb, n_batch=n_batch)
        off, chunk_dst, chunk_src, fill_dst, block_e, n_used = _routing_plan(
            cnt[:, :, 0], tm=tm, sorted_rows=sorted_rows)
        off_col = jnp.broadcast_to(off.astype(F32)[:, :, None], (ntiles, n_exp, LANES))
        xb, pos = _dispatch(xs, mods, gain2, ei, off_col, chunk_dst, fill_dst, tm=tm, tpb=tpb, n_batch=n_batch,
                            sorted_rows=sorted_rows, total_rows=total_rows)
        yb = _experts(xb, block_e, n_used, w_up, b_up, w_down, b_down, layer=layer, nb_max=nb_max)
        xs = _combine(xs, mods, pos, wt, chunk_src, yb, tm=tm, tpb=tpb, n_batch=n_batch, sorted_rows=sorted_rows)

    return _final_norm(xs, norm_final[None, :], tm=tm, tpb=tpb, n_batch=n_batch, n=n)
```

```python
import functools

import jax
import jax.numpy as jnp
from jax import lax
from jax.experimental import pallas as pl
from jax.experimental.pallas import tpu as pltpu

GRID_W = 64
RET_HEADS = 4
RET_DK = 128
RET_DV = 256
RET_CHUNK = 256
N_EXPERTS = 32
TOP_K = 4
SWIGLU_LIMIT = 7.0
SWIGLU_ALPHA = 1.702
ROPE_BASE = 10000.0
EPS = 1e-6

SUBLANES = 8
LANES = 128
VMEM_LIMIT = 56 * 1024 * 1024

EXPERT_TILE = 512
ROW_ALIGN = SUBLANES
FILL_SIZES = tuple(EXPERT_TILE >> k for k in range(1, EXPERT_TILE.bit_length()) if EXPERT_TILE >> k >= ROW_ALIGN)

F32 = jnp.float32
BF16 = jnp.bfloat16
NT_DIMS = (((1,), (1,)), ((), ()))
TN_DIMS = (((0,), (0,)), ((), ()))


def _params(n_axes=1):
    return pltpu.CompilerParams(dimension_semantics=("arbitrary",) * n_axes, vmem_limit_bytes=VMEM_LIMIT)


def _modulated_norm(x, gain, shift, scale):
    ms = jnp.mean(x * x, axis=-1, keepdims=True)
    return x * lax.rsqrt(ms + EPS) * gain * (1.0 + scale) + shift


def _sigmoid(z):
    return 1.0 / (1.0 + jnp.exp(-z))


def _split_bf16(a):
    hi = a.astype(BF16)
    lo = (a - hi.astype(F32)).astype(BF16)
    return hi, lo


def _pack_bf16_pairs(a):
    w = a.shape[1] // 2
    lo = lax.bitcast_convert_type(a[:, :w], jnp.uint32)
    hi = lax.bitcast_convert_type(a[:, w:], jnp.uint32)
    return (lo >> 16) | (hi & jnp.uint32(0xFFFF0000))


def _unpack_bf16_pairs(p):
    lo = lax.bitcast_convert_type(p << 16, F32)
    hi = lax.bitcast_convert_type(p & jnp.uint32(0xFFFF0000), F32)
    return jnp.concatenate([lo, hi], axis=1).astype(BF16)


def _modulation_kernel(c_ref, w_ref, b_ref, o_ref):
    c = c_ref[...]
    s = c * _sigmoid(c)
    s_hi, s_lo = _split_bf16(s)
    w_hi, w_lo = _split_bf16(w_ref[0])
    acc = jnp.dot(s_hi, w_hi, preferred_element_type=F32)
    acc += jnp.dot(s_hi, w_lo, preferred_element_type=F32)
    acc += jnp.dot(s_lo, w_hi, preferred_element_type=F32)
    o_ref[0] = acc + b_ref[0]


def _modulation(cond, w_mod, b_mod):
    depth, d, six_d = w_mod.shape
    nblk = six_d // d
    return pl.pallas_call(
        _modulation_kernel,
        grid=(depth, nblk),
        in_specs=[
            pl.BlockSpec((SUBLANES, d), lambda l, j: (0, 0)),
            pl.BlockSpec((1, d, d), lambda l, j: (l, 0, j)),
            pl.BlockSpec((1, 1, d), lambda l, j: (l, 0, j)),
        ],
        out_specs=pl.BlockSpec((1, SUBLANES, d), lambda l, j: (l, 0, j)),
        out_shape=jax.ShapeDtypeStruct((depth, SUBLANES, six_d), F32),
        compiler_params=_params(2),
        name="modulation",
    )(cond, w_mod, b_mod.reshape(depth, 1, six_d))


def _inproj_kernel(x_ref, mod_ref, gain_ref, w_ref, cos_ref, sin_ref, qkv_ref, rest_ref, *, qk_width):
    _inproj_tile(x_ref[...], mod_ref, gain_ref, w_ref, cos_ref, sin_ref, qkv_ref, rest_ref, qk_width)


def _inproj_tile(x, mod_ref, gain_ref, w_ref, cos_ref, sin_ref, qkv_ref, rest_ref, qk_width):
    q_scale = RET_DK ** -0.5
    h = _modulated_norm(x, gain_ref[...], mod_ref[0, 0:1, :], mod_ref[0, 1:2, :]).astype(BF16)
    reps = qk_width // RET_DK
    cos = jnp.concatenate([cos_ref[...]] * reps, axis=1)
    sin = jnp.concatenate([sin_ref[...]] * reps, axis=1)
    lane = lax.broadcasted_iota(jnp.int32, cos.shape, 1)
    first_half = (lane % (RET_DK // 2)) < (RET_DK // 4)
    for part, scale in ((0, q_scale), (1, 1.0)):
        c0 = part * qk_width
        p = jnp.dot(h, w_ref[:, c0:c0 + qk_width], preferred_element_type=F32)
        partner = jnp.where(first_half, pltpu.roll(p, qk_width - RET_DK // 4, 1), pltpu.roll(p, RET_DK // 4, 1))
        r = p * cos + partner * sin
        if scale != 1.0:
            r = r * scale
        qkv_ref[:, c0:c0 + qk_width] = r.astype(BF16)
    qkv_width = qkv_ref.shape[1]
    step = 512
    for c0 in range(2 * qk_width, qkv_width, step):
        qkv_ref[:, c0:c0 + step] = jnp.dot(h, w_ref[:, c0:c0 + step], preferred_element_type=F32).astype(BF16)
    for c0 in range(0, rest_ref.shape[1], step):
        rest_ref[:, c0:c0 + step] = jnp.dot(
            h, w_ref[:, qkv_width + c0:qkv_width + c0 + step], preferred_element_type=F32).astype(BF16)


def _inproj(xs, mods, gain, w_bf16, cos_t, sin_t, *, tm, tpb, n_batch, qk_width, qkv_width):
    t, d = xs.shape
    in_width = w_bf16.shape[1]
    rest_width = in_width - qkv_width
    pos_blocks = cos_t.shape[0] // tm - 1

    def seg(i):
        return jnp.where(i % tpb == 0, n_batch, i // tpb)

    def pos(i):
        j = i % tpb
        return jnp.where(j == 0, pos_blocks, j - 1)

    return pl.pallas_call(
        functools.partial(_inproj_kernel, qk_width=qk_width),
        grid=(t // tm,),
        in_specs=[
            pl.BlockSpec((tm, d), lambda i: (i, 0)),
            pl.BlockSpec((1, SUBLANES, d), lambda i: (seg(i), 0, 0)),
            pl.BlockSpec((1, d), lambda i: (0, 0)),
            pl.BlockSpec((d, in_width), lambda i: (0, 0)),
            pl.BlockSpec((tm, RET_DK), lambda i: (pos(i), 0)),
            pl.BlockSpec((tm, RET_DK), lambda i: (pos(i), 0)),
        ],
        out_specs=[
            pl.BlockSpec((tm, qkv_width), lambda i: (i, 0)),
            pl.BlockSpec((tm, rest_width), lambda i: (i, 0)),
        ],
        out_shape=[
            jax.ShapeDtypeStruct((t, qkv_width), BF16),
            jax.ShapeDtypeStruct((t, rest_width), BF16),
        ],
        compiler_params=_params(1),
        name="inproj",
    )(xs, mods, gain, w_bf16, cos_t, sin_t)


def _retention_kernel(gc_ref, q_ref, k_ref, v_ref, dec_ref, dmat_ref, y_ref, sf_ref, sb_ref, st_ref, *, lc, n):
    head = pl.program_id(1)
    c = RET_CHUNK
    qd = dec_ref[0, :, 0:2 * RET_DK]
    kdf = dec_ref[0, :, 2 * RET_DK:3 * RET_DK]
    kdb = dec_ref[0, :, 3 * RET_DK:4 * RET_DK]
    gcf = gc_ref[head, 0]
    gcb = gc_ref[head, 1]

    def rows(chunk):
        return pl.ds(pl.multiple_of(chunk * c, c), c)

    def advance(s_ref, chunk, kd, gc):
        kdec = (k_ref[rows(chunk), :].astype(F32) * kd).astype(BF16)
        kv = lax.dot_general(kdec, v_ref[rows(chunk), :], TN_DIMS, preferred_element_type=F32)
        s_ref[...] = gc * s_ref[...] + kv

    def sweep(chunk0, nchunks):
        def states(i, carry):
            cf = chunk0 + i
            cb = chunk0 + nchunks - 1 - i
            st_ref[cf, 0:RET_DK, :] = sf_ref[...].astype(BF16)
            st_ref[cb, RET_DK:2 * RET_DK, :] = sb_ref[...].astype(BF16)
            advance(sf_ref, cf, kdf, gcf)
            advance(sb_ref, cb, kdb, gcb)
            return carry

        def outputs(i, carry):
            r = rows(chunk0 + i)
            q, k, v = q_ref[r, :], k_ref[r, :], v_ref[r, :]
            scores = lax.dot_general(q, k, NT_DIMS, preferred_element_type=F32)
            qdec = (jnp.concatenate([q, q], axis=1).astype(F32) * qd).astype(BF16)
            y = jnp.dot((scores * dmat_ref[0]).astype(BF16), v, preferred_element_type=F32)
            y += jnp.dot(qdec, st_ref[chunk0 + i], preferred_element_type=F32)
            y_ref[r, :] = (y * lax.rsqrt(jnp.mean(y * y, axis=-1, keepdims=True) + EPS)).astype(BF16)
            return carry

        unroll = 2 if nchunks % 2 == 0 else 1
        lax.fori_loop(0, nchunks, states, 0, unroll=unroll)
        lax.fori_loop(0, nchunks, outputs, 0, unroll=unroll)

    sf_ref[...] = jnp.zeros_like(sf_ref)
    sb_ref[...] = jnp.zeros_like(sb_ref)
    sweep(0, lc // c)
    sweep(lc // c, n // c)


def _retention(qkv, dec, dmat, gc, *, n_batch, lc, n, qk_width):
    s = lc + n
    t = qkv.shape[0]
    k_blk0 = qk_width // RET_DK
    v_blk0 = 2 * qk_width // RET_DV
    return pl.pallas_call(
        functools.partial(_retention_kernel, lc=lc, n=n),
        grid_spec=pltpu.PrefetchScalarGridSpec(
            num_scalar_prefetch=1,
            grid=(n_batch, RET_HEADS),
            in_specs=[
                pl.BlockSpec((s, RET_DK), lambda b, h, gc: (b, h)),
                pl.BlockSpec((s, RET_DK), lambda b, h, gc: (b, k_blk0 + h)),
                pl.BlockSpec((s, RET_DV), lambda b, h, gc: (b, v_blk0 + h)),
                pl.BlockSpec((1, RET_CHUNK, 4 * RET_DK), lambda b, h, gc: (h, 0, 0)),
                pl.BlockSpec((1, RET_CHUNK, RET_CHUNK), lambda b, h, gc: (h, 0, 0)),
            ],
            out_specs=pl.BlockSpec((s, RET_DV), lambda b, h, gc: (b, h)),
            scratch_shapes=[
                pltpu.VMEM((RET_DK, RET_DV), F32),
                pltpu.VMEM((RET_DK, RET_DV), F32),
                pltpu.VMEM((s // RET_CHUNK, 2 * RET_DK, RET_DV), BF16),
            ],
        ),
        out_shape=jax.ShapeDtypeStruct((t, RET_HEADS * RET_DV), BF16),
        compiler_params=_params(2),
        name="retention",
    )(gc, qkv, qkv, qkv, dec, dmat)


def _decay_tables(logit_f, logit_b):
    c = RET_CHUNK
    lg_f = jax.nn.log_sigmoid(logit_f.astype(F32))[:, None]
    lg_b = jax.nn.log_sigmoid(logit_b.astype(F32))[:, None]
    i = jnp.arange(c, dtype=F32)[None, :]
    vecs = jnp.stack([
        jnp.exp((i + 1) * lg_f), jnp.exp((c - i) * lg_b),
        jnp.exp((c - 1 - i) * lg_f), jnp.exp(i * lg_b)], axis=2)
    dec = jnp.broadcast_to(vecs[..., None], vecs.shape + (RET_DK,)).reshape(vecs.shape[0], c, 4 * RET_DK)
    diff = i[0][:, None] - i[0][None, :]
    lower = jnp.where(diff >= 0, jnp.exp(jnp.where(diff >= 0, diff, 0.0)[None] * lg_f[:, :, None]), 0.0)
    upper = jnp.where(diff < 0, jnp.exp(jnp.where(diff < 0, -diff, 0.0)[None] * lg_b[:, :, None]), 0.0)
    gc = jnp.concatenate([jnp.exp(c * lg_f), jnp.exp(c * lg_b)], axis=1)
    return dec, lower + upper, gc


def _mix_kernel(x_ref, y_ref, rest_ref, mod_ref, cw_ref, wo_ref, gain2_ref, rw_ref, rb_ref,
                o_ref, ei_ref, wt_ref, cnt_ref, *, tpb, ret_width, conv_width):
    is_ctx = (pl.program_id(0) % tpb) == 0
    tm = x_ref.shape[0]
    g = rest_ref[:, 0:ret_width].astype(F32)
    cb = rest_ref[:, ret_width:ret_width + conv_width].astype(F32)
    cc = rest_ref[:, ret_width + conv_width:ret_width + 2 * conv_width].astype(F32)
    cx = rest_ref[:, ret_width + 2 * conv_width:ret_width + 3 * conv_width].astype(F32)
    ret = (g * _sigmoid(g) * y_ref[...].astype(F32)).astype(BF16)
    u = cc * cx
    row = lax.broadcasted_iota(jnp.int32, u.shape, 0)
    pos = jnp.where(is_ctx, row, row % GRID_W)
    last = jnp.where(is_ctx, tm - 1, GRID_W - 1)
    u_prev = jnp.where(pos == 0, 0.0, pltpu.roll(u, 1, 0))
    u_next = jnp.where(pos == last, 0.0, pltpu.roll(u, tm - 1, 0))
    conv = (cb * (u_prev * cw_ref[0:1, :] + u * cw_ref[1:2, :] + u_next * cw_ref[2:3, :])).astype(BF16)
    mix = jnp.dot(ret, wo_ref[0:ret_width, :], preferred_element_type=F32)
    mix += jnp.dot(conv, wo_ref[ret_width:ret_width + conv_width, :], preferred_element_type=F32)
    x_new = x_ref[...] + mod_ref[0, 2:3, :] * mix
    o_ref[...] = x_new
    h2 = _modulated_norm(x_new, gain2_ref[...], mod_ref[0, 3:4, :], mod_ref[0, 4:5, :])
    _route(h2, rw_ref, rb_ref, ei_ref, wt_ref, cnt_ref)


def _mix(xs, y, rest, mods, conv_w8, wo_bf16, gain2, rw_t, rb_col, *, tm, tpb, n_batch):
    t, d = xs.shape
    ntiles = t // tm
    ret_width = y.shape[1]
    conv_width = conv_w8.shape[1]
    n_exp = rw_t.shape[0]

    def seg(i):
        return jnp.where(i % tpb == 0, n_batch, i // tpb)

    return pl.pallas_call(
        functools.partial(_mix_kernel, tpb=tpb, ret_width=ret_width, conv_width=conv_width),
        grid=(ntiles,),
        in_specs=[
            pl.BlockSpec((tm, d), lambda i: (i, 0)),
            pl.BlockSpec((tm, ret_width), lambda i: (i, 0)),
            pl.BlockSpec((tm, rest.shape[1]), lambda i: (i, 0)),
            pl.BlockSpec((1, SUBLANES, d), lambda i: (seg(i), 0, 0)),
            pl.BlockSpec((SUBLANES, conv_width), lambda i: (0, 0)),
            pl.BlockSpec(wo_bf16.shape, lambda i: (0, 0)),
            pl.BlockSpec((1, d), lambda i: (0, 0)),
            pl.BlockSpec((n_exp, d), lambda i: (0, 0)),
            pl.BlockSpec((n_exp, LANES), lambda i: (0, 0)),
        ],
        out_specs=[
            pl.BlockSpec((tm, d), lambda i: (i, 0)),
            pl.BlockSpec((1, SUBLANES, tm), lambda i: (i, 0, 0)),
            pl.BlockSpec((1, SUBLANES, tm), lambda i: (i, 0, 0)),
            pl.BlockSpec((1, n_exp, LANES), lambda i: (i, 0, 0)),
        ],
        out_shape=[
            jax.ShapeDtypeStruct((t, d), F32),
            jax.ShapeDtypeStruct((ntiles, SUBLANES, tm), jnp.int32),
            jax.ShapeDtypeStruct((ntiles, SUBLANES, tm), F32),
            jax.ShapeDtypeStruct((ntiles, n_exp, LANES), jnp.int32),
        ],
        compiler_params=_params(1),
        name="mix",
    )(xs, y, rest, mods, conv_w8, wo_bf16, gain2, rw_t, rb_col)


def _route(h2, rw_ref, rb_ref, ei_ref, wt_ref, cnt_ref):
    h_hi, h_lo = _split_bf16(h2)
    w_hi, w_lo = _split_bf16(rw_ref[...])
    logits = lax.dot_general(w_hi, h_hi, NT_DIMS, preferred_element_type=F32)
    logits += lax.dot_general(w_hi, h_lo, NT_DIMS, preferred_element_type=F32)
    logits += lax.dot_general(w_lo, h_hi, NT_DIMS, preferred_element_type=F32)
    logits += rb_ref[:, 0:1]
    n_exp, tt = logits.shape
    expert = lax.broadcasted_iota(jnp.int32, (n_exp, tt), 0).astype(F32)
    slot = lax.broadcasted_iota(jnp.int32, (SUBLANES, tt), 0)
    ei = jnp.zeros((SUBLANES, tt), F32)
    ev = jnp.zeros((SUBLANES, tt), F32)
    taken = jnp.zeros((n_exp, tt), F32)
    top = None
    denom = jnp.zeros((1, tt), F32)
    for k in range(TOP_K):
        m = jnp.max(logits, axis=0, keepdims=True)
        idx = jnp.min(jnp.where(logits == m, expert, float(n_exp)), axis=0, keepdims=True)
        hit = expert == idx
        taken += hit.astype(F32)
        logits = jnp.where(hit, -jnp.inf, logits)
        if k == 0:
            top = m
        e = jnp.exp(m - top)
        denom += e
        ei = jnp.where(slot == k, idx, ei)
        ev = jnp.where(slot == k, e, ev)
    ei_ref[0] = ei.astype(jnp.int32)
    wt_ref[0] = jnp.where(slot < TOP_K, ev / denom, 0.0)
    cnt = jnp.sum(taken, axis=1, keepdims=True)
    cnt_ref[0] = jnp.broadcast_to(cnt, (n_exp, LANES)).astype(jnp.int32)


def _routing_plan(cnt, *, tm, sorted_rows):
    ntiles, n_exp = cnt.shape
    seg_len = (cnt + ROW_ALIGN - 1) // ROW_ALIGN * ROW_ALIGN
    off = jnp.cumsum(seg_len, axis=1) - seg_len
    tile_rows = jnp.sum(seg_len, axis=1)
    before = jnp.cumsum(seg_len, axis=0) - seg_len
    filled = jnp.sum(seg_len, axis=0)
    region = (filled + EXPERT_TILE - 1) // EXPERT_TILE * EXPERT_TILE
    region_end = jnp.cumsum(region)
    region_start = region_end - region
    base = region_start[None, :] + before

    nb_max = _max_blocks(ntiles, sorted_rows, n_exp)
    dump0 = nb_max * EXPERT_TILE
    nch = sorted_rows // ROW_ALIGN
    q = jnp.arange(nch, dtype=jnp.int32) * ROW_ALIGN
    seg_end = off + seg_len
    inside = jnp.logical_and(off[:, None, :] <= q[None, :, None], q[None, :, None] < seg_end[:, None, :])
    valid = q[None, :] < tile_rows[:, None]
    row = q[None, :] + jnp.sum(jnp.where(inside, (base - off)[:, None, :], 0), axis=2)
    parity = (jnp.arange(ntiles, dtype=jnp.int32) % 2)[:, None]
    chunk_dst = jnp.where(valid, row, dump0 + parity * sorted_rows + q[None, :]).astype(jnp.int32)
    chunk_src = jnp.where(valid, row, 0).astype(jnp.int32)

    sizes = jnp.asarray(FILL_SIZES, dtype=jnp.int32)[None, :]
    gap = (region - filled)[:, None]
    used = (gap & sizes) != 0
    fill_row = filled[:, None] + (gap & ~(2 * sizes - 1))
    fill_dump = dump0 + 2 * sorted_rows + jnp.arange(n_exp, dtype=jnp.int32)[:, None] * EXPERT_TILE + (
        EXPERT_TILE - 2 * sizes)
    fill_dst = jnp.where(used, region_start[:, None] + fill_row, fill_dump).reshape(-1).astype(jnp.int32)

    blk = jnp.arange(nb_max, dtype=jnp.int32) * EXPERT_TILE
    block_e = jnp.sum((region_end[None, :] <= blk[:, None]).astype(jnp.int32), axis=1)
    block_e = jnp.minimum(block_e, n_exp - 1).astype(jnp.int32)
    n_used = (region_end[-1] // EXPERT_TILE).astype(jnp.int32).reshape(1)
    return off, chunk_dst, chunk_src, fill_dst, block_e, n_used


def _max_blocks(ntiles, sorted_rows, n_exp):
    return -(-(ntiles * sorted_rows + n_exp * EXPERT_TILE) // EXPERT_TILE)


def _sorted_positions(ei, off_col, tt):
    n_exp = off_col.shape[0]
    expert = lax.broadcasted_iota(jnp.int32, (n_exp, tt), 0)
    hits = [expert == ei[k:k + 1, :] for k in range(TOP_K)]
    chosen = sum(h.astype(F32) for h in hits).astype(BF16)
    t_row = lax.broadcasted_iota(jnp.int32, (tt, tt), 0)
    t_col = lax.broadcasted_iota(jnp.int32, (tt, tt), 1)
    earlier = (t_row < t_col).astype(BF16)
    rank = jnp.dot(chosen, earlier, preferred_element_type=F32) + off_col
    return [jnp.sum(jnp.where(h, rank, 0.0), axis=0, keepdims=True) for h in hits]


def _dispatch_kernel(cd_ref, fd_ref, x_ref, mod_ref, gain_ref, ei_ref, off_ref, xb_ref, pos_ref,
                     buf, zeros, sem, fill_sem):
    step = pl.program_id(0)
    tt = x_ref.shape[0]
    sorted_rows = buf.shape[1]
    nch = sorted_rows // ROW_ALIGN

    def fill_copy(e, s):
        size = FILL_SIZES[s]
        dst = pl.multiple_of(fd_ref[e * len(FILL_SIZES) + s], ROW_ALIGN)
        return pltpu.make_async_copy(zeros.at[pl.ds(0, size), :], xb_ref.at[pl.ds(dst, size), :], fill_sem)

    @pl.when(step == 0)
    def _():
        zeros[...] = jnp.zeros_like(zeros)

        def start(e, c):
            for s in range(len(FILL_SIZES)):
                fill_copy(e, s).start()
            return c

        def wait(e, c):
            for s in range(len(FILL_SIZES)):
                fill_copy(e, s).wait()
            return c

        n_regions = fd_ref.shape[0] // len(FILL_SIZES)
        lax.fori_loop(0, n_regions, start, 0)
        lax.fori_loop(0, n_regions, wait, 0)

    cur = step % 2
    last_step = pl.num_programs(0) - 1

    def chunk_copy(slot, j):
        dst = pl.multiple_of(cd_ref[0, 0, j], ROW_ALIGN)
        return pltpu.make_async_copy(buf.at[slot, pl.ds(j * ROW_ALIGN, ROW_ALIGN), :],
                                     xb_ref.at[pl.ds(dst, ROW_ALIGN), :], sem.at[slot])

    def drain(slot):
        for j in range(nch):
            chunk_copy(slot, j).wait()

    @pl.when(step >= 2)
    def _():
        drain(cur)

    h2 = _modulated_norm(x_ref[...], gain_ref[...], mod_ref[0, 3:4, :], mod_ref[0, 4:5, :]).astype(BF16)
    pos = _sorted_positions(ei_ref[0], off_ref[0][:, 0:1], tt)
    slot_id = lax.broadcasted_iota(jnp.int32, (SUBLANES, tt), 0)
    pos8 = jnp.zeros((SUBLANES, tt), F32)
    for k in range(TOP_K):
        pos8 = jnp.where(slot_id == k, pos[k], pos8)
    pos_ref[0] = pos8
    r = lax.broadcasted_iota(jnp.int32, (sorted_rows, tt), 0).astype(F32)
    perm = sum((r == pos[k]).astype(F32) for k in range(TOP_K)).astype(BF16)
    buf[cur] = _pack_bf16_pairs(jnp.dot(perm, h2, preferred_element_type=F32))
    for j in range(nch):
        chunk_copy(cur, j).start()

    @pl.when(step == last_step)
    def _():
        drain(cur)

    @pl.when(jnp.logical_and(step == last_step, step >= 1))
    def _():
        drain(1 - cur)


def _dispatch(xs, mods, gain, ei, off_col, chunk_dst, fill_dst, *, tm, tpb, n_batch, sorted_rows, total_rows):
    t, d = xs.shape
    ntiles = t // tm
    n_exp = off_col.shape[1]
    nch = sorted_rows // ROW_ALIGN

    def seg(i):
        return jnp.where(i % tpb == 0, n_batch, i // tpb)

    return pl.pallas_call(
        _dispatch_kernel,
        grid=(ntiles,),
        in_specs=[
            pl.BlockSpec((1, 1, nch), lambda i: (i, 0, 0), memory_space=pltpu.SMEM),
            pl.BlockSpec(memory_space=pltpu.SMEM),
            pl.BlockSpec((tm, d), lambda i: (i, 0)),
            pl.BlockSpec((1, SUBLANES, d), lambda i: (seg(i), 0, 0)),
            pl.BlockSpec((1, d), lambda i: (0, 0)),
            pl.BlockSpec((1, SUBLANES, tm), lambda i: (i, 0, 0)),
            pl.BlockSpec((1, n_exp, LANES), lambda i: (i, 0, 0)),
        ],
        out_specs=[
            pl.BlockSpec(memory_space=pl.ANY),
            pl.BlockSpec((1, SUBLANES, tm), lambda i: (i, 0, 0)),
        ],
        out_shape=[
            jax.ShapeDtypeStruct((total_rows, d // 2), jnp.uint32),
            jax.ShapeDtypeStruct((ntiles, SUBLANES, tm), F32),
        ],
        scratch_shapes=[
            pltpu.VMEM((2, sorted_rows, d // 2), jnp.uint32),
            pltpu.VMEM((FILL_SIZES[0], d // 2), jnp.uint32),
            pltpu.SemaphoreType.DMA((2,)),
            pltpu.SemaphoreType.DMA(()),
        ],
        compiler_params=_params(1),
        name="dispatch",
    )(chunk_dst.reshape(ntiles, 1, nch), fill_dst, xs, mods, gain, ei, off_col)


def _experts_kernel(be_ref, nu_ref, xb_ref, wu_ref, bu_ref, wd_ref, bd_ref, yb_ref, wu_s, wd_s, *, d_ff):
    i = pl.program_id(0)
    prev = be_ref[jnp.maximum(i - 1, 0)]
    fresh = jnp.logical_or(i == 0, be_ref[i] != prev)

    @pl.when(jnp.logical_and(fresh, i < nu_ref[0]))
    def _():
        wu_s[...] = wu_ref[0, 0].astype(BF16)
        wd_s[...] = wd_ref[0, 0].astype(BF16)

    @pl.when(i < nu_ref[0])
    def _():
        u = jnp.dot(_unpack_bf16_pairs(xb_ref[...]), wu_s[...], preferred_element_type=F32) + bu_ref[0, 0]
        gate = jnp.minimum(u[:, :d_ff], SWIGLU_LIMIT)
        lin = jnp.clip(u[:, d_ff:], -SWIGLU_LIMIT, SWIGLU_LIMIT)
        act = (gate * _sigmoid(SWIGLU_ALPHA * gate) * (lin + 1.0)).astype(BF16)
        y = jnp.dot(act, wd_s[...], preferred_element_type=F32) + bd_ref[0, 0]
        yb_ref[...] = _pack_bf16_pairs(y.astype(BF16).astype(F32))


def _experts(xb, block_e, n_used, w_up, b_up, w_down, b_down, *, layer, nb_max):
    depth, n_exp, d, two_ff = w_up.shape
    d_ff = two_ff // 2
    dp = xb.shape[1]

    def blk(i, be, nu):
        return jnp.minimum(i, nu[0] - 1)

    def exp(i, be, nu):
        return be[jnp.minimum(i, nu[0] - 1)]

    return pl.pallas_call(
        functools.partial(_experts_kernel, d_ff=d_ff),
        grid_spec=pltpu.PrefetchScalarGridSpec(
            num_scalar_prefetch=2,
            grid=(nb_max,),
            in_specs=[
                pl.BlockSpec((EXPERT_TILE, dp), lambda i, be, nu: (blk(i, be, nu), 0)),
                pl.BlockSpec((1, 1, d, two_ff), lambda i, be, nu: (layer, exp(i, be, nu), 0, 0)),
                pl.BlockSpec((1, 1, 1, two_ff), lambda i, be, nu: (layer, exp(i, be, nu), 0, 0)),
                pl.BlockSpec((1, 1, d_ff, d), lambda i, be, nu: (layer, exp(i, be, nu), 0, 0)),
                pl.BlockSpec((1, 1, 1, d), lambda i, be, nu: (layer, exp(i, be, nu), 0, 0)),
            ],
            out_specs=pl.BlockSpec((EXPERT_TILE, dp), lambda i, be, nu: (blk(i, be, nu), 0)),
            scratch_shapes=[pltpu.VMEM((d, two_ff), BF16), pltpu.VMEM((d_ff, d), BF16)],
        ),
        out_shape=jax.ShapeDtypeStruct((nb_max * EXPERT_TILE, dp), jnp.uint32),
        compiler_params=_params(1),
        name="experts",
    )(block_e, n_used, xb, w_up, b_up.reshape(depth, n_exp, 1, two_ff), w_down, b_down.reshape(depth, n_exp, 1, d))


def _combine_kernel(cs_ref, csn_ref, x_ref, mod_ref, pos_ref, wt_ref, yb_ref, *rest, qk_width, follow):
    buf, sem = rest[-2:]
    x_new = _combine_tile(cs_ref, csn_ref, x_ref, mod_ref, pos_ref, wt_ref, yb_ref, buf, sem)
    if follow == "inproj":
        mod2_ref, gain_ref, w_ref, cos_ref, sin_ref, o_ref, qkv_ref, rest_ref = rest[:-2]
        o_ref[...] = x_new
        _inproj_tile(x_new, mod2_ref, gain_ref, w_ref, cos_ref, sin_ref, qkv_ref, rest_ref, qk_width)
    elif follow == "final":
        gain_ref, o_ref = rest[:-2]
        ms = jnp.mean(x_new * x_new, axis=-1, keepdims=True)
        o_ref[0] = x_new * lax.rsqrt(ms + EPS) * gain_ref[...]
    else:
        (o_ref,) = rest[:-2]
        o_ref[...] = x_new


def _combine_tile(cs_ref, csn_ref, x_ref, mod_ref, pos_ref, wt_ref, yb_ref, buf, sem):
    step = pl.program_id(0)
    cur = step % 2
    tt = x_ref.shape[0]
    sorted_rows = buf.shape[1]
    nch = sorted_rows // ROW_ALIGN

    def chunk_copy(table, slot, j):
        src = pl.multiple_of(table[0, 0, j], ROW_ALIGN)
        return pltpu.make_async_copy(yb_ref.at[pl.ds(src, ROW_ALIGN), :],
                                     buf.at[slot, pl.ds(j * ROW_ALIGN, ROW_ALIGN), :], sem.at[slot])

    @pl.when(step == 0)
    def _():
        for j in range(nch):
            chunk_copy(cs_ref, cur, j).start()

    @pl.when(step + 1 < pl.num_programs(0))
    def _():
        for j in range(nch):
            chunk_copy(csn_ref, 1 - cur, j).start()

    r = lax.broadcasted_iota(jnp.int32, (sorted_rows, tt), 0).astype(F32)
    pos = pos_ref[0]
    wt = wt_ref[0]
    sel = sum(jnp.where(r == pos[k:k + 1, :], wt[k:k + 1, :], 0.0) for k in range(TOP_K)).astype(BF16)
    for j in range(nch):
        chunk_copy(cs_ref, cur, j).wait()
    y2 = lax.dot_general(sel, _unpack_bf16_pairs(buf[cur]), TN_DIMS, preferred_element_type=F32)
    return x_ref[...] + mod_ref[0, 5:6, :] * y2


def _combine(xs, mods, pos, wt, chunk_src, yb, *, tm, tpb, n_batch, sorted_rows, inproj=None, final=None):
    t, d = xs.shape
    ntiles = t // tm
    nch = sorted_rows // ROW_ALIGN

    def seg(i):
        return jnp.where(i % tpb == 0, n_batch, i // tpb)

    in_specs = [
        pl.BlockSpec((1, 1, nch), lambda i: (i, 0, 0), memory_space=pltpu.SMEM),
        pl.BlockSpec((1, 1, nch), lambda i: (jnp.minimum(i + 1, ntiles - 1), 0, 0), memory_space=pltpu.SMEM),
        pl.BlockSpec((tm, d), lambda i: (i, 0)),
        pl.BlockSpec((1, SUBLANES, d), lambda i: (seg(i), 0, 0)),
        pl.BlockSpec((1, SUBLANES, tm), lambda i: (i, 0, 0)),
        pl.BlockSpec((1, SUBLANES, tm), lambda i: (i, 0, 0)),
        pl.BlockSpec(memory_space=pl.ANY),
    ]
    args = [chunk_src.reshape(ntiles, 1, nch), chunk_src.reshape(ntiles, 1, nch), xs, mods, pos, wt, yb]
    x_spec = pl.BlockSpec((tm, d), lambda i: (i, 0))
    x_shape = jax.ShapeDtypeStruct((t, d), F32)
    qk_width = None
    if inproj is not None:
        mods2, gain, w_bf16, cos_t, sin_t, qk_width, qkv_width = inproj
        in_width = w_bf16.shape[1]
        pos_blocks = cos_t.shape[0] // tm - 1

        def rope_blk(i):
            j = i % tpb
            return jnp.where(j == 0, pos_blocks, j - 1)

        in_specs += [
            pl.BlockSpec((1, SUBLANES, d), lambda i: (seg(i), 0, 0)),
            pl.BlockSpec((1, d), lambda i: (0, 0)),
            pl.BlockSpec((d, in_width), lambda i: (0, 0)),
            pl.BlockSpec((tm, RET_DK), lambda i: (rope_blk(i), 0)),
            pl.BlockSpec((tm, RET_DK), lambda i: (rope_blk(i), 0)),
        ]
        args += [mods2, gain, w_bf16, cos_t, sin_t]
        out_specs = [x_spec, pl.BlockSpec((tm, qkv_width), lambda i: (i, 0)),
                     pl.BlockSpec((tm, in_width - qkv_width), lambda i: (i, 0))]
        out_shape = [x_shape, jax.ShapeDtypeStruct((t, qkv_width), BF16),
                     jax.ShapeDtypeStruct((t, in_width - qkv_width), BF16)]
        follow = "inproj"
    elif final is not None:
        gain, n = final
        in_specs += [pl.BlockSpec((1, d), lambda i: (0, 0))]
        args += [gain]
        out_specs = pl.BlockSpec((1, tm, d), lambda i: (i // tpb, jnp.maximum(i % tpb - 1, 0), 0))
        out_shape = jax.ShapeDtypeStruct((n_batch, n, d), F32)
        follow = "final"
    else:
        out_specs, out_shape, follow = x_spec, x_shape, None

    return pl.pallas_call(
        functools.partial(_combine_kernel, qk_width=qk_width, follow=follow),
        grid=(ntiles,),
        in_specs=in_specs,
        out_specs=out_specs,
        out_shape=out_shape,
        scratch_shapes=[pltpu.VMEM((2, sorted_rows, yb.shape[1]), jnp.uint32), pltpu.SemaphoreType.DMA((2,))],
        compiler_params=_params(1),
        name="combine",
    )(*args)


def _rope_tables(n, tm):
    t = jnp.arange(n, dtype=jnp.int32)
    m = RET_DK // 4
    freqs = ROPE_BASE ** (-jnp.arange(m, dtype=F32) / m)
    ang_r = (t // GRID_W).astype(F32)[:, None] * freqs[None, :]
    ang_c = (t % GRID_W).astype(F32)[:, None] * freqs[None, :]
    cos = jnp.concatenate([jnp.cos(ang_r)] * 2 + [jnp.cos(ang_c)] * 2, axis=1)
    sin = jnp.concatenate([-jnp.sin(ang_r), jnp.sin(ang_r), -jnp.sin(ang_c), jnp.sin(ang_c)], axis=1)
    cos = jnp.concatenate([cos, jnp.ones((tm, RET_DK), F32)], axis=0)
    sin = jnp.concatenate([sin, jnp.zeros((tm, RET_DK), F32)], axis=0)
    return cos, sin


def kernel(x, c, ctx, c_ctx, w_mod, b_mod, norm_mix, norm_ffn, w_in, ret_decay_f, ret_decay_b, conv_w, w_out,
           router_w, router_b, w_up, b_up, w_down, b_down, norm_final):
    n_batch, n, d = x.shape
    lc = ctx.shape[1]
    depth = w_mod.shape[0]
    tm = lc
    assert tm % 256 == 0 and n % tm == 0 and n % GRID_W == 0 and tm % GRID_W == 0
    assert n % RET_CHUNK == 0 and lc % RET_CHUNK == 0 and n_batch + 1 <= SUBLANES
    tpb = (lc + n) // tm
    qk_width = RET_HEADS * RET_DK
    qkv_width = 2 * qk_width + RET_HEADS * RET_DV
    n_exp = router_w.shape[2]
    sorted_rows = TOP_K * tm + n_exp * ROW_ALIGN
    ntiles = n_batch * tpb
    nb_max = _max_blocks(ntiles, sorted_rows, n_exp)
    total_rows = nb_max * EXPERT_TILE + 2 * sorted_rows + n_exp * EXPERT_TILE

    xs = jnp.concatenate([ctx, x], axis=1).reshape(n_batch * (lc + n), d)
    cond = jnp.zeros((SUBLANES, d), F32).at[:n_batch].set(c).at[n_batch].set(c_ctx)
    mod_all = _modulation(cond, w_mod, b_mod).reshape(depth, SUBLANES, 6, d)
    mod_all = jnp.pad(mod_all, ((0, 0), (0, 0), (0, SUBLANES - 6), (0, 0)))
    cos_t, sin_t = _rope_tables(n, tm)

    tile_args = dict(tm=tm, tpb=tpb, n_batch=n_batch)
    qkv, rest = _inproj(xs, mod_all[0], norm_mix[0][None, :], w_in[0].astype(BF16), cos_t, sin_t,
                        qk_width=qk_width, qkv_width=qkv_width, **tile_args)
    for layer in range(depth):
        mods = mod_all[layer]
        dec, dmat, gc = _decay_tables(ret_decay_f[layer], ret_decay_b[layer])
        y = _retention(qkv, dec, dmat, gc, n_batch=n_batch, lc=lc, n=n, qk_width=qk_width)
        conv_w8 = jnp.pad(conv_w[layer], ((0, SUBLANES - conv_w.shape[1]), (0, 0)))
        gain2 = norm_ffn[layer][None, :]
        rb_col = jnp.broadcast_to(router_b[layer][:, None], (n_exp, LANES))
        xs, ei, wt, cnt = _mix(xs, y, rest, mods, conv_w8, w_out[layer].astype(BF16), gain2, router_w[layer].T,
                               rb_col, tm=tm, tpb=tpb, n_batch=n_batch)
        off, chunk_dst, chunk_src, fill_dst, block_e, n_used = _routing_plan(
            cnt[:, :, 0], tm=tm, sorted_rows=sorted_rows)
        off_col = jnp.broadcast_to(off.astype(F32)[:, :, None], (ntiles, n_exp, LANES))
        xb, pos = _dispatch(xs, mods, gain2, ei, off_col, chunk_dst, fill_dst, tm=tm, tpb=tpb, n_batch=n_batch,
                            sorted_rows=sorted_rows, total_rows=total_rows)
        yb = _experts(xb, block_e, n_used, w_up, b_up, w_down, b_down, layer=layer, nb_max=nb_max)
        if layer + 1 == depth:
            return _combine(xs, mods, pos, wt, chunk_src, yb, sorted_rows=sorted_rows,
                            final=(norm_final[None, :], n), **tile_args)
        nxt = layer + 1
        xs, qkv, rest = _combine(
            xs, mods, pos, wt, chunk_src, yb, sorted_rows=sorted_rows,
            inproj=(mod_all[nxt], norm_mix[nxt][None, :], w_in[nxt].astype(BF16), cos_t, sin_t, qk_width, qkv_width),
            **tile_args)
```

```python
import functools

import jax
import jax.numpy as jnp
from jax import lax
from jax.experimental import pallas as pl
from jax.experimental.pallas import tpu as pltpu

GRID_W = 64
RET_HEADS = 4
RET_DK = 128
RET_DV = 256
RET_CHUNK = 256
N_EXPERTS = 32
TOP_K = 4
SWIGLU_LIMIT = 7.0
SWIGLU_ALPHA = 1.702
ROPE_BASE = 10000.0
EPS = 1e-6

SUBLANES = 8
LANES = 128
VMEM_LIMIT = 56 * 1024 * 1024

EXPERT_TILE = 512
ROW_ALIGN = SUBLANES
FILL_SIZES = tuple(EXPERT_TILE >> k for k in range(1, EXPERT_TILE.bit_length()) if EXPERT_TILE >> k >= ROW_ALIGN)

F32 = jnp.float32
BF16 = jnp.bfloat16
NT_DIMS = (((1,), (1,)), ((), ()))
TN_DIMS = (((0,), (0,)), ((), ()))


def _params(n_axes=1):
    return pltpu.CompilerParams(dimension_semantics=("arbitrary",) * n_axes, vmem_limit_bytes=VMEM_LIMIT)


def _modulated_norm(x, gain, shift, scale):
    ms = jnp.mean(x * x, axis=-1, keepdims=True)
    return x * lax.rsqrt(ms + EPS) * gain * (1.0 + scale) + shift


def _sigmoid(z):
    return 1.0 / (1.0 + jnp.exp(-z))


def _split_bf16(a):
    hi = a.astype(BF16)
    lo = (a - hi.astype(F32)).astype(BF16)
    return hi, lo


def _pack_bf16_pairs(a):
    w = a.shape[1] // 2
    lo = lax.bitcast_convert_type(a[:, :w], jnp.uint32)
    hi = lax.bitcast_convert_type(a[:, w:], jnp.uint32)
    return (lo >> 16) | (hi & jnp.uint32(0xFFFF0000))


def _unpack_bf16_pairs(p):
    lo = lax.bitcast_convert_type(p << 16, F32)
    hi = lax.bitcast_convert_type(p & jnp.uint32(0xFFFF0000), F32)
    return jnp.concatenate([lo, hi], axis=1).astype(BF16)


def _modulation_kernel(c_ref, w_ref, b_ref, o_ref):
    c = c_ref[...]
    s = c * _sigmoid(c)
    s_hi, s_lo = _split_bf16(s)
    w_hi, w_lo = _split_bf16(w_ref[0])
    acc = jnp.dot(s_hi, w_hi, preferred_element_type=F32)
    acc += jnp.dot(s_hi, w_lo, preferred_element_type=F32)
    acc += jnp.dot(s_lo, w_hi, preferred_element_type=F32)
    o_ref[0] = acc + b_ref[0]


def _modulation(cond, w_mod, b_mod):
    depth, d, six_d = w_mod.shape
    nblk = six_d // d
    return pl.pallas_call(
        _modulation_kernel,
        grid=(depth, nblk),
        in_specs=[
            pl.BlockSpec((SUBLANES, d), lambda l, j: (0, 0)),
            pl.BlockSpec((1, d, d), lambda l, j: (l, 0, j)),
            pl.BlockSpec((1, 1, d), lambda l, j: (l, 0, j)),
        ],
        out_specs=pl.BlockSpec((1, SUBLANES, d), lambda l, j: (l, 0, j)),
        out_shape=jax.ShapeDtypeStruct((depth, SUBLANES, six_d), F32),
        compiler_params=_params(2),
        name="modulation",
    )(cond, w_mod, b_mod.reshape(depth, 1, six_d))


def _inproj_kernel(x_ref, mod_ref, gain_ref, w_ref, cos_ref, sin_ref, qkv_ref, rest_ref, *, qk_width):
    _inproj_tile(x_ref[...], mod_ref, gain_ref, w_ref, cos_ref, sin_ref, qkv_ref, rest_ref, qk_width)


def _inproj_tile(x, mod_ref, gain_ref, w_ref, cos_ref, sin_ref, qkv_ref, rest_ref, qk_width):
    q_scale = RET_DK ** -0.5
    h = _modulated_norm(x, gain_ref[...], mod_ref[0, 0:1, :], mod_ref[0, 1:2, :]).astype(BF16)
    reps = qk_width // RET_DK
    cos = jnp.concatenate([cos_ref[...]] * reps, axis=1)
    sin = jnp.concatenate([sin_ref[...]] * reps, axis=1)
    lane = lax.broadcasted_iota(jnp.int32, cos.shape, 1)
    first_half = (lane % (RET_DK // 2)) < (RET_DK // 4)
    for part, scale in ((0, q_scale), (1, 1.0)):
        c0 = part * qk_width
        p = jnp.dot(h, w_ref[:, c0:c0 + qk_width], preferred_element_type=F32)
        partner = jnp.where(first_half, pltpu.roll(p, qk_width - RET_DK // 4, 1), pltpu.roll(p, RET_DK // 4, 1))
        r = p * cos + partner * sin
        if scale != 1.0:
            r = r * scale
        qkv_ref[:, c0:c0 + qk_width] = r.astype(BF16)
    qkv_width = qkv_ref.shape[1]
    step = 512
    for c0 in range(2 * qk_width, qkv_width, step):
        qkv_ref[:, c0:c0 + step] = jnp.dot(h, w_ref[:, c0:c0 + step], preferred_element_type=F32).astype(BF16)
    for c0 in range(0, rest_ref.shape[1], step):
        rest_ref[:, c0:c0 + step] = jnp.dot(
            h, w_ref[:, qkv_width + c0:qkv_width + c0 + step], preferred_element_type=F32).astype(BF16)


def _inproj(xs, mods, gain, w_bf16, cos_t, sin_t, *, tm, tpb, n_batch, qk_width, qkv_width):
    t, d = xs.shape
    in_width = w_bf16.shape[1]
    rest_width = in_width - qkv_width
    pos_blocks = cos_t.shape[0] // tm - 1

    def seg(i):
        return jnp.where(i % tpb == 0, n_batch, i // tpb)

    def pos(i):
        j = i % tpb
        return jnp.where(j == 0, pos_blocks, j - 1)

    return pl.pallas_call(
        functools.partial(_inproj_kernel, qk_width=qk_width),
        grid=(t // tm,),
        in_specs=[
            pl.BlockSpec((tm, d), lambda i: (i, 0)),
            pl.BlockSpec((1, SUBLANES, d), lambda i: (seg(i), 0, 0)),
            pl.BlockSpec((1, d), lambda i: (0, 0)),
            pl.BlockSpec((d, in_width), lambda i: (0, 0)),
            pl.BlockSpec((tm, RET_DK), lambda i: (pos(i), 0)),
            pl.BlockSpec((tm, RET_DK), lambda i: (pos(i), 0)),
        ],
        out_specs=[
            pl.BlockSpec((tm, qkv_width), lambda i: (i, 0)),
            pl.BlockSpec((tm, rest_width), lambda i: (i, 0)),
        ],
        out_shape=[
            jax.ShapeDtypeStruct((t, qkv_width), BF16),
            jax.ShapeDtypeStruct((t, rest_width), BF16),
        ],
        compiler_params=_params(1),
        name="inproj",
    )(xs, mods, gain, w_bf16, cos_t, sin_t)


def _retention_kernel(gc_ref, q_ref, k_ref, v_ref, dec_ref, dmat_ref, y_ref, sf_ref, sb_ref, st_ref, *, lc, n):
    head = pl.program_id(1)
    c = RET_CHUNK
    qd = dec_ref[0, :, 0:2 * RET_DK]
    kdf = dec_ref[0, :, 2 * RET_DK:3 * RET_DK]
    kdb = dec_ref[0, :, 3 * RET_DK:4 * RET_DK]
    gcf = gc_ref[head, 0]
    gcb = gc_ref[head, 1]

    def rows(chunk):
        return pl.ds(pl.multiple_of(chunk * c, c), c)

    def advance(s_ref, chunk, kd, gc):
        kdec = (k_ref[rows(chunk), :].astype(F32) * kd).astype(BF16)
        kv = lax.dot_general(kdec, v_ref[rows(chunk), :], TN_DIMS, preferred_element_type=F32)
        s_ref[...] = gc * s_ref[...] + kv

    def sweep(chunk0, nchunks):
        def states(i, carry):
            cf = chunk0 + i
            cb = chunk0 + nchunks - 1 - i
            st_ref[cf, 0:RET_DK, :] = sf_ref[...].astype(BF16)
            st_ref[cb, RET_DK:2 * RET_DK, :] = sb_ref[...].astype(BF16)
            advance(sf_ref, cf, kdf, gcf)
            advance(sb_ref, cb, kdb, gcb)
            return carry

        def outputs(i, carry):
            r = rows(chunk0 + i)
            q, k, v = q_ref[r, :], k_ref[r, :], v_ref[r, :]
            scores = lax.dot_general(q, k, NT_DIMS, preferred_element_type=F32)
            qdec = (jnp.concatenate([q, q], axis=1).astype(F32) * qd).astype(BF16)
            y = jnp.dot((scores * dmat_ref[0]).astype(BF16), v, preferred_element_type=F32)
            y += jnp.dot(qdec, st_ref[chunk0 + i], preferred_element_type=F32)
            y_ref[r, :] = (y * lax.rsqrt(jnp.mean(y * y, axis=-1, keepdims=True) + EPS)).astype(BF16)
            return carry

        unroll = 2 if nchunks % 2 == 0 else 1
        lax.fori_loop(0, nchunks, states, 0, unroll=unroll)
        lax.fori_loop(0, nchunks, outputs, 0, unroll=unroll)

    sf_ref[...] = jnp.zeros_like(sf_ref)
    sb_ref[...] = jnp.zeros_like(sb_ref)
    sweep(0, lc // c)
    sweep(lc // c, n // c)


def _retention(qkv, dec, dmat, gc, *, n_batch, lc, n, qk_width):
    s = lc + n
    t = qkv.shape[0]
    k_blk0 = qk_width // RET_DK
    v_blk0 = 2 * qk_width // RET_DV
    return pl.pallas_call(
        functools.partial(_retention_kernel, lc=lc, n=n),
        grid_spec=pltpu.PrefetchScalarGridSpec(
            num_scalar_prefetch=1,
            grid=(n_batch, RET_HEADS),
            in_specs=[
                pl.BlockSpec((s, RET_DK), lambda b, h, gc: (b, h)),
                pl.BlockSpec((s, RET_DK), lambda b, h, gc: (b, k_blk0 + h)),
                pl.BlockSpec((s, RET_DV), lambda b, h, gc: (b, v_blk0 + h)),
                pl.BlockSpec((1, RET_CHUNK, 4 * RET_DK), lambda b, h, gc: (h, 0, 0)),
                pl.BlockSpec((1, RET_CHUNK, RET_CHUNK), lambda b, h, gc: (h, 0, 0)),
            ],
            out_specs=pl.BlockSpec((s, RET_DV), lambda b, h, gc: (b, h)),
            scratch_shapes=[
                pltpu.VMEM((RET_DK, RET_DV), F32),
                pltpu.VMEM((RET_DK, RET_DV), F32),
                pltpu.VMEM((s // RET_CHUNK, 2 * RET_DK, RET_DV), BF16),
            ],
        ),
        out_shape=jax.ShapeDtypeStruct((t, RET_HEADS * RET_DV), BF16),
        compiler_params=_params(2),
        name="retention",
    )(gc, qkv, qkv, qkv, dec, dmat)


def _decay_tables(logit_f, logit_b):
    c = RET_CHUNK
    lg_f = jax.nn.log_sigmoid(logit_f.astype(F32))[:, None]
    lg_b = jax.nn.log_sigmoid(logit_b.astype(F32))[:, None]
    i = jnp.arange(c, dtype=F32)[None, :]
    vecs = jnp.stack([
        jnp.exp((i + 1) * lg_f), jnp.exp((c - i) * lg_b),
        jnp.exp((c - 1 - i) * lg_f), jnp.exp(i * lg_b)], axis=2)
    dec = jnp.broadcast_to(vecs[..., None], vecs.shape + (RET_DK,)).reshape(vecs.shape[0], c, 4 * RET_DK)
    diff = i[0][:, None] - i[0][None, :]
    lower = jnp.where(diff >= 0, jnp.exp(jnp.where(diff >= 0, diff, 0.0)[None] * lg_f[:, :, None]), 0.0)
    upper = jnp.where(diff < 0, jnp.exp(jnp.where(diff < 0, -diff, 0.0)[None] * lg_b[:, :, None]), 0.0)
    gc = jnp.concatenate([jnp.exp(c * lg_f), jnp.exp(c * lg_b)], axis=1)
    return dec, lower + upper, gc


def _mix_kernel(x_ref, y_ref, rest_ref, mod_ref, cw_ref, wo_ref, gain2_ref, rw_ref, rb_ref,
                o_ref, h2_ref, ei_ref, wt_ref, cnt_ref, *, tpb, ret_width, conv_width):
    is_ctx = (pl.program_id(0) % tpb) == 0
    tm = x_ref.shape[0]
    g = rest_ref[:, 0:ret_width].astype(F32)
    cb = rest_ref[:, ret_width:ret_width + conv_width].astype(F32)
    cc = rest_ref[:, ret_width + conv_width:ret_width + 2 * conv_width].astype(F32)
    cx = rest_ref[:, ret_width + 2 * conv_width:ret_width + 3 * conv_width].astype(F32)
    ret = (g * _sigmoid(g) * y_ref[...].astype(F32)).astype(BF16)
    u = cc * cx
    row = lax.broadcasted_iota(jnp.int32, u.shape, 0)
    pos = jnp.where(is_ctx, row, row % GRID_W)
    last = jnp.where(is_ctx, tm - 1, GRID_W - 1)
    u_prev = jnp.where(pos == 0, 0.0, pltpu.roll(u, 1, 0))
    u_next = jnp.where(pos == last, 0.0, pltpu.roll(u, tm - 1, 0))
    conv = (cb * (u_prev * cw_ref[0:1, :] + u * cw_ref[1:2, :] + u_next * cw_ref[2:3, :])).astype(BF16)
    mix = jnp.dot(ret, wo_ref[0:ret_width, :], preferred_element_type=F32)
    mix += jnp.dot(conv, wo_ref[ret_width:ret_width + conv_width, :], preferred_element_type=F32)
    x_new = x_ref[...] + mod_ref[0, 2:3, :] * mix
    o_ref[...] = x_new
    h2 = _modulated_norm(x_new, gain2_ref[...], mod_ref[0, 3:4, :], mod_ref[0, 4:5, :])
    h2_ref[...] = h2.astype(BF16)
    _route(h2, rw_ref, rb_ref, ei_ref, wt_ref, cnt_ref)


def _mix(xs, y, rest, mods, conv_w8, wo_bf16, gain2, rw_t, rb_col, *, tm, tpb, n_batch):
    t, d = xs.shape
    ntiles = t // tm
    ret_width = y.shape[1]
    conv_width = conv_w8.shape[1]
    n_exp = rw_t.shape[0]

    def seg(i):
        return jnp.where(i % tpb == 0, n_batch, i // tpb)

    return pl.pallas_call(
        functools.partial(_mix_kernel, tpb=tpb, ret_width=ret_width, conv_width=conv_width),
        grid=(ntiles,),
        in_specs=[
            pl.BlockSpec((tm, d), lambda i: (i, 0)),
            pl.BlockSpec((tm, ret_width), lambda i: (i, 0)),
            pl.BlockSpec((tm, rest.shape[1]), lambda i: (i, 0)),
            pl.BlockSpec((1, SUBLANES, d), lambda i: (seg(i), 0, 0)),
            pl.BlockSpec((SUBLANES, conv_width), lambda i: (0, 0)),
            pl.BlockSpec(wo_bf16.shape, lambda i: (0, 0)),
            pl.BlockSpec((1, d), lambda i: (0, 0)),
            pl.BlockSpec((n_exp, d), lambda i: (0, 0)),
            pl.BlockSpec((n_exp, LANES), lambda i: (0, 0)),
        ],
        out_specs=[
            pl.BlockSpec((tm, d), lambda i: (i, 0)),
            pl.BlockSpec((tm, d), lambda i: (i, 0)),
            pl.BlockSpec((1, SUBLANES, tm), lambda i: (i, 0, 0)),
            pl.BlockSpec((1, SUBLANES, tm), lambda i: (i, 0, 0)),
            pl.BlockSpec((1, n_exp, LANES), lambda i: (i, 0, 0)),
        ],
        out_shape=[
            jax.ShapeDtypeStruct((t, d), F32),
            jax.ShapeDtypeStruct((t, d), BF16),
            jax.ShapeDtypeStruct((ntiles, SUBLANES, tm), jnp.int32),
            jax.ShapeDtypeStruct((ntiles, SUBLANES, tm), F32),
            jax.ShapeDtypeStruct((ntiles, n_exp, LANES), jnp.int32),
        ],
        compiler_params=_params(1),
        name="mix",
    )(xs, y, rest, mods, conv_w8, wo_bf16, gain2, rw_t, rb_col)


def _route(h2, rw_ref, rb_ref, ei_ref, wt_ref, cnt_ref):
    h_hi, h_lo = _split_bf16(h2)
    w_hi, w_lo = _split_bf16(rw_ref[...])
    logits = lax.dot_general(w_hi, h_hi, NT_DIMS, preferred_element_type=F32)
    logits += lax.dot_general(w_hi, h_lo, NT_DIMS, preferred_element_type=F32)
    logits += lax.dot_general(w_lo, h_hi, NT_DIMS, preferred_element_type=F32)
    logits += rb_ref[:, 0:1]
    n_exp, tt = logits.shape
    expert = lax.broadcasted_iota(jnp.int32, (n_exp, tt), 0).astype(F32)
    slot = lax.broadcasted_iota(jnp.int32, (SUBLANES, tt), 0)
    ei = jnp.zeros((SUBLANES, tt), F32)
    ev = jnp.zeros((SUBLANES, tt), F32)
    taken = jnp.zeros((n_exp, tt), F32)
    top = None
    denom = jnp.zeros((1, tt), F32)
    for k in range(TOP_K):
        m = jnp.max(logits, axis=0, keepdims=True)
        idx = jnp.min(jnp.where(logits == m, expert, float(n_exp)), axis=0, keepdims=True)
        hit = expert == idx
        taken += hit.astype(F32)
        logits = jnp.where(hit, -jnp.inf, logits)
        if k == 0:
            top = m
        e = jnp.exp(m - top)
        denom += e
        ei = jnp.where(slot == k, idx, ei)
        ev = jnp.where(slot == k, e, ev)
    ei_ref[0] = ei.astype(jnp.int32)
    wt_ref[0] = jnp.where(slot < TOP_K, ev / denom, 0.0)
    cnt = jnp.sum(taken, axis=1, keepdims=True)
    cnt_ref[0] = jnp.broadcast_to(cnt, (n_exp, LANES)).astype(jnp.int32)


def _routing_plan(cnt, *, tm, sorted_rows):
    ntiles, n_exp = cnt.shape
    seg_len = (cnt + ROW_ALIGN - 1) // ROW_ALIGN * ROW_ALIGN
    off = jnp.cumsum(seg_len, axis=1) - seg_len
    tile_rows = jnp.sum(seg_len, axis=1)
    before = jnp.cumsum(seg_len, axis=0) - seg_len
    filled = jnp.sum(seg_len, axis=0)
    region = (filled + EXPERT_TILE - 1) // EXPERT_TILE * EXPERT_TILE
    region_end = jnp.cumsum(region)
    region_start = region_end - region
    base = region_start[None, :] + before

    nb_max = _max_blocks(ntiles, sorted_rows, n_exp)
    dump0 = nb_max * EXPERT_TILE
    nch = sorted_rows // ROW_ALIGN
    q = jnp.arange(nch, dtype=jnp.int32) * ROW_ALIGN
    seg_end = off + seg_len
    inside = jnp.logical_and(off[:, None, :] <= q[None, :, None], q[None, :, None] < seg_end[:, None, :])
    valid = q[None, :] < tile_rows[:, None]
    row = q[None, :] + jnp.sum(jnp.where(inside, (base - off)[:, None, :], 0), axis=2)
    parity = (jnp.arange(ntiles, dtype=jnp.int32) % 2)[:, None]
    chunk_dst = jnp.where(valid, row, dump0 + parity * sorted_rows + q[None, :]).astype(jnp.int32)
    chunk_src = jnp.where(valid, row, 0).astype(jnp.int32)

    sizes = jnp.asarray(FILL_SIZES, dtype=jnp.int32)[None, :]
    gap = (region - filled)[:, None]
    used = (gap & sizes) != 0
    fill_row = filled[:, None] + (gap & ~(2 * sizes - 1))
    fill_dump = dump0 + 2 * sorted_rows + jnp.arange(n_exp, dtype=jnp.int32)[:, None] * EXPERT_TILE + (
        EXPERT_TILE - 2 * sizes)
    fill_dst = jnp.where(used, region_start[:, None] + fill_row, fill_dump).reshape(-1).astype(jnp.int32)

    blk = jnp.arange(nb_max, dtype=jnp.int32) * EXPERT_TILE
    block_e = jnp.sum((region_end[None, :] <= blk[:, None]).astype(jnp.int32), axis=1)
    block_e = jnp.minimum(block_e, n_exp - 1).astype(jnp.int32)
    n_used = (region_end[-1] // EXPERT_TILE).astype(jnp.int32).reshape(1)
    return off, chunk_dst, chunk_src, fill_dst, block_e, n_used


def _max_blocks(ntiles, sorted_rows, n_exp):
    return -(-(ntiles * sorted_rows + n_exp * EXPERT_TILE) // EXPERT_TILE)


def _sorted_positions(ei, off_col, tt):
    n_exp = off_col.shape[0]
    expert = lax.broadcasted_iota(jnp.int32, (n_exp, tt), 0)
    hits = [expert == ei[k:k + 1, :] for k in range(TOP_K)]
    chosen = sum(h.astype(F32) for h in hits).astype(BF16)
    t_row = lax.broadcasted_iota(jnp.int32, (tt, tt), 0)
    t_col = lax.broadcasted_iota(jnp.int32, (tt, tt), 1)
    earlier = (t_row < t_col).astype(BF16)
    rank = jnp.dot(chosen, earlier, preferred_element_type=F32) + off_col
    return [jnp.sum(jnp.where(h, rank, 0.0), axis=0, keepdims=True) for h in hits]


def _dispatch_kernel(cd_ref, fd_ref, h2_ref, ei_ref, off_ref, xb_ref, pos_ref, buf, zeros, sem, fill_sem):
    step = pl.program_id(0)
    tt = h2_ref.shape[0]
    sorted_rows = buf.shape[1]
    nch = sorted_rows // ROW_ALIGN

    def fill_copy(e, s):
        size = FILL_SIZES[s]
        dst = pl.multiple_of(fd_ref[e * len(FILL_SIZES) + s], ROW_ALIGN)
        return pltpu.make_async_copy(zeros.at[pl.ds(0, size), :], xb_ref.at[pl.ds(dst, size), :], fill_sem)

    @pl.when(step == 0)
    def _():
        zeros[...] = jnp.zeros_like(zeros)

        def start(e, c):
            for s in range(len(FILL_SIZES)):
                fill_copy(e, s).start()
            return c

        def wait(e, c):
            for s in range(len(FILL_SIZES)):
                fill_copy(e, s).wait()
            return c

        n_regions = fd_ref.shape[0] // len(FILL_SIZES)
        lax.fori_loop(0, n_regions, start, 0)
        lax.fori_loop(0, n_regions, wait, 0)

    cur = step % 2
    last_step = pl.num_programs(0) - 1

    def chunk_copy(slot, j):
        dst = pl.multiple_of(cd_ref[0, 0, j], ROW_ALIGN)
        return pltpu.make_async_copy(buf.at[slot, pl.ds(j * ROW_ALIGN, ROW_ALIGN), :],
                                     xb_ref.at[pl.ds(dst, ROW_ALIGN), :], sem.at[slot])

    def drain(slot):
        for j in range(nch):
            chunk_copy(slot, j).wait()

    @pl.when(step >= 2)
    def _():
        drain(cur)

    pos = _sorted_positions(ei_ref[0], off_ref[0][:, 0:1], tt)
    slot_id = lax.broadcasted_iota(jnp.int32, (SUBLANES, tt), 0)
    pos8 = jnp.zeros((SUBLANES, tt), F32)
    for k in range(TOP_K):
        pos8 = jnp.where(slot_id == k, pos[k], pos8)
    pos_ref[0] = pos8
    r = lax.broadcasted_iota(jnp.int32, (sorted_rows, tt), 0)
    hit = r == pos[0].astype(jnp.int32)
    for k in range(1, TOP_K):
        hit = jnp.logical_or(hit, r == pos[k].astype(jnp.int32))
    perm = jnp.where(hit, 1.0, 0.0).astype(BF16)
    buf[cur] = _pack_bf16_pairs(jnp.dot(perm, h2_ref[...], preferred_element_type=F32))
    for j in range(nch):
        chunk_copy(cur, j).start()

    @pl.when(step == last_step)
    def _():
        drain(cur)

    @pl.when(jnp.logical_and(step == last_step, step >= 1))
    def _():
        drain(1 - cur)


def _dispatch(h2, ei, off_col, chunk_dst, fill_dst, *, tm, sorted_rows, total_rows):
    t, d = h2.shape
    ntiles = t // tm
    n_exp = off_col.shape[1]
    nch = sorted_rows // ROW_ALIGN

    return pl.pallas_call(
        _dispatch_kernel,
        grid=(ntiles,),
        in_specs=[
            pl.BlockSpec((1, 1, nch), lambda i: (i, 0, 0), memory_space=pltpu.SMEM),
            pl.BlockSpec(memory_space=pltpu.SMEM),
            pl.BlockSpec((tm, d), lambda i: (i, 0)),
            pl.BlockSpec((1, SUBLANES, tm), lambda i: (i, 0, 0)),
            pl.BlockSpec((1, n_exp, LANES), lambda i: (i, 0, 0)),
        ],
        out_specs=[
            pl.BlockSpec(memory_space=pl.ANY),
            pl.BlockSpec((1, SUBLANES, tm), lambda i: (i, 0, 0)),
        ],
        out_shape=[
            jax.ShapeDtypeStruct((total_rows, d // 2), jnp.uint32),
            jax.ShapeDtypeStruct((ntiles, SUBLANES, tm), F32),
        ],
        scratch_shapes=[
            pltpu.VMEM((2, sorted_rows, d // 2), jnp.uint32),
            pltpu.VMEM((FILL_SIZES[0], d // 2), jnp.uint32),
            pltpu.SemaphoreType.DMA((2,)),
            pltpu.SemaphoreType.DMA(()),
        ],
        compiler_params=_params(1),
        name="dispatch",
    )(chunk_dst.reshape(ntiles, 1, nch), fill_dst, h2, ei, off_col)


def _experts_kernel(be_ref, nu_ref, xb_ref, wu_ref, bu_ref, wd_ref, bd_ref, yb_ref, wu_s, wd_s, *, d_ff):
    i = pl.program_id(0)
    prev = be_ref[jnp.maximum(i - 1, 0)]
    fresh = jnp.logical_or(i == 0, be_ref[i] != prev)

    @pl.when(jnp.logical_and(fresh, i < nu_ref[0]))
    def _():
        wu_s[...] = wu_ref[0, 0].astype(BF16)
        wd_s[...] = wd_ref[0, 0].astype(BF16)

    @pl.when(i < nu_ref[0])
    def _():
        u = jnp.dot(_unpack_bf16_pairs(xb_ref[...]), wu_s[...], preferred_element_type=F32) + bu_ref[0, 0]
        gate = jnp.minimum(u[:, :d_ff], SWIGLU_LIMIT)
        lin = jnp.clip(u[:, d_ff:], -SWIGLU_LIMIT, SWIGLU_LIMIT)
        act = (gate * _sigmoid(SWIGLU_ALPHA * gate) * (lin + 1.0)).astype(BF16)
        y = jnp.dot(act, wd_s[...], preferred_element_type=F32) + bd_ref[0, 0]
        yb_ref[...] = _pack_bf16_pairs(y.astype(BF16).astype(F32))


def _experts(xb, block_e, n_used, w_up, b_up, w_down, b_down, *, layer, nb_max):
    depth, n_exp, d, two_ff = w_up.shape
    d_ff = two_ff // 2
    dp = xb.shape[1]

    def blk(i, be, nu):
        return jnp.minimum(i, nu[0] - 1)

    def exp(i, be, nu):
        return be[jnp.minimum(i, nu[0] - 1)]

    return pl.pallas_call(
        functools.partial(_experts_kernel, d_ff=d_ff),
        grid_spec=pltpu.PrefetchScalarGridSpec(
            num_scalar_prefetch=2,
            grid=(nb_max,),
            in_specs=[
                pl.BlockSpec((EXPERT_TILE, dp), lambda i, be, nu: (blk(i, be, nu), 0)),
                pl.BlockSpec((1, 1, d, two_ff), lambda i, be, nu: (layer, exp(i, be, nu), 0, 0)),
                pl.BlockSpec((1, 1, 1, two_ff), lambda i, be, nu: (layer, exp(i, be, nu), 0, 0)),
                pl.BlockSpec((1, 1, d_ff, d), lambda i, be, nu: (layer, exp(i, be, nu), 0, 0)),
                pl.BlockSpec((1, 1, 1, d), lambda i, be, nu: (layer, exp(i, be, nu), 0, 0)),
            ],
            out_specs=pl.BlockSpec((EXPERT_TILE, dp), lambda i, be, nu: (blk(i, be, nu), 0)),
            scratch_shapes=[pltpu.VMEM((d, two_ff), BF16), pltpu.VMEM((d_ff, d), BF16)],
        ),
        out_shape=jax.ShapeDtypeStruct((nb_max * EXPERT_TILE, dp), jnp.uint32),
        compiler_params=_params(1),
        name="experts",
    )(block_e, n_used, xb, w_up, b_up.reshape(depth, n_exp, 1, two_ff), w_down, b_down.reshape(depth, n_exp, 1, d))


def _combine_kernel(cs_ref, csn_ref, x_ref, mod_ref, pos_ref, wt_ref, yb_ref, *rest, qk_width, follow):
    if follow == "inproj":
        mod2_ref, gain_ref, w_ref, cos_ref, sin_ref, o_ref, qkv_ref, rest_ref, buf, sem, x_prev = rest

        @pl.when(pl.program_id(0) == 0)
        def _():
            x_prev[...] = jnp.zeros_like(x_prev)

        _combine_fetch(cs_ref, csn_ref, yb_ref, buf, sem)
        _inproj_tile(x_prev[...], mod2_ref, gain_ref, w_ref, cos_ref, sin_ref, qkv_ref, rest_ref, qk_width)
        x_new = _combine_tile(x_ref, mod_ref, pos_ref, wt_ref, buf)
        o_ref[...] = x_new
        x_prev[...] = x_new
    else:
        gain_ref, o_ref, buf, sem = rest
        _combine_fetch(cs_ref, csn_ref, yb_ref, buf, sem)
        x_new = _combine_tile(x_ref, mod_ref, pos_ref, wt_ref, buf)
        ms = jnp.mean(x_new * x_new, axis=-1, keepdims=True)
        o_ref[0] = x_new * lax.rsqrt(ms + EPS) * gain_ref[...]


def _combine_fetch(cs_ref, csn_ref, yb_ref, buf, sem):
    step = pl.program_id(0)
    cur = step % 2
    nch = buf.shape[1] // ROW_ALIGN

    def chunk_copy(table, slot, j):
        src = pl.multiple_of(table[0, 0, j], ROW_ALIGN)
        return pltpu.make_async_copy(yb_ref.at[pl.ds(src, ROW_ALIGN), :],
                                     buf.at[slot, pl.ds(j * ROW_ALIGN, ROW_ALIGN), :], sem.at[slot])

    @pl.when(step == 0)
    def _():
        for j in range(nch):
            chunk_copy(cs_ref, cur, j).start()

    @pl.when(step + 1 < pl.num_programs(0))
    def _():
        for j in range(nch):
            chunk_copy(csn_ref, 1 - cur, j).start()

    for j in range(nch):
        chunk_copy(cs_ref, cur, j).wait()


def _combine_tile(x_ref, mod_ref, pos_ref, wt_ref, buf):
    tt = x_ref.shape[0]
    sorted_rows = buf.shape[1]
    r = lax.broadcasted_iota(jnp.int32, (sorted_rows, tt), 0)
    pos = pos_ref[0].astype(jnp.int32)
    wt = wt_ref[0]
    sel = jnp.zeros((sorted_rows, tt), F32)
    for k in reversed(range(TOP_K)):
        sel = jnp.where(r == pos[k:k + 1, :], wt[k:k + 1, :], sel)
    y2 = lax.dot_general(sel.astype(BF16), _unpack_bf16_pairs(buf[pl.program_id(0) % 2]), TN_DIMS,
                         preferred_element_type=F32)
    return x_ref[...] + mod_ref[0, 5:6, :] * y2


def _combine(xs, mods, pos, wt, chunk_src, yb, *, tm, tpb, n_batch, sorted_rows, inproj=None, final=None):
    t, d = xs.shape
    ntiles = t // tm
    nch = sorted_rows // ROW_ALIGN

    def seg(i):
        return jnp.where(i % tpb == 0, n_batch, i // tpb)

    def cur(s):
        return jnp.minimum(s, ntiles - 1)

    def prev(s):
        return jnp.maximum(s - 1, 0)

    in_specs = [
        pl.BlockSpec((1, 1, nch), lambda s: (cur(s), 0, 0), memory_space=pltpu.SMEM),
        pl.BlockSpec((1, 1, nch), lambda s: (cur(s + 1), 0, 0), memory_space=pltpu.SMEM),
        pl.BlockSpec((tm, d), lambda s: (cur(s), 0)),
        pl.BlockSpec((1, SUBLANES, d), lambda s: (seg(cur(s)), 0, 0)),
        pl.BlockSpec((1, SUBLANES, tm), lambda s: (cur(s), 0, 0)),
        pl.BlockSpec((1, SUBLANES, tm), lambda s: (cur(s), 0, 0)),
        pl.BlockSpec(memory_space=pl.ANY),
    ]
    args = [chunk_src.reshape(ntiles, 1, nch), chunk_src.reshape(ntiles, 1, nch), xs, mods, pos, wt, yb]
    scratch = [pltpu.VMEM((2, sorted_rows, yb.shape[1]), jnp.uint32), pltpu.SemaphoreType.DMA((2,))]
    if inproj is not None:
        mods2, gain, w_bf16, cos_t, sin_t, qk_width, qkv_width = inproj
        in_width = w_bf16.shape[1]
        pos_blocks = cos_t.shape[0] // tm - 1

        def rope_blk(i):
            j = i % tpb
            return jnp.where(j == 0, pos_blocks, j - 1)

        in_specs += [
            pl.BlockSpec((1, SUBLANES, d), lambda s: (seg(prev(s)), 0, 0)),
            pl.BlockSpec((1, d), lambda s: (0, 0)),
            pl.BlockSpec((d, in_width), lambda s: (0, 0)),
            pl.BlockSpec((tm, RET_DK), lambda s: (rope_blk(prev(s)), 0)),
            pl.BlockSpec((tm, RET_DK), lambda s: (rope_blk(prev(s)), 0)),
        ]
        args += [mods2, gain, w_bf16, cos_t, sin_t]
        out_specs = [pl.BlockSpec((tm, d), lambda s: (cur(s), 0)),
                     pl.BlockSpec((tm, qkv_width), lambda s: (prev(s), 0)),
                     pl.BlockSpec((tm, in_width - qkv_width), lambda s: (prev(s), 0))]
        out_shape = [jax.ShapeDtypeStruct((t, d), F32), jax.ShapeDtypeStruct((t, qkv_width), BF16),
                     jax.ShapeDtypeStruct((t, in_width - qkv_width), BF16)]
        scratch += [pltpu.VMEM((tm, d), F32)]
        steps, follow = ntiles + 1, "inproj"
    else:
        gain, n = final
        qk_width = None
        in_specs += [pl.BlockSpec((1, d), lambda s: (0, 0))]
        args += [gain]
        out_specs = pl.BlockSpec((1, tm, d), lambda s: (s // tpb, jnp.maximum(s % tpb - 1, 0), 0))
        out_shape = jax.ShapeDtypeStruct((n_batch, n, d), F32)
        steps, follow = ntiles, "final"

    return pl.pallas_call(
        functools.partial(_combine_kernel, qk_width=qk_width, follow=follow),
        grid=(steps,),
        in_specs=in_specs,
        out_specs=out_specs,
        out_shape=out_shape,
        scratch_shapes=scratch,
        compiler_params=_params(1),
        name="combine",
    )(*args)


def _rope_tables(n, tm):
    t = jnp.arange(n, dtype=jnp.int32)
    m = RET_DK // 4
    freqs = ROPE_BASE ** (-jnp.arange(m, dtype=F32) / m)
    ang_r = (t // GRID_W).astype(F32)[:, None] * freqs[None, :]
    ang_c = (t % GRID_W).astype(F32)[:, None] * freqs[None, :]
    cos = jnp.concatenate([jnp.cos(ang_r)] * 2 + [jnp.cos(ang_c)] * 2, axis=1)
    sin = jnp.concatenate([-jnp.sin(ang_r), jnp.sin(ang_r), -jnp.sin(ang_c), jnp.sin(ang_c)], axis=1)
    cos = jnp.concatenate([cos, jnp.ones((tm, RET_DK), F32)], axis=0)
    sin = jnp.concatenate([sin, jnp.zeros((tm, RET_DK), F32)], axis=0)
    return cos, sin


def kernel(x, c, ctx, c_ctx, w_mod, b_mod, norm_mix, norm_ffn, w_in, ret_decay_f, ret_decay_b, conv_w, w_out,
           router_w, router_b, w_up, b_up, w_down, b_down, norm_final):
    n_batch, n, d = x.shape
    lc = ctx.shape[1]
    depth = w_mod.shape[0]
    tm = lc
    assert tm % 256 == 0 and n % tm == 0 and n % GRID_W == 0 and tm % GRID_W == 0
    assert n % RET_CHUNK == 0 and lc % RET_CHUNK == 0 and n_batch + 1 <= SUBLANES
    tpb = (lc + n) // tm
    qk_width = RET_HEADS * RET_DK
    qkv_width = 2 * qk_width + RET_HEADS * RET_DV
    n_exp = router_w.shape[2]
    sorted_rows = TOP_K * tm + n_exp * ROW_ALIGN
    ntiles = n_batch * tpb
    nb_max = _max_blocks(ntiles, sorted_rows, n_exp)
    total_rows = nb_max * EXPERT_TILE + 2 * sorted_rows + n_exp * EXPERT_TILE

    xs = jnp.concatenate([ctx, x], axis=1).reshape(n_batch * (lc + n), d)
    cond = jnp.zeros((SUBLANES, d), F32).at[:n_batch].set(c).at[n_batch].set(c_ctx)
    mod_all = _modulation(cond, w_mod, b_mod).reshape(depth, SUBLANES, 6, d)
    mod_all = jnp.pad(mod_all, ((0, 0), (0, 0), (0, SUBLANES - 6), (0, 0)))
    cos_t, sin_t = _rope_tables(n, tm)

    tile_args = dict(tm=tm, tpb=tpb, n_batch=n_batch)
    qkv, rest = _inproj(xs, mod_all[0], norm_mix[0][None, :], w_in[0].astype(BF16), cos_t, sin_t,
                        qk_width=qk_width, qkv_width=qkv_width, **tile_args)
    for layer in range(depth):
        mods = mod_all[layer]
        dec, dmat, gc = _decay_tables(ret_decay_f[layer], ret_decay_b[layer])
        y = _retention(qkv, dec, dmat, gc, n_batch=n_batch, lc=lc, n=n, qk_width=qk_width)
        conv_w8 = jnp.pad(conv_w[layer], ((0, SUBLANES - conv_w.shape[1]), (0, 0)))
        gain2 = norm_ffn[layer][None, :]
        rb_col = jnp.broadcast_to(router_b[layer][:, None], (n_exp, LANES))
        xs, h2, ei, wt, cnt = _mix(xs, y, rest, mods, conv_w8, w_out[layer].astype(BF16), gain2, router_w[layer].T,
                                   rb_col, **tile_args)
        off, chunk_dst, chunk_src, fill_dst, block_e, n_used = _routing_plan(
            cnt[:, :, 0], tm=tm, sorted_rows=sorted_rows)
        off_col = jnp.broadcast_to(off.astype(F32)[:, :, None], (ntiles, n_exp, LANES))
        xb, pos = _dispatch(h2, ei, off_col, chunk_dst, fill_dst, tm=tm, sorted_rows=sorted_rows,
                            total_rows=total_rows)
        yb = _experts(xb, block_e, n_used, w_up, b_up, w_down, b_down, layer=layer, nb_max=nb_max)
        if layer + 1 == depth:
            return _combine(xs, mods, pos, wt, chunk_src, yb, sorted_rows=sorted_rows,
                            final=(norm_final[None, :], n), **tile_args)
        nxt = layer + 1
        xs, qkv, rest = _combine(
            xs, mods, pos, wt, chunk_src, yb, sorted_rows=sorted_rows,
            inproj=(mod_all[nxt], norm_mix[nxt][None, :], w_in[nxt].astype(BF16), cos_t, sin_t, qk_width, qkv_width),
            **tile_args)
```

```python
import functools

import jax
import jax.numpy as jnp
from jax import lax
from jax.experimental import pallas as pl
from jax.experimental.pallas import tpu as pltpu

GRID_W = 64
RET_HEADS = 4
RET_DK = 128
RET_DV = 256
RET_CHUNK = 256
N_EXPERTS = 32
TOP_K = 4
SWIGLU_LIMIT = 7.0
SWIGLU_ALPHA = 1.702
ROPE_BASE = 10000.0
EPS = 1e-6

SUBLANES = 8
LANES = 128
VMEM_LIMIT = 56 * 1024 * 1024

EXPERT_TILE = 512
ROW_ALIGN = SUBLANES
FILL_SIZES = tuple(EXPERT_TILE >> k for k in range(1, EXPERT_TILE.bit_length()) if EXPERT_TILE >> k >= ROW_ALIGN)

F32 = jnp.float32
BF16 = jnp.bfloat16
NT_DIMS = (((1,), (1,)), ((), ()))
TN_DIMS = (((0,), (0,)), ((), ()))


def _params(n_axes=1):
    return pltpu.CompilerParams(dimension_semantics=("arbitrary",) * n_axes, vmem_limit_bytes=VMEM_LIMIT)


def _modulated_norm(x, gain, shift, scale):
    ms = jnp.mean(x * x, axis=-1, keepdims=True)
    return x * lax.rsqrt(ms + EPS) * gain * (1.0 + scale) + shift


def _sigmoid(z):
    return 1.0 / (1.0 + jnp.exp(-z))


def _split_bf16(a):
    hi = a.astype(BF16)
    lo = (a - hi.astype(F32)).astype(BF16)
    return hi, lo


def _pack_bf16_pairs(a):
    w = a.shape[1] // 2
    lo = lax.bitcast_convert_type(a[:, :w], jnp.uint32)
    hi = lax.bitcast_convert_type(a[:, w:], jnp.uint32)
    return (lo >> 16) | (hi & jnp.uint32(0xFFFF0000))


def _unpack_bf16_pairs(p):
    lo = lax.bitcast_convert_type(p << 16, F32)
    hi = lax.bitcast_convert_type(p & jnp.uint32(0xFFFF0000), F32)
    return jnp.concatenate([lo, hi], axis=1).astype(BF16)


def _modulation_kernel(c_ref, w_ref, b_ref, o_ref):
    c = c_ref[...]
    s = c * _sigmoid(c)
    s_hi, s_lo = _split_bf16(s)
    w_hi, w_lo = _split_bf16(w_ref[0])
    acc = jnp.dot(s_hi, w_hi, preferred_element_type=F32)
    acc += jnp.dot(s_hi, w_lo, preferred_element_type=F32)
    acc += jnp.dot(s_lo, w_hi, preferred_element_type=F32)
    o_ref[0] = acc + b_ref[0]


def _modulation(cond, w_mod, b_mod):
    depth, d, six_d = w_mod.shape
    nblk = six_d // d
    return pl.pallas_call(
        _modulation_kernel,
        grid=(depth, nblk),
        in_specs=[
            pl.BlockSpec((SUBLANES, d), lambda l, j: (0, 0)),
            pl.BlockSpec((1, d, d), lambda l, j: (l, 0, j)),
            pl.BlockSpec((1, 1, d), lambda l, j: (l, 0, j)),
        ],
        out_specs=pl.BlockSpec((1, SUBLANES, d), lambda l, j: (l, 0, j)),
        out_shape=jax.ShapeDtypeStruct((depth, SUBLANES, six_d), F32),
        compiler_params=_params(2),
        name="modulation",
    )(cond, w_mod, b_mod.reshape(depth, 1, six_d))


def _inproj_kernel(x_ref, mod_ref, gain_ref, w_ref, cos_ref, sin_ref, qkv_ref, rest_ref, *, qk_width):
    _inproj_tile(x_ref[...], mod_ref, gain_ref, w_ref, cos_ref, sin_ref, qkv_ref, rest_ref, qk_width)


def _inproj_tile(x, mod_ref, gain_ref, w_ref, cos_ref, sin_ref, qkv_ref, rest_ref, qk_width):
    q_scale = RET_DK ** -0.5
    h = _modulated_norm(x, gain_ref[...], mod_ref[0, 0:1, :], mod_ref[0, 1:2, :]).astype(BF16)
    reps = qk_width // RET_DK
    cos = jnp.concatenate([cos_ref[...]] * reps, axis=1)
    sin = jnp.concatenate([sin_ref[...]] * reps, axis=1)
    lane = lax.broadcasted_iota(jnp.int32, cos.shape, 1)
    first_half = (lane % (RET_DK // 2)) < (RET_DK // 4)
    for part, scale in ((0, q_scale), (1, 1.0)):
        c0 = part * qk_width
        p = jnp.dot(h, w_ref[:, c0:c0 + qk_width], preferred_element_type=F32)
        partner = jnp.where(first_half, pltpu.roll(p, qk_width - RET_DK // 4, 1), pltpu.roll(p, RET_DK // 4, 1))
        r = p * cos + partner * sin
        if scale != 1.0:
            r = r * scale
        qkv_ref[:, c0:c0 + qk_width] = r.astype(BF16)
    qkv_width = qkv_ref.shape[1]
    step = 512
    for c0 in range(2 * qk_width, qkv_width, step):
        qkv_ref[:, c0:c0 + step] = jnp.dot(h, w_ref[:, c0:c0 + step], preferred_element_type=F32).astype(BF16)
    for c0 in range(0, rest_ref.shape[1], step):
        rest_ref[:, c0:c0 + step] = jnp.dot(
            h, w_ref[:, qkv_width + c0:qkv_width + c0 + step], preferred_element_type=F32).astype(BF16)


def _inproj(xs, mods, gain, w_bf16, cos_t, sin_t, *, tm, tpb, n_batch, qk_width, qkv_width):
    t, d = xs.shape
    in_width = w_bf16.shape[1]
    rest_width = in_width - qkv_width
    pos_blocks = cos_t.shape[0] // tm - 1

    def seg(i):
        return jnp.where(i % tpb == 0, n_batch, i // tpb)

    def pos(i):
        j = i % tpb
        return jnp.where(j == 0, pos_blocks, j - 1)

    return pl.pallas_call(
        functools.partial(_inproj_kernel, qk_width=qk_width),
        grid=(t // tm,),
        in_specs=[
            pl.BlockSpec((tm, d), lambda i: (i, 0)),
            pl.BlockSpec((1, SUBLANES, d), lambda i: (seg(i), 0, 0)),
            pl.BlockSpec((1, d), lambda i: (0, 0)),
            pl.BlockSpec((d, in_width), lambda i: (0, 0)),
            pl.BlockSpec((tm, RET_DK), lambda i: (pos(i), 0)),
            pl.BlockSpec((tm, RET_DK), lambda i: (pos(i), 0)),
        ],
        out_specs=[
            pl.BlockSpec((tm, qkv_width), lambda i: (i, 0)),
            pl.BlockSpec((tm, rest_width), lambda i: (i, 0)),
        ],
        out_shape=[
            jax.ShapeDtypeStruct((t, qkv_width), BF16),
            jax.ShapeDtypeStruct((t, rest_width), BF16),
        ],
        compiler_params=_params(1),
        name="inproj",
    )(xs, mods, gain, w_bf16, cos_t, sin_t)


def _retention_kernel(gc_ref, q_ref, k_ref, v_ref, dec_ref, dmat_ref, y_ref, sf_ref, sb_ref, st_ref, *, lc, n):
    head = pl.program_id(1)
    c = RET_CHUNK
    qd = dec_ref[0, :, 0:2 * RET_DK]
    kdf = dec_ref[0, :, 2 * RET_DK:3 * RET_DK]
    kdb = dec_ref[0, :, 3 * RET_DK:4 * RET_DK]
    gcf = gc_ref[head, 0]
    gcb = gc_ref[head, 1]

    def rows(chunk):
        return pl.ds(pl.multiple_of(chunk * c, c), c)

    def advance(s_ref, chunk, kd, gc):
        kdec = (k_ref[rows(chunk), :].astype(F32) * kd).astype(BF16)
        kv = lax.dot_general(kdec, v_ref[rows(chunk), :], TN_DIMS, preferred_element_type=F32)
        s_ref[...] = gc * s_ref[...] + kv

    def sweep(chunk0, nchunks):
        def states(i, carry):
            cf = chunk0 + i
            cb = chunk0 + nchunks - 1 - i
            st_ref[cf, 0:RET_DK, :] = sf_ref[...].astype(BF16)
            st_ref[cb, RET_DK:2 * RET_DK, :] = sb_ref[...].astype(BF16)
            advance(sf_ref, cf, kdf, gcf)
            advance(sb_ref, cb, kdb, gcb)
            return carry

        def outputs(i, carry):
            r = rows(chunk0 + i)
            q, k, v = q_ref[r, :], k_ref[r, :], v_ref[r, :]
            scores = lax.dot_general(q, k, NT_DIMS, preferred_element_type=F32)
            qdec = (jnp.concatenate([q, q], axis=1).astype(F32) * qd).astype(BF16)
            y = jnp.dot((scores * dmat_ref[0]).astype(BF16), v, preferred_element_type=F32)
            y += jnp.dot(qdec, st_ref[chunk0 + i], preferred_element_type=F32)
            y_ref[r, :] = (y * lax.rsqrt(jnp.mean(y * y, axis=-1, keepdims=True) + EPS)).astype(BF16)
            return carry

        unroll = 2 if nchunks % 2 == 0 else 1
        lax.fori_loop(0, nchunks, states, 0, unroll=unroll)
        lax.fori_loop(0, nchunks, outputs, 0, unroll=unroll)

    sf_ref[...] = jnp.zeros_like(sf_ref)
    sb_ref[...] = jnp.zeros_like(sb_ref)
    sweep(0, lc // c)
    sweep(lc // c, n // c)


def _retention(qkv, dec, dmat, gc, *, n_batch, lc, n, qk_width):
    s = lc + n
    t = qkv.shape[0]
    k_blk0 = qk_width // RET_DK
    v_blk0 = 2 * qk_width // RET_DV
    return pl.pallas_call(
        functools.partial(_retention_kernel, lc=lc, n=n),
        grid_spec=pltpu.PrefetchScalarGridSpec(
            num_scalar_prefetch=1,
            grid=(n_batch, RET_HEADS),
            in_specs=[
                pl.BlockSpec((s, RET_DK), lambda b, h, gc: (b, h)),
                pl.BlockSpec((s, RET_DK), lambda b, h, gc: (b, k_blk0 + h)),
                pl.BlockSpec((s, RET_DV), lambda b, h, gc: (b, v_blk0 + h)),
                pl.BlockSpec((1, RET_CHUNK, 4 * RET_DK), lambda b, h, gc: (h, 0, 0)),
                pl.BlockSpec((1, RET_CHUNK, RET_CHUNK), lambda b, h, gc: (h, 0, 0)),
            ],
            out_specs=pl.BlockSpec((s, RET_DV), lambda b, h, gc: (b, h)),
            scratch_shapes=[
                pltpu.VMEM((RET_DK, RET_DV), F32),
                pltpu.VMEM((RET_DK, RET_DV), F32),
                pltpu.VMEM((s // RET_CHUNK, 2 * RET_DK, RET_DV), BF16),
            ],
        ),
        out_shape=jax.ShapeDtypeStruct((t, RET_HEADS * RET_DV), BF16),
        compiler_params=_params(2),
        name="retention",
    )(gc, qkv, qkv, qkv, dec, dmat)


def _decay_tables(logit_f, logit_b):
    c = RET_CHUNK
    lg_f = jax.nn.log_sigmoid(logit_f.astype(F32))[:, None]
    lg_b = jax.nn.log_sigmoid(logit_b.astype(F32))[:, None]
    i = jnp.arange(c, dtype=F32)[None, :]
    vecs = jnp.stack([
        jnp.exp((i + 1) * lg_f), jnp.exp((c - i) * lg_b),
        jnp.exp((c - 1 - i) * lg_f), jnp.exp(i * lg_b)], axis=2)
    dec = jnp.broadcast_to(vecs[..., None], vecs.shape + (RET_DK,)).reshape(vecs.shape[0], c, 4 * RET_DK)
    diff = i[0][:, None] - i[0][None, :]
    lower = jnp.where(diff >= 0, jnp.exp(jnp.where(diff >= 0, diff, 0.0)[None] * lg_f[:, :, None]), 0.0)
    upper = jnp.where(diff < 0, jnp.exp(jnp.where(diff < 0, -diff, 0.0)[None] * lg_b[:, :, None]), 0.0)
    gc = jnp.concatenate([jnp.exp(c * lg_f), jnp.exp(c * lg_b)], axis=1)
    return dec, lower + upper, gc


def _mix_kernel(x_ref, y_ref, rest_ref, mod_ref, cw_ref, wo_ref, gain2_ref, rw_ref, rb_ref,
                o_ref, h2_ref, ei_ref, wt_ref, cnt_ref, h_prev, *, tpb, ntiles, ret_width, conv_width):
    step = pl.program_id(0)

    @pl.when(step == 0)
    def _():
        h_prev[...] = jnp.zeros_like(h_prev)

    _route(h_prev[...], rw_ref, rb_ref, ei_ref, wt_ref, cnt_ref)
    is_ctx = (jnp.minimum(step, ntiles - 1) % tpb) == 0
    tm = x_ref.shape[0]
    g = rest_ref[:, 0:ret_width].astype(F32)
    cb = rest_ref[:, ret_width:ret_width + conv_width].astype(F32)
    cc = rest_ref[:, ret_width + conv_width:ret_width + 2 * conv_width].astype(F32)
    cx = rest_ref[:, ret_width + 2 * conv_width:ret_width + 3 * conv_width].astype(F32)
    ret = (g * _sigmoid(g) * y_ref[...].astype(F32)).astype(BF16)
    u = cc * cx
    row = lax.broadcasted_iota(jnp.int32, u.shape, 0)
    pos = jnp.where(is_ctx, row, row % GRID_W)
    last = jnp.where(is_ctx, tm - 1, GRID_W - 1)
    u_prev = jnp.where(pos == 0, 0.0, pltpu.roll(u, 1, 0))
    u_next = jnp.where(pos == last, 0.0, pltpu.roll(u, tm - 1, 0))
    conv = (cb * (u_prev * cw_ref[0:1, :] + u * cw_ref[1:2, :] + u_next * cw_ref[2:3, :])).astype(BF16)
    mix = jnp.dot(ret, wo_ref[0:ret_width, :], preferred_element_type=F32)
    mix += jnp.dot(conv, wo_ref[ret_width:ret_width + conv_width, :], preferred_element_type=F32)
    x_new = x_ref[...] + mod_ref[0, 2:3, :] * mix
    o_ref[...] = x_new
    h2 = _modulated_norm(x_new, gain2_ref[...], mod_ref[0, 3:4, :], mod_ref[0, 4:5, :])
    h2_ref[...] = h2.astype(BF16)
    h_prev[...] = h2


def _mix(xs, y, rest, mods, conv_w8, wo_bf16, gain2, rw_t, rb_col, *, tm, tpb, n_batch):
    t, d = xs.shape
    ntiles = t // tm
    ret_width = y.shape[1]
    conv_width = conv_w8.shape[1]
    n_exp = rw_t.shape[0]

    def cur(s):
        return jnp.minimum(s, ntiles - 1)

    def prev(s):
        return jnp.maximum(s - 1, 0)

    def seg(i):
        return jnp.where(i % tpb == 0, n_batch, i // tpb)

    return pl.pallas_call(
        functools.partial(_mix_kernel, tpb=tpb, ntiles=ntiles, ret_width=ret_width, conv_width=conv_width),
        grid=(ntiles + 1,),
        in_specs=[
            pl.BlockSpec((tm, d), lambda s: (cur(s), 0)),
            pl.BlockSpec((tm, ret_width), lambda s: (cur(s), 0)),
            pl.BlockSpec((tm, rest.shape[1]), lambda s: (cur(s), 0)),
            pl.BlockSpec((1, SUBLANES, d), lambda s: (seg(cur(s)), 0, 0)),
            pl.BlockSpec((SUBLANES, conv_width), lambda s: (0, 0)),
            pl.BlockSpec(wo_bf16.shape, lambda s: (0, 0)),
            pl.BlockSpec((1, d), lambda s: (0, 0)),
            pl.BlockSpec((n_exp, d), lambda s: (0, 0)),
            pl.BlockSpec((n_exp, LANES), lambda s: (0, 0)),
        ],
        out_specs=[
            pl.BlockSpec((tm, d), lambda s: (cur(s), 0)),
            pl.BlockSpec((tm, d), lambda s: (cur(s), 0)),
            pl.BlockSpec((1, SUBLANES, tm), lambda s: (prev(s), 0, 0)),
            pl.BlockSpec((1, SUBLANES, tm), lambda s: (prev(s), 0, 0)),
            pl.BlockSpec((1, n_exp, LANES), lambda s: (prev(s), 0, 0)),
        ],
        out_shape=[
            jax.ShapeDtypeStruct((t, d), F32),
            jax.ShapeDtypeStruct((t, d), BF16),
            jax.ShapeDtypeStruct((ntiles, SUBLANES, tm), jnp.int32),
            jax.ShapeDtypeStruct((ntiles, SUBLANES, tm), F32),
            jax.ShapeDtypeStruct((ntiles, n_exp, LANES), jnp.int32),
        ],
        scratch_shapes=[pltpu.VMEM((tm, d), F32)],
        compiler_params=_params(1),
        name="mix",
    )(xs, y, rest, mods, conv_w8, wo_bf16, gain2, rw_t, rb_col)


def _route(h2, rw_ref, rb_ref, ei_ref, wt_ref, cnt_ref):
    h_hi, h_lo = _split_bf16(h2)
    w_hi, w_lo = _split_bf16(rw_ref[...])
    logits = lax.dot_general(w_hi, h_hi, NT_DIMS, preferred_element_type=F32)
    logits += lax.dot_general(w_hi, h_lo, NT_DIMS, preferred_element_type=F32)
    logits += lax.dot_general(w_lo, h_hi, NT_DIMS, preferred_element_type=F32)
    logits += rb_ref[:, 0:1]
    n_exp, tt = logits.shape
    expert = lax.broadcasted_iota(jnp.int32, (n_exp, tt), 0).astype(F32)
    slot = lax.broadcasted_iota(jnp.int32, (SUBLANES, tt), 0)
    ei = jnp.zeros((SUBLANES, tt), F32)
    ev = jnp.zeros((SUBLANES, tt), F32)
    taken = jnp.zeros((n_exp, tt), F32)
    top = None
    denom = jnp.zeros((1, tt), F32)
    for k in range(TOP_K):
        m = jnp.max(logits, axis=0, keepdims=True)
        idx = jnp.min(jnp.where(logits == m, expert, float(n_exp)), axis=0, keepdims=True)
        hit = expert == idx
        taken += hit.astype(F32)
        logits = jnp.where(hit, -jnp.inf, logits)
        if k == 0:
            top = m
        e = jnp.exp(m - top)
        denom += e
        ei = jnp.where(slot == k, idx, ei)
        ev = jnp.where(slot == k, e, ev)
    ei_ref[0] = ei.astype(jnp.int32)
    wt_ref[0] = jnp.where(slot < TOP_K, ev / denom, 0.0)
    cnt = jnp.sum(taken, axis=1, keepdims=True)
    cnt_ref[0] = jnp.broadcast_to(cnt, (n_exp, LANES)).astype(jnp.int32)


def _routing_plan(cnt, *, tm, sorted_rows):
    ntiles, n_exp = cnt.shape
    seg_len = (cnt + ROW_ALIGN - 1) // ROW_ALIGN * ROW_ALIGN
    off = jnp.cumsum(seg_len, axis=1) - seg_len
    tile_rows = jnp.sum(seg_len, axis=1)
    before = jnp.cumsum(seg_len, axis=0) - seg_len
    filled = jnp.sum(seg_len, axis=0)
    region = (filled + EXPERT_TILE - 1) // EXPERT_TILE * EXPERT_TILE
    region_end = jnp.cumsum(region)
    region_start = region_end - region
    base = region_start[None, :] + before

    nb_max = _max_blocks(ntiles, sorted_rows, n_exp)
    dump0 = nb_max * EXPERT_TILE
    nch = sorted_rows // ROW_ALIGN
    q = jnp.arange(nch, dtype=jnp.int32) * ROW_ALIGN
    seg_end = off + seg_len
    inside = jnp.logical_and(off[:, None, :] <= q[None, :, None], q[None, :, None] < seg_end[:, None, :])
    valid = q[None, :] < tile_rows[:, None]
    row = q[None, :] + jnp.sum(jnp.where(inside, (base - off)[:, None, :], 0), axis=2)
    parity = (jnp.arange(ntiles, dtype=jnp.int32) % 2)[:, None]
    chunk_dst = jnp.where(valid, row, dump0 + parity * sorted_rows + q[None, :]).astype(jnp.int32)
    chunk_src = jnp.where(valid, row, 0).astype(jnp.int32)

    sizes = jnp.asarray(FILL_SIZES, dtype=jnp.int32)[None, :]
    gap = (region - filled)[:, None]
    used = (gap & sizes) != 0
    fill_row = filled[:, None] + (gap & ~(2 * sizes - 1))
    fill_dump = dump0 + 2 * sorted_rows + jnp.arange(n_exp, dtype=jnp.int32)[:, None] * EXPERT_TILE + (
        EXPERT_TILE - 2 * sizes)
    fill_dst = jnp.where(used, region_start[:, None] + fill_row, fill_dump).reshape(-1).astype(jnp.int32)

    blk = jnp.arange(nb_max, dtype=jnp.int32) * EXPERT_TILE
    block_e = jnp.sum((region_end[None, :] <= blk[:, None]).astype(jnp.int32), axis=1)
    block_e = jnp.minimum(block_e, n_exp - 1).astype(jnp.int32)
    n_used = (region_end[-1] // EXPERT_TILE).astype(jnp.int32).reshape(1)
    return off, chunk_dst, chunk_src, fill_dst, block_e, n_used


def _max_blocks(ntiles, sorted_rows, n_exp):
    return -(-(ntiles * sorted_rows + n_exp * EXPERT_TILE) // EXPERT_TILE)


def _sorted_positions(ei, off_col, tt):
    n_exp = off_col.shape[0]
    expert = lax.broadcasted_iota(jnp.int32, (n_exp, tt), 0)
    hits = [expert == ei[k:k + 1, :] for k in range(TOP_K)]
    chosen = sum(h.astype(F32) for h in hits).astype(BF16)
    t_row = lax.broadcasted_iota(jnp.int32, (tt, tt), 0)
    t_col = lax.broadcasted_iota(jnp.int32, (tt, tt), 1)
    earlier = (t_row < t_col).astype(BF16)
    rank = jnp.dot(chosen, earlier, preferred_element_type=F32) + off_col
    return [jnp.sum(jnp.where(h, rank, 0.0), axis=0, keepdims=True) for h in hits]


def _dispatch_kernel(cd_ref, fd_ref, h2_ref, ei_ref, off_ref, xb_ref, pos_ref, buf, zeros, sem, fill_sem):
    step = pl.program_id(0)
    tt = h2_ref.shape[0]
    sorted_rows = buf.shape[1]
    nch = sorted_rows // ROW_ALIGN

    def fill_copy(e, s):
        size = FILL_SIZES[s]
        dst = pl.multiple_of(fd_ref[e * len(FILL_SIZES) + s], ROW_ALIGN)
        return pltpu.make_async_copy(zeros.at[pl.ds(0, size), :], xb_ref.at[pl.ds(dst, size), :], fill_sem)

    @pl.when(step == 0)
    def _():
        zeros[...] = jnp.zeros_like(zeros)

        def start(e, c):
            for s in range(len(FILL_SIZES)):
                fill_copy(e, s).start()
            return c

        def wait(e, c):
            for s in range(len(FILL_SIZES)):
                fill_copy(e, s).wait()
            return c

        n_regions = fd_ref.shape[0] // len(FILL_SIZES)
        lax.fori_loop(0, n_regions, start, 0)
        lax.fori_loop(0, n_regions, wait, 0)

    cur = step % 2
    last_step = pl.num_programs(0) - 1

    def chunk_copy(slot, j):
        dst = pl.multiple_of(cd_ref[0, 0, j], ROW_ALIGN)
        return pltpu.make_async_copy(buf.at[slot, pl.ds(j * ROW_ALIGN, ROW_ALIGN), :],
                                     xb_ref.at[pl.ds(dst, ROW_ALIGN), :], sem.at[slot])

    def drain(slot):
        for j in range(nch):
            chunk_copy(slot, j).wait()

    @pl.when(step >= 2)
    def _():
        drain(cur)

    pos = _sorted_positions(ei_ref[0], off_ref[0][:, 0:1], tt)
    slot_id = lax.broadcasted_iota(jnp.int32, (SUBLANES, tt), 0)
    pos8 = jnp.zeros((SUBLANES, tt), F32)
    for k in range(TOP_K):
        pos8 = jnp.where(slot_id == k, pos[k], pos8)
    pos_ref[0] = pos8
    r = lax.broadcasted_iota(jnp.int32, (sorted_rows, tt), 0)
    hit = r == pos[0].astype(jnp.int32)
    for k in range(1, TOP_K):
        hit = jnp.logical_or(hit, r == pos[k].astype(jnp.int32))
    perm = jnp.where(hit, 1.0, 0.0).astype(BF16)
    buf[cur] = _pack_bf16_pairs(jnp.dot(perm, h2_ref[...], preferred_element_type=F32))
    for j in range(nch):
        chunk_copy(cur, j).start()

    @pl.when(step == last_step)
    def _():
        drain(cur)

    @pl.when(jnp.logical_and(step == last_step, step >= 1))
    def _():
        drain(1 - cur)


def _dispatch(h2, ei, off_col, chunk_dst, fill_dst, *, tm, sorted_rows, total_rows):
    t, d = h2.shape
    ntiles = t // tm
    n_exp = off_col.shape[1]
    nch = sorted_rows // ROW_ALIGN

    return pl.pallas_call(
        _dispatch_kernel,
        grid=(ntiles,),
        in_specs=[
            pl.BlockSpec((1, 1, nch), lambda i: (i, 0, 0), memory_space=pltpu.SMEM),
            pl.BlockSpec(memory_space=pltpu.SMEM),
            pl.BlockSpec((tm, d), lambda i: (i, 0)),
            pl.BlockSpec((1, SUBLANES, tm), lambda i: (i, 0, 0)),
            pl.BlockSpec((1, n_exp, LANES), lambda i: (i, 0, 0)),
        ],
        out_specs=[
            pl.BlockSpec(memory_space=pl.ANY),
            pl.BlockSpec((1, SUBLANES, tm), lambda i: (i, 0, 0)),
        ],
        out_shape=[
            jax.ShapeDtypeStruct((total_rows, d // 2), jnp.uint32),
            jax.ShapeDtypeStruct((ntiles, SUBLANES, tm), F32),
        ],
        scratch_shapes=[
            pltpu.VMEM((2, sorted_rows, d // 2), jnp.uint32),
            pltpu.VMEM((FILL_SIZES[0], d // 2), jnp.uint32),
            pltpu.SemaphoreType.DMA((2,)),
            pltpu.SemaphoreType.DMA(()),
        ],
        compiler_params=_params(1),
        name="dispatch",
    )(chunk_dst.reshape(ntiles, 1, nch), fill_dst, h2, ei, off_col)


def _experts_kernel(be_ref, nu_ref, xb_ref, wu_ref, bu_ref, wd_ref, bd_ref, yb_ref, wu_s, wd_s, *, d_ff):
    i = pl.program_id(0)
    prev = be_ref[jnp.maximum(i - 1, 0)]
    fresh = jnp.logical_or(i == 0, be_ref[i] != prev)

    @pl.when(jnp.logical_and(fresh, i < nu_ref[0]))
    def _():
        wu_s[...] = wu_ref[0, 0].astype(BF16)
        wd_s[...] = wd_ref[0, 0].astype(BF16)

    @pl.when(i < nu_ref[0])
    def _():
        u = jnp.dot(_unpack_bf16_pairs(xb_ref[...]), wu_s[...], preferred_element_type=F32) + bu_ref[0, 0]
        gate = jnp.minimum(u[:, :d_ff], SWIGLU_LIMIT)
        lin = jnp.clip(u[:, d_ff:], -SWIGLU_LIMIT, SWIGLU_LIMIT)
        act = (gate * _sigmoid(SWIGLU_ALPHA * gate) * (lin + 1.0)).astype(BF16)
        y = jnp.dot(act, wd_s[...], preferred_element_type=F32) + bd_ref[0, 0]
        yb_ref[...] = _pack_bf16_pairs(y.astype(BF16).astype(F32))


def _experts(xb, block_e, n_used, w_up, b_up, w_down, b_down, *, layer, nb_max):
    depth, n_exp, d, two_ff = w_up.shape
    d_ff = two_ff // 2
    dp = xb.shape[1]

    def blk(i, be, nu):
        return jnp.minimum(i, nu[0] - 1)

    def exp(i, be, nu):
        return be[jnp.minimum(i, nu[0] - 1)]

    return pl.pallas_call(
        functools.partial(_experts_kernel, d_ff=d_ff),
        grid_spec=pltpu.PrefetchScalarGridSpec(
            num_scalar_prefetch=2,
            grid=(nb_max,),
            in_specs=[
                pl.BlockSpec((EXPERT_TILE, dp), lambda i, be, nu: (blk(i, be, nu), 0)),
                pl.BlockSpec((1, 1, d, two_ff), lambda i, be, nu: (layer, exp(i, be, nu), 0, 0)),
                pl.BlockSpec((1, 1, 1, two_ff), lambda i, be, nu: (layer, exp(i, be, nu), 0, 0)),
                pl.BlockSpec((1, 1, d_ff, d), lambda i, be, nu: (layer, exp(i, be, nu), 0, 0)),
                pl.BlockSpec((1, 1, 1, d), lambda i, be, nu: (layer, exp(i, be, nu), 0, 0)),
            ],
            out_specs=pl.BlockSpec((EXPERT_TILE, dp), lambda i, be, nu: (blk(i, be, nu), 0)),
            scratch_shapes=[pltpu.VMEM((d, two_ff), BF16), pltpu.VMEM((d_ff, d), BF16)],
        ),
        out_shape=jax.ShapeDtypeStruct((nb_max * EXPERT_TILE, dp), jnp.uint32),
        compiler_params=_params(1),
        name="experts",
    )(block_e, n_used, xb, w_up, b_up.reshape(depth, n_exp, 1, two_ff), w_down, b_down.reshape(depth, n_exp, 1, d))


def _combine_kernel(cs_ref, csn_ref, x_ref, mod_ref, pos_ref, wt_ref, yb_ref, *rest, qk_width, follow):
    if follow == "inproj":
        mod2_ref, gain_ref, w_ref, cos_ref, sin_ref, o_ref, qkv_ref, rest_ref, buf, sem, x_prev = rest

        @pl.when(pl.program_id(0) == 0)
        def _():
            x_prev[...] = jnp.zeros_like(x_prev)

        _combine_fetch(cs_ref, csn_ref, yb_ref, buf, sem)()
        _inproj_tile(x_prev[...], mod2_ref, gain_ref, w_ref, cos_ref, sin_ref, qkv_ref, rest_ref, qk_width)
        x_new = _combine_tile(x_ref, mod_ref, pos_ref, wt_ref, buf)
        o_ref[...] = x_new
        x_prev[...] = x_new
    else:
        gain_ref, o_ref, buf, sem = rest
        wait = _combine_fetch(cs_ref, csn_ref, yb_ref, buf, sem)
        x_new = _combine_tile(x_ref, mod_ref, pos_ref, wt_ref, buf, wait)
        ms = jnp.mean(x_new * x_new, axis=-1, keepdims=True)
        o_ref[0] = x_new * lax.rsqrt(ms + EPS) * gain_ref[...]


def _combine_fetch(cs_ref, csn_ref, yb_ref, buf, sem):
    step = pl.program_id(0)
    cur = step % 2
    nch = buf.shape[1] // ROW_ALIGN

    def chunk_copy(table, slot, j):
        src = pl.multiple_of(table[0, 0, j], ROW_ALIGN)
        return pltpu.make_async_copy(yb_ref.at[pl.ds(src, ROW_ALIGN), :],
                                     buf.at[slot, pl.ds(j * ROW_ALIGN, ROW_ALIGN), :], sem.at[slot])

    @pl.when(step == 0)
    def _():
        for j in range(nch):
            chunk_copy(cs_ref, cur, j).start()

    @pl.when(step + 1 < pl.num_programs(0))
    def _():
        for j in range(nch):
            chunk_copy(csn_ref, 1 - cur, j).start()

    def wait():
        for j in range(nch):
            chunk_copy(cs_ref, cur, j).wait()

    return wait


def _combine_tile(x_ref, mod_ref, pos_ref, wt_ref, buf, wait=None):
    tt = x_ref.shape[0]
    sorted_rows = buf.shape[1]
    r = lax.broadcasted_iota(jnp.int32, (sorted_rows, tt), 0)
    pos = pos_ref[0].astype(jnp.int32)
    wt = wt_ref[0]
    sel = jnp.zeros((sorted_rows, tt), F32)
    for k in reversed(range(TOP_K)):
        sel = jnp.where(r == pos[k:k + 1, :], wt[k:k + 1, :], sel)
    if wait is not None:
        wait()
    y2 = lax.dot_general(sel.astype(BF16), _unpack_bf16_pairs(buf[pl.program_id(0) % 2]), TN_DIMS,
                         preferred_element_type=F32)
    return x_ref[...] + mod_ref[0, 5:6, :] * y2


def _combine(xs, mods, pos, wt, chunk_src, yb, *, tm, tpb, n_batch, sorted_rows, inproj=None, final=None):
    t, d = xs.shape
    ntiles = t // tm
    nch = sorted_rows // ROW_ALIGN

    def seg(i):
        return jnp.where(i % tpb == 0, n_batch, i // tpb)

    def cur(s):
        return jnp.minimum(s, ntiles - 1)

    def prev(s):
        return jnp.maximum(s - 1, 0)

    in_specs = [
        pl.BlockSpec((1, 1, nch), lambda s: (cur(s), 0, 0), memory_space=pltpu.SMEM),
        pl.BlockSpec((1, 1, nch), lambda s: (cur(s + 1), 0, 0), memory_space=pltpu.SMEM),
        pl.BlockSpec((tm, d), lambda s: (cur(s), 0)),
        pl.BlockSpec((1, SUBLANES, d), lambda s: (seg(cur(s)), 0, 0)),
        pl.BlockSpec((1, SUBLANES, tm), lambda s: (cur(s), 0, 0)),
        pl.BlockSpec((1, SUBLANES, tm), lambda s: (cur(s), 0, 0)),
        pl.BlockSpec(memory_space=pl.ANY),
    ]
    args = [chunk_src.reshape(ntiles, 1, nch), chunk_src.reshape(ntiles, 1, nch), xs, mods, pos, wt, yb]
    scratch = [pltpu.VMEM((2, sorted_rows, yb.shape[1]), jnp.uint32), pltpu.SemaphoreType.DMA((2,))]
    if inproj is not None:
        mods2, gain, w_bf16, cos_t, sin_t, qk_width, qkv_width = inproj
        in_width = w_bf16.shape[1]
        pos_blocks = cos_t.shape[0] // tm - 1

        def rope_blk(i):
            j = i % tpb
            return jnp.where(j == 0, pos_blocks, j - 1)

        in_specs += [
            pl.BlockSpec((1, SUBLANES, d), lambda s: (seg(prev(s)), 0, 0)),
            pl.BlockSpec((1, d), lambda s: (0, 0)),
            pl.BlockSpec((d, in_width), lambda s: (0, 0)),
            pl.BlockSpec((tm, RET_DK), lambda s: (rope_blk(prev(s)), 0)),
            pl.BlockSpec((tm, RET_DK), lambda s: (rope_blk(prev(s)), 0)),
        ]
        args += [mods2, gain, w_bf16, cos_t, sin_t]
        out_specs = [pl.BlockSpec((tm, d), lambda s: (cur(s), 0)),
                     pl.BlockSpec((tm, qkv_width), lambda s: (prev(s), 0)),
                     pl.BlockSpec((tm, in_width - qkv_width), lambda s: (prev(s), 0))]
        out_shape = [jax.ShapeDtypeStruct((t, d), F32), jax.ShapeDtypeStruct((t, qkv_width), BF16),
                     jax.ShapeDtypeStruct((t, in_width - qkv_width), BF16)]
        scratch += [pltpu.VMEM((tm, d), F32)]
        steps, follow = ntiles + 1, "inproj"
    else:
        gain, n = final
        qk_width = None
        in_specs += [pl.BlockSpec((1, d), lambda s: (0, 0))]
        args += [gain]
        out_specs = pl.BlockSpec((1, tm, d), lambda s: (s // tpb, jnp.maximum(s % tpb - 1, 0), 0))
        out_shape = jax.ShapeDtypeStruct((n_batch, n, d), F32)
        steps, follow = ntiles, "final"

    return pl.pallas_call(
        functools.partial(_combine_kernel, qk_width=qk_width, follow=follow),
        grid=(steps,),
        in_specs=in_specs,
        out_specs=out_specs,
        out_shape=out_shape,
        scratch_shapes=scratch,
        compiler_params=_params(1),
        name="combine",
    )(*args)


def _rope_tables(n, tm):
    t = jnp.arange(n, dtype=jnp.int32)
    m = RET_DK // 4
    freqs = ROPE_BASE ** (-jnp.arange(m, dtype=F32) / m)
    ang_r = (t // GRID_W).astype(F32)[:, None] * freqs[None, :]
    ang_c = (t % GRID_W).astype(F32)[:, None] * freqs[None, :]
    cos = jnp.concatenate([jnp.cos(ang_r)] * 2 + [jnp.cos(ang_c)] * 2, axis=1)
    sin = jnp.concatenate([-jnp.sin(ang_r), jnp.sin(ang_r), -jnp.sin(ang_c), jnp.sin(ang_c)], axis=1)
    cos = jnp.concatenate([cos, jnp.ones((tm, RET_DK), F32)], axis=0)
    sin = jnp.concatenate([sin, jnp.zeros((tm, RET_DK), F32)], axis=0)
    return cos, sin


def kernel(x, c, ctx, c_ctx, w_mod, b_mod, norm_mix, norm_ffn, w_in, ret_decay_f, ret_decay_b, conv_w, w_out,
           router_w, router_b, w_up, b_up, w_down, b_down, norm_final):
    n_batch, n, d = x.shape
    lc = ctx.shape[1]
    depth = w_mod.shape[0]
    tm = lc
    assert tm % 256 == 0 and n % tm == 0 and n % GRID_W == 0 and tm % GRID_W == 0
    assert n % RET_CHUNK == 0 and lc % RET_CHUNK == 0 and n_batch + 1 <= SUBLANES
    tpb = (lc + n) // tm
    qk_width = RET_HEADS * RET_DK
    qkv_width = 2 * qk_width + RET_HEADS * RET_DV
    n_exp = router_w.shape[2]
    sorted_rows = TOP_K * tm + n_exp * ROW_ALIGN
    ntiles = n_batch * tpb
    nb_max = _max_blocks(ntiles, sorted_rows, n_exp)
    total_rows = nb_max * EXPERT_TILE + 2 * sorted_rows + n_exp * EXPERT_TILE

    xs = jnp.concatenate([ctx, x], axis=1).reshape(n_batch * (lc + n), d)
    cond = jnp.zeros((SUBLANES, d), F32).at[:n_batch].set(c).at[n_batch].set(c_ctx)
    mod_all = _modulation(cond, w_mod, b_mod).reshape(depth, SUBLANES, 6, d)
    mod_all = jnp.pad(mod_all, ((0, 0), (0, 0), (0, SUBLANES - 6), (0, 0)))
    cos_t, sin_t = _rope_tables(n, tm)

    tile_args = dict(tm=tm, tpb=tpb, n_batch=n_batch)
    qkv, rest = _inproj(xs, mod_all[0], norm_mix[0][None, :], w_in[0].astype(BF16), cos_t, sin_t,
                        qk_width=qk_width, qkv_width=qkv_width, **tile_args)
    for layer in range(depth):
        mods = mod_all[layer]
        dec, dmat, gc = _decay_tables(ret_decay_f[layer], ret_decay_b[layer])
        y = _retention(qkv, dec, dmat, gc, n_batch=n_batch, lc=lc, n=n, qk_width=qk_width)
        conv_w8 = jnp.pad(conv_w[layer], ((0, SUBLANES - conv_w.shape[1]), (0, 0)))
        gain2 = norm_ffn[layer][None, :]
        rb_col = jnp.broadcast_to(router_b[layer][:, None], (n_exp, LANES))
        xs, h2, ei, wt, cnt = _mix(xs, y, rest, mods, conv_w8, w_out[layer].astype(BF16), gain2, router_w[layer].T,
                                   rb_col, **tile_args)
        off, chunk_dst, chunk_src, fill_dst, block_e, n_used = _routing_plan(
            cnt[:, :, 0], tm=tm, sorted_rows=sorted_rows)
        off_col = jnp.broadcast_to(off.astype(F32)[:, :, None], (ntiles, n_exp, LANES))
        xb, pos = _dispatch(h2, ei, off_col, chunk_dst, fill_dst, tm=tm, sorted_rows=sorted_rows,
                            total_rows=total_rows)
        yb = _experts(xb, block_e, n_used, w_up, b_up, w_down, b_down, layer=layer, nb_max=nb_max)
        if layer + 1 == depth:
            return _combine(xs, mods, pos, wt, chunk_src, yb, sorted_rows=sorted_rows,
                            final=(norm_final[None, :], n), **tile_args)
        nxt = layer + 1
        xs, qkv, rest = _combine(
            xs, mods, pos, wt, chunk_src, yb, sorted_rows=sorted_rows,
            inproj=(mod_all[nxt], norm_mix[nxt][None, :], w_in[nxt].astype(BF16), cos_t, sin_t, qk_width, qkv_width),
            **tile_args)
```

```python
import functools

import jax
import jax.numpy as jnp
from jax import lax
from jax.experimental import pallas as pl
from jax.experimental.pallas import tpu as pltpu

GRID_W = 64
RET_HEADS = 4
RET_DK = 128
RET_DV = 256
RET_CHUNK = 256
N_EXPERTS = 32
TOP_K = 4
SWIGLU_LIMIT = 7.0
SWIGLU_ALPHA = 1.702
ROPE_BASE = 10000.0
EPS = 1e-6

SUBLANES = 8
LANES = 128
VMEM_LIMIT = 56 * 1024 * 1024

EXPERT_TILE = 512
ROW_ALIGN = SUBLANES
FILL_SIZES = tuple(EXPERT_TILE >> k for k in range(1, EXPERT_TILE.bit_length()) if EXPERT_TILE >> k >= ROW_ALIGN)

F32 = jnp.float32
BF16 = jnp.bfloat16
NT_DIMS = (((1,), (1,)), ((), ()))
TN_DIMS = (((0,), (0,)), ((), ()))


def _params(n_axes=1):
    return pltpu.CompilerParams(dimension_semantics=("arbitrary",) * n_axes, vmem_limit_bytes=VMEM_LIMIT)


def _modulated_norm(x, gain, shift, scale):
    ms = jnp.mean(x * x, axis=-1, keepdims=True)
    return x * lax.rsqrt(ms + EPS) * gain * (1.0 + scale) + shift


def _sigmoid(z):
    return 1.0 / (1.0 + jnp.exp(-z))


def _split_bf16(a):
    hi = a.astype(BF16)
    lo = (a - hi.astype(F32)).astype(BF16)
    return hi, lo


def _pack_bf16_pairs(a):
    w = a.shape[1] // 2
    lo = lax.bitcast_convert_type(a[:, :w], jnp.uint32)
    hi = lax.bitcast_convert_type(a[:, w:], jnp.uint32)
    return (lo >> 16) | (hi & jnp.uint32(0xFFFF0000))


def _unpack_bf16_pairs(p):
    lo = lax.bitcast_convert_type(p << 16, F32)
    hi = lax.bitcast_convert_type(p & jnp.uint32(0xFFFF0000), F32)
    return jnp.concatenate([lo, hi], axis=1).astype(BF16)


def _modulation_kernel(c_ref, w_ref, b_ref, o_ref):
    c = c_ref[...]
    s = c * _sigmoid(c)
    s_hi, s_lo = _split_bf16(s)
    w_hi, w_lo = _split_bf16(w_ref[0])
    acc = jnp.dot(s_hi, w_hi, preferred_element_type=F32)
    acc += jnp.dot(s_hi, w_lo, preferred_element_type=F32)
    acc += jnp.dot(s_lo, w_hi, preferred_element_type=F32)
    o_ref[0] = acc + b_ref[0]


def _modulation(cond, w_mod, b_mod):
    depth, d, six_d = w_mod.shape
    nblk = six_d // d
    return pl.pallas_call(
        _modulation_kernel,
        grid=(depth, nblk),
        in_specs=[
            pl.BlockSpec((SUBLANES, d), lambda l, j: (0, 0)),
            pl.BlockSpec((1, d, d), lambda l, j: (l, 0, j)),
            pl.BlockSpec((1, 1, d), lambda l, j: (l, 0, j)),
        ],
        out_specs=pl.BlockSpec((1, SUBLANES, d), lambda l, j: (l, 0, j)),
        out_shape=jax.ShapeDtypeStruct((depth, SUBLANES, six_d), F32),
        compiler_params=_params(2),
        name="modulation",
    )(cond, w_mod, b_mod.reshape(depth, 1, six_d))


def _inproj_kernel(x_ref, mod_ref, gain_ref, w_ref, cos_ref, sin_ref, qkv_ref, rest_ref, *, qk_width):
    _inproj_tile(x_ref[...], mod_ref, gain_ref, w_ref, cos_ref, sin_ref, qkv_ref, rest_ref, qk_width)


def _inproj_tile(x, mod_ref, gain_ref, w_ref, cos_ref, sin_ref, qkv_ref, rest_ref, qk_width):
    q_scale = RET_DK ** -0.5
    h = _modulated_norm(x, gain_ref[...], mod_ref[0, 0:1, :], mod_ref[0, 1:2, :]).astype(BF16)
    reps = qk_width // RET_DK
    cos = jnp.concatenate([cos_ref[...]] * reps, axis=1)
    sin = jnp.concatenate([sin_ref[...]] * reps, axis=1)
    lane = lax.broadcasted_iota(jnp.int32, cos.shape, 1)
    first_half = (lane % (RET_DK // 2)) < (RET_DK // 4)
    for part, scale in ((0, q_scale), (1, 1.0)):
        c0 = part * qk_width
        p = jnp.dot(h, w_ref[:, c0:c0 + qk_width], preferred_element_type=F32)
        partner = jnp.where(first_half, pltpu.roll(p, qk_width - RET_DK // 4, 1), pltpu.roll(p, RET_DK // 4, 1))
        r = p * cos + partner * sin
        if scale != 1.0:
            r = r * scale
        qkv_ref[:, c0:c0 + qk_width] = r.astype(BF16)
    qkv_width = qkv_ref.shape[1]
    step = 512
    for c0 in range(2 * qk_width, qkv_width, step):
        qkv_ref[:, c0:c0 + step] = jnp.dot(h, w_ref[:, c0:c0 + step], preferred_element_type=F32).astype(BF16)
    for c0 in range(0, rest_ref.shape[1], step):
        rest_ref[:, c0:c0 + step] = jnp.dot(
            h, w_ref[:, qkv_width + c0:qkv_width + c0 + step], preferred_element_type=F32).astype(BF16)


def _inproj(xs, mods, gain, w_bf16, cos_t, sin_t, *, tm, tpb, n_batch, qk_width, qkv_width):
    t, d = xs.shape
    in_width = w_bf16.shape[1]
    rest_width = in_width - qkv_width
    pos_blocks = cos_t.shape[0] // tm - 1

    def seg(i):
        return jnp.where(i % tpb == 0, n_batch, i // tpb)

    def pos(i):
        j = i % tpb
        return jnp.where(j == 0, pos_blocks, j - 1)

    return pl.pallas_call(
        functools.partial(_inproj_kernel, qk_width=qk_width),
        grid=(t // tm,),
        in_specs=[
            pl.BlockSpec((tm, d), lambda i: (i, 0)),
            pl.BlockSpec((1, SUBLANES, d), lambda i: (seg(i), 0, 0)),
            pl.BlockSpec((1, d), lambda i: (0, 0)),
            pl.BlockSpec((d, in_width), lambda i: (0, 0)),
            pl.BlockSpec((tm, RET_DK), lambda i: (pos(i), 0)),
            pl.BlockSpec((tm, RET_DK), lambda i: (pos(i), 0)),
        ],
        out_specs=[
            pl.BlockSpec((tm, qkv_width), lambda i: (i, 0)),
            pl.BlockSpec((tm, rest_width), lambda i: (i, 0)),
        ],
        out_shape=[
            jax.ShapeDtypeStruct((t, qkv_width), BF16),
            jax.ShapeDtypeStruct((t, rest_width), BF16),
        ],
        compiler_params=_params(1),
        name="inproj",
    )(xs, mods, gain, w_bf16, cos_t, sin_t)


def _retention_kernel(gc_ref, q_ref, k_ref, v_ref, dec_ref, dmat_ref, y_ref, sf_ref, sb_ref, st_ref, *, lc, n):
    head = pl.program_id(1)
    c = RET_CHUNK
    qd = dec_ref[0, :, 0:2 * RET_DK]
    kdf = dec_ref[0, :, 2 * RET_DK:3 * RET_DK]
    kdb = dec_ref[0, :, 3 * RET_DK:4 * RET_DK]
    gcf = gc_ref[head, 0]
    gcb = gc_ref[head, 1]

    def rows(chunk):
        return pl.ds(pl.multiple_of(chunk * c, c), c)

    def advance(s_ref, chunk, kd, gc):
        kdec = (k_ref[rows(chunk), :].astype(F32) * kd).astype(BF16)
        kv = lax.dot_general(kdec, v_ref[rows(chunk), :], TN_DIMS, preferred_element_type=F32)
        s_ref[...] = gc * s_ref[...] + kv

    def sweep(chunk0, nchunks):
        def states(i, carry):
            cf = chunk0 + i
            cb = chunk0 + nchunks - 1 - i
            st_ref[cf, 0:RET_DK, :] = sf_ref[...].astype(BF16)
            st_ref[cb, RET_DK:2 * RET_DK, :] = sb_ref[...].astype(BF16)
            advance(sf_ref, cf, kdf, gcf)
            advance(sb_ref, cb, kdb, gcb)
            return carry

        def outputs(i, carry):
            r = rows(chunk0 + i)
            q, k, v = q_ref[r, :], k_ref[r, :], v_ref[r, :]
            scores = lax.dot_general(q, k, NT_DIMS, preferred_element_type=F32)
            qdec = (jnp.concatenate([q, q], axis=1).astype(F32) * qd).astype(BF16)
            y = jnp.dot((scores * dmat_ref[0]).astype(BF16), v, preferred_element_type=F32)
            y += jnp.dot(qdec, st_ref[chunk0 + i], preferred_element_type=F32)
            y_ref[r, :] = (y * lax.rsqrt(jnp.mean(y * y, axis=-1, keepdims=True) + EPS)).astype(BF16)
            return carry

        unroll = 2 if nchunks % 2 == 0 else 1
        lax.fori_loop(0, nchunks, states, 0, unroll=unroll)
        lax.fori_loop(0, nchunks, outputs, 0, unroll=unroll)

    sf_ref[...] = jnp.zeros_like(sf_ref)
    sb_ref[...] = jnp.zeros_like(sb_ref)
    sweep(0, lc // c)
    sweep(lc // c, n // c)


def _retention(qkv, dec, dmat, gc, *, n_batch, lc, n, qk_width):
    s = lc + n
    t = qkv.shape[0]
    k_blk0 = qk_width // RET_DK
    v_blk0 = 2 * qk_width // RET_DV
    return pl.pallas_call(
        functools.partial(_retention_kernel, lc=lc, n=n),
        grid_spec=pltpu.PrefetchScalarGridSpec(
            num_scalar_prefetch=1,
            grid=(n_batch, RET_HEADS),
            in_specs=[
                pl.BlockSpec((s, RET_DK), lambda b, h, gc: (b, h)),
                pl.BlockSpec((s, RET_DK), lambda b, h, gc: (b, k_blk0 + h)),
                pl.BlockSpec((s, RET_DV), lambda b, h, gc: (b, v_blk0 + h)),
                pl.BlockSpec((1, RET_CHUNK, 4 * RET_DK), lambda b, h, gc: (h, 0, 0)),
                pl.BlockSpec((1, RET_CHUNK, RET_CHUNK), lambda b, h, gc: (h, 0, 0)),
            ],
            out_specs=pl.BlockSpec((s, RET_DV), lambda b, h, gc: (b, h)),
            scratch_shapes=[
                pltpu.VMEM((RET_DK, RET_DV), F32),
                pltpu.VMEM((RET_DK, RET_DV), F32),
                pltpu.VMEM((s // RET_CHUNK, 2 * RET_DK, RET_DV), BF16),
            ],
        ),
        out_shape=jax.ShapeDtypeStruct((t, RET_HEADS * RET_DV), BF16),
        compiler_params=_params(2),
        name="retention",
    )(gc, qkv, qkv, qkv, dec, dmat)


def _decay_tables(logit_f, logit_b):
    c = RET_CHUNK
    lg_f = jax.nn.log_sigmoid(logit_f.astype(F32))[:, None]
    lg_b = jax.nn.log_sigmoid(logit_b.astype(F32))[:, None]
    i = jnp.arange(c, dtype=F32)[None, :]
    vecs = jnp.stack([
        jnp.exp((i + 1) * lg_f), jnp.exp((c - i) * lg_b),
        jnp.exp((c - 1 - i) * lg_f), jnp.exp(i * lg_b)], axis=2)
    dec = jnp.broadcast_to(vecs[..., None], vecs.shape + (RET_DK,)).reshape(vecs.shape[0], c, 4 * RET_DK)
    diff = i[0][:, None] - i[0][None, :]
    lower = jnp.where(diff >= 0, jnp.exp(jnp.where(diff >= 0, diff, 0.0)[None] * lg_f[:, :, None]), 0.0)
    upper = jnp.where(diff < 0, jnp.exp(jnp.where(diff < 0, -diff, 0.0)[None] * lg_b[:, :, None]), 0.0)
    gc = jnp.concatenate([jnp.exp(c * lg_f), jnp.exp(c * lg_b)], axis=1)
    return dec, lower + upper, gc


def _mix_kernel(x_ref, y_ref, rest_ref, mod_ref, cw_ref, wo_ref, gain2_ref, rw_ref, rb_ref,
                o_ref, h2_ref, ei_ref, wt_ref, cnt_ref, h_prev, *, tpb, ntiles, ret_width, conv_width):
    step = pl.program_id(0)

    @pl.when(step == 0)
    def _():
        h_prev[...] = jnp.zeros_like(h_prev)

    _route(h_prev[...], rw_ref, rb_ref, ei_ref, wt_ref, cnt_ref)
    is_ctx = (jnp.minimum(step, ntiles - 1) % tpb) == 0
    tm = x_ref.shape[0]
    g = rest_ref[:, 0:ret_width].astype(F32)
    cb = rest_ref[:, ret_width:ret_width + conv_width].astype(F32)
    cc = rest_ref[:, ret_width + conv_width:ret_width + 2 * conv_width].astype(F32)
    cx = rest_ref[:, ret_width + 2 * conv_width:ret_width + 3 * conv_width].astype(F32)
    ret = (g * _sigmoid(g) * y_ref[...].astype(F32)).astype(BF16)
    u = cc * cx
    row = lax.broadcasted_iota(jnp.int32, u.shape, 0)
    pos = jnp.where(is_ctx, row, row % GRID_W)
    last = jnp.where(is_ctx, tm - 1, GRID_W - 1)
    u_prev = jnp.where(pos == 0, 0.0, pltpu.roll(u, 1, 0))
    u_next = jnp.where(pos == last, 0.0, pltpu.roll(u, tm - 1, 0))
    conv = (cb * (u_prev * cw_ref[0:1, :] + u * cw_ref[1:2, :] + u_next * cw_ref[2:3, :])).astype(BF16)
    mix = jnp.dot(ret, wo_ref[0:ret_width, :], preferred_element_type=F32)
    mix += jnp.dot(conv, wo_ref[ret_width:ret_width + conv_width, :], preferred_element_type=F32)
    x_new = x_ref[...] + mod_ref[0, 2:3, :] * mix
    o_ref[...] = x_new
    h2 = _modulated_norm(x_new, gain2_ref[...], mod_ref[0, 3:4, :], mod_ref[0, 4:5, :])
    h2_ref[...] = h2.astype(BF16)
    h_prev[...] = h2


def _mix(xs, y, rest, mods, conv_w8, wo_bf16, gain2, rw_t, rb_col, *, tm, tpb, n_batch):
    t, d = xs.shape
    ntiles = t // tm
    ret_width = y.shape[1]
    conv_width = conv_w8.shape[1]
    n_exp = rw_t.shape[0]

    def cur(s):
        return jnp.minimum(s, ntiles - 1)

    def prev(s):
        return jnp.maximum(s - 1, 0)

    def seg(i):
        return jnp.where(i % tpb == 0, n_batch, i // tpb)

    return pl.pallas_call(
        functools.partial(_mix_kernel, tpb=tpb, ntiles=ntiles, ret_width=ret_width, conv_width=conv_width),
        grid=(ntiles + 1,),
        in_specs=[
            pl.BlockSpec((tm, d), lambda s: (cur(s), 0)),
            pl.BlockSpec((tm, ret_width), lambda s: (cur(s), 0)),
            pl.BlockSpec((tm, rest.shape[1]), lambda s: (cur(s), 0)),
            pl.BlockSpec((1, SUBLANES, d), lambda s: (seg(cur(s)), 0, 0)),
            pl.BlockSpec((SUBLANES, conv_width), lambda s: (0, 0)),
            pl.BlockSpec(wo_bf16.shape, lambda s: (0, 0)),
            pl.BlockSpec((1, d), lambda s: (0, 0)),
            pl.BlockSpec((n_exp, d), lambda s: (0, 0)),
            pl.BlockSpec((n_exp, LANES), lambda s: (0, 0)),
        ],
        out_specs=[
            pl.BlockSpec((tm, d), lambda s: (cur(s), 0)),
            pl.BlockSpec((tm, d), lambda s: (cur(s), 0)),
            pl.BlockSpec((1, SUBLANES, tm), lambda s: (prev(s), 0, 0)),
            pl.BlockSpec((1, SUBLANES, tm), lambda s: (prev(s), 0, 0)),
            pl.BlockSpec((1, n_exp, LANES), lambda s: (prev(s), 0, 0)),
        ],
        out_shape=[
            jax.ShapeDtypeStruct((t, d), F32),
            jax.ShapeDtypeStruct((t, d), BF16),
            jax.ShapeDtypeStruct((ntiles, SUBLANES, tm), jnp.int32),
            jax.ShapeDtypeStruct((ntiles, SUBLANES, tm), F32),
            jax.ShapeDtypeStruct((ntiles, n_exp, LANES), jnp.int32),
        ],
        scratch_shapes=[pltpu.VMEM((tm, d), F32)],
        compiler_params=_params(1),
        name="mix",
    )(xs, y, rest, mods, conv_w8, wo_bf16, gain2, rw_t, rb_col)


def _route(h2, rw_ref, rb_ref, ei_ref, wt_ref, cnt_ref):
    h_hi, h_lo = _split_bf16(h2)
    w_hi, w_lo = _split_bf16(rw_ref[...])
    logits = lax.dot_general(w_hi, h_hi, NT_DIMS, preferred_element_type=F32)
    logits += lax.dot_general(w_hi, h_lo, NT_DIMS, preferred_element_type=F32)
    logits += lax.dot_general(w_lo, h_hi, NT_DIMS, preferred_element_type=F32)
    logits += rb_ref[:, 0:1]
    n_exp, tt = logits.shape
    expert = lax.broadcasted_iota(jnp.int32, (n_exp, tt), 0).astype(F32)
    slot = lax.broadcasted_iota(jnp.int32, (SUBLANES, tt), 0)
    ei = jnp.zeros((SUBLANES, tt), F32)
    ev = jnp.zeros((SUBLANES, tt), F32)
    taken = jnp.zeros((n_exp, tt), F32)
    top = None
    denom = jnp.zeros((1, tt), F32)
    for k in range(TOP_K):
        m = jnp.max(logits, axis=0, keepdims=True)
        idx = jnp.min(jnp.where(logits == m, expert, float(n_exp)), axis=0, keepdims=True)
        hit = expert == idx
        taken += hit.astype(F32)
        logits = jnp.where(hit, -jnp.inf, logits)
        if k == 0:
            top = m
        e = jnp.exp(m - top)
        denom += e
        ei = jnp.where(slot == k, idx, ei)
        ev = jnp.where(slot == k, e, ev)
    ei_ref[0] = ei.astype(jnp.int32)
    wt_ref[0] = jnp.where(slot < TOP_K, ev / denom, 0.0)
    cnt = jnp.sum(taken, axis=1, keepdims=True)
    cnt_ref[0] = jnp.broadcast_to(cnt, (n_exp, LANES)).astype(jnp.int32)


def _routing_plan(cnt, *, tm, sorted_rows):
    ntiles, n_exp = cnt.shape
    seg_len = (cnt + ROW_ALIGN - 1) // ROW_ALIGN * ROW_ALIGN
    off = jnp.cumsum(seg_len, axis=1) - seg_len
    tile_rows = jnp.sum(seg_len, axis=1)
    before = jnp.cumsum(seg_len, axis=0) - seg_len
    filled = jnp.sum(seg_len, axis=0)
    region = (filled + EXPERT_TILE - 1) // EXPERT_TILE * EXPERT_TILE
    region_end = jnp.cumsum(region)
    region_start = region_end - region
    base = region_start[None, :] + before

    nb_max = _max_blocks(ntiles, sorted_rows, n_exp)
    dump0 = nb_max * EXPERT_TILE
    nch = sorted_rows // ROW_ALIGN
    q = jnp.arange(nch, dtype=jnp.int32) * ROW_ALIGN
    seg_end = off + seg_len
    inside = jnp.logical_and(off[:, None, :] <= q[None, :, None], q[None, :, None] < seg_end[:, None, :])
    valid = q[None, :] < tile_rows[:, None]
    row = q[None, :] + jnp.sum(jnp.where(inside, (base - off)[:, None, :], 0), axis=2)
    parity = (jnp.arange(ntiles, dtype=jnp.int32) % 2)[:, None]
    chunk_dst = jnp.where(valid, row, dump0 + parity * sorted_rows + q[None, :]).astype(jnp.int32)
    chunk_src = jnp.where(valid, row, 0).astype(jnp.int32)

    sizes = jnp.asarray(FILL_SIZES, dtype=jnp.int32)[None, :]
    gap = (region - filled)[:, None]
    used = (gap & sizes) != 0
    fill_row = filled[:, None] + (gap & ~(2 * sizes - 1))
    fill_dump = dump0 + 2 * sorted_rows + jnp.arange(n_exp, dtype=jnp.int32)[:, None] * EXPERT_TILE + (
        EXPERT_TILE - 2 * sizes)
    fill_dst = jnp.where(used, region_start[:, None] + fill_row, fill_dump).reshape(-1).astype(jnp.int32)

    blk = jnp.arange(nb_max, dtype=jnp.int32) * EXPERT_TILE
    block_e = jnp.sum((region_end[None, :] <= blk[:, None]).astype(jnp.int32), axis=1)
    block_e = jnp.minimum(block_e, n_exp - 1).astype(jnp.int32)
    n_used = (region_end[-1] // EXPERT_TILE).astype(jnp.int32).reshape(1)
    used = region > 0
    eidx = jnp.arange(n_exp, dtype=jnp.int32)
    rank = (jnp.cumsum(used) - used).astype(jnp.int32)
    later = jnp.where(jnp.logical_and(used[None, :], eidx[None, :] > eidx[:, None]), eidx[None, :], n_exp)
    nxt = jnp.min(later, axis=1)
    nxt = jnp.where(nxt == n_exp, -1, nxt)
    mine = block_e[:, None] == eidx[None, :]
    sched = jnp.stack([block_e, jnp.sum(jnp.where(mine, (rank % 2)[None, :], 0), axis=1),
                       jnp.sum(jnp.where(mine, nxt[None, :], 0), axis=1)]).astype(jnp.int32)
    return off, chunk_dst, chunk_src, fill_dst, sched, n_used


def _max_blocks(ntiles, sorted_rows, n_exp):
    return -(-(ntiles * sorted_rows + n_exp * EXPERT_TILE) // EXPERT_TILE)


def _sorted_positions(ei, off_col, tt):
    n_exp = off_col.shape[0]
    expert = lax.broadcasted_iota(jnp.int32, (n_exp, tt), 0)
    hits = [expert == ei[k:k + 1, :] for k in range(TOP_K)]
    chosen = sum(h.astype(F32) for h in hits).astype(BF16)
    t_row = lax.broadcasted_iota(jnp.int32, (tt, tt), 0)
    t_col = lax.broadcasted_iota(jnp.int32, (tt, tt), 1)
    earlier = (t_row < t_col).astype(BF16)
    rank = jnp.dot(chosen, earlier, preferred_element_type=F32) + off_col
    return [jnp.sum(jnp.where(h, rank, 0.0), axis=0, keepdims=True) for h in hits]


def _dispatch_kernel(cd_ref, fd_ref, h2_ref, ei_ref, off_ref, xb_ref, pos_ref, buf, zeros, sem, fill_sem):
    step = pl.program_id(0)
    tt = h2_ref.shape[0]
    sorted_rows = buf.shape[1]
    nch = sorted_rows // ROW_ALIGN

    def fill_copy(e, s):
        size = FILL_SIZES[s]
        dst = pl.multiple_of(fd_ref[e * len(FILL_SIZES) + s], ROW_ALIGN)
        return pltpu.make_async_copy(zeros.at[pl.ds(0, size), :], xb_ref.at[pl.ds(dst, size), :], fill_sem)

    @pl.when(step == 0)
    def _():
        zeros[...] = jnp.zeros_like(zeros)

        def start(e, c):
            for s in range(len(FILL_SIZES)):
                fill_copy(e, s).start()
            return c

        def wait(e, c):
            for s in range(len(FILL_SIZES)):
                fill_copy(e, s).wait()
            return c

        n_regions = fd_ref.shape[0] // len(FILL_SIZES)
        lax.fori_loop(0, n_regions, start, 0)
        lax.fori_loop(0, n_regions, wait, 0)

    cur = step % 2
    last_step = pl.num_programs(0) - 1

    def chunk_copy(slot, j):
        dst = pl.multiple_of(cd_ref[0, 0, j], ROW_ALIGN)
        return pltpu.make_async_copy(buf.at[slot, pl.ds(j * ROW_ALIGN, ROW_ALIGN), :],
                                     xb_ref.at[pl.ds(dst, ROW_ALIGN), :], sem.at[slot])

    def drain(slot):
        for j in range(nch):
            chunk_copy(slot, j).wait()

    @pl.when(step >= 2)
    def _():
        drain(cur)

    pos = _sorted_positions(ei_ref[0], off_ref[0][:, 0:1], tt)
    slot_id = lax.broadcasted_iota(jnp.int32, (SUBLANES, tt), 0)
    pos8 = jnp.zeros((SUBLANES, tt), F32)
    for k in range(TOP_K):
        pos8 = jnp.where(slot_id == k, pos[k], pos8)
    pos_ref[0] = pos8
    r = lax.broadcasted_iota(jnp.int32, (sorted_rows, tt), 0)
    hit = r == pos[0].astype(jnp.int32)
    for k in range(1, TOP_K):
        hit = jnp.logical_or(hit, r == pos[k].astype(jnp.int32))
    perm = jnp.where(hit, 1.0, 0.0).astype(BF16)
    buf[cur] = _pack_bf16_pairs(jnp.dot(perm, h2_ref[...], preferred_element_type=F32))
    for j in range(nch):
        chunk_copy(cur, j).start()

    @pl.when(step == last_step)
    def _():
        drain(cur)

    @pl.when(jnp.logical_and(step == last_step, step >= 1))
    def _():
        drain(1 - cur)


def _dispatch(h2, ei, off_col, chunk_dst, fill_dst, *, tm, sorted_rows, total_rows):
    t, d = h2.shape
    ntiles = t // tm
    n_exp = off_col.shape[1]
    nch = sorted_rows // ROW_ALIGN

    return pl.pallas_call(
        _dispatch_kernel,
        grid=(ntiles,),
        in_specs=[
            pl.BlockSpec((1, 1, nch), lambda i: (i, 0, 0), memory_space=pltpu.SMEM),
            pl.BlockSpec(memory_space=pltpu.SMEM),
            pl.BlockSpec((tm, d), lambda i: (i, 0)),
            pl.BlockSpec((1, SUBLANES, tm), lambda i: (i, 0, 0)),
            pl.BlockSpec((1, n_exp, LANES), lambda i: (i, 0, 0)),
        ],
        out_specs=[
            pl.BlockSpec(memory_space=pl.ANY),
            pl.BlockSpec((1, SUBLANES, tm), lambda i: (i, 0, 0)),
        ],
        out_shape=[
            jax.ShapeDtypeStruct((total_rows, d // 2), jnp.uint32),
            jax.ShapeDtypeStruct((ntiles, SUBLANES, tm), F32),
        ],
        scratch_shapes=[
            pltpu.VMEM((2, sorted_rows, d // 2), jnp.uint32),
            pltpu.VMEM((FILL_SIZES[0], d // 2), jnp.uint32),
            pltpu.SemaphoreType.DMA((2,)),
            pltpu.SemaphoreType.DMA(()),
        ],
        compiler_params=_params(1),
        name="dispatch",
    )(chunk_dst.reshape(ntiles, 1, nch), fill_dst, h2, ei, off_col)


def _experts_kernel(sched_ref, nu_ref, xb_ref, wu_hbm, bu_ref, wd_hbm, bd_ref, yb_ref,
                    wu_f, wd_f, wu_s, wd_s, sem, *, d_ff, layer):
    i = pl.program_id(0)
    expert, slot, nxt = sched_ref[0, i], sched_ref[1, i], sched_ref[2, i]
    fresh = jnp.logical_or(i == 0, expert != sched_ref[0, jnp.maximum(i - 1, 0)])

    def fetch(e, s):
        return (pltpu.make_async_copy(wu_hbm.at[layer, e], wu_f.at[s], sem.at[0, s]),
                pltpu.make_async_copy(wd_hbm.at[layer, e], wd_f.at[s], sem.at[1, s]))

    @pl.when(i == 0)
    def _():
        for c in fetch(expert, slot):
            c.start()

    @pl.when(jnp.logical_and(fresh, i < nu_ref[0]))
    def _():
        for c in fetch(expert, slot):
            c.wait()

        @pl.when(nxt >= 0)
        def _():
            for c in fetch(nxt, 1 - slot):
                c.start()

        wu_s[...] = wu_f[slot].astype(BF16)
        wd_s[...] = wd_f[slot].astype(BF16)

    @pl.when(i < nu_ref[0])
    def _():
        u = jnp.dot(_unpack_bf16_pairs(xb_ref[...]), wu_s[...], preferred_element_type=F32) + bu_ref[0, 0]
        gate = jnp.minimum(u[:, :d_ff], SWIGLU_LIMIT)
        lin = jnp.clip(u[:, d_ff:], -SWIGLU_LIMIT, SWIGLU_LIMIT)
        act = (gate * _sigmoid(SWIGLU_ALPHA * gate) * (lin + 1.0)).astype(BF16)
        y = jnp.dot(act, wd_s[...], preferred_element_type=F32) + bd_ref[0, 0]
        yb_ref[...] = _pack_bf16_pairs(y.astype(BF16).astype(F32))


def _experts(xb, sched, n_used, w_up, b_up, w_down, b_down, *, layer, nb_max):
    depth, n_exp, d, two_ff = w_up.shape
    d_ff = two_ff // 2
    dp = xb.shape[1]

    def blk(i, sc, nu):
        return jnp.minimum(i, nu[0] - 1)

    def exp(i, sc, nu):
        return sc[0, jnp.minimum(i, nu[0] - 1)]

    return pl.pallas_call(
        functools.partial(_experts_kernel, d_ff=d_ff, layer=layer),
        grid_spec=pltpu.PrefetchScalarGridSpec(
            num_scalar_prefetch=2,
            grid=(nb_max,),
            in_specs=[
                pl.BlockSpec((EXPERT_TILE, dp), lambda i, sc, nu: (blk(i, sc, nu), 0)),
                pl.BlockSpec(memory_space=pl.ANY),
                pl.BlockSpec((1, 1, 1, two_ff), lambda i, sc, nu: (layer, exp(i, sc, nu), 0, 0)),
                pl.BlockSpec(memory_space=pl.ANY),
                pl.BlockSpec((1, 1, 1, d), lambda i, sc, nu: (layer, exp(i, sc, nu), 0, 0)),
            ],
            out_specs=pl.BlockSpec((EXPERT_TILE, dp), lambda i, sc, nu: (blk(i, sc, nu), 0)),
            scratch_shapes=[
                pltpu.VMEM((2, d, two_ff), F32),
                pltpu.VMEM((2, d_ff, d), F32),
                pltpu.VMEM((d, two_ff), BF16),
                pltpu.VMEM((d_ff, d), BF16),
                pltpu.SemaphoreType.DMA((2, 2)),
            ],
        ),
        out_shape=jax.ShapeDtypeStruct((nb_max * EXPERT_TILE, dp), jnp.uint32),
        compiler_params=_params(1),
        name="experts",
    )(sched, n_used, xb, w_up, b_up.reshape(depth, n_exp, 1, two_ff), w_down, b_down.reshape(depth, n_exp, 1, d))


def _combine_kernel(cs_ref, csn_ref, x_ref, mod_ref, pos_ref, wt_ref, yb_ref, *rest, qk_width, follow):
    if follow == "inproj":
        mod2_ref, gain_ref, w_ref, cos_ref, sin_ref, o_ref, qkv_ref, rest_ref, buf, sem, x_prev = rest

        @pl.when(pl.program_id(0) == 0)
        def _():
            x_prev[...] = jnp.zeros_like(x_prev)

        _combine_fetch(cs_ref, csn_ref, yb_ref, buf, sem)()
        _inproj_tile(x_prev[...], mod2_ref, gain_ref, w_ref, cos_ref, sin_ref, qkv_ref, rest_ref, qk_width)
        x_new = _combine_tile(x_ref, mod_ref, pos_ref, wt_ref, buf)
        o_ref[...] = x_new
        x_prev[...] = x_new
    else:
        gain_ref, o_ref, buf, sem = rest
        wait = _combine_fetch(cs_ref, csn_ref, yb_ref, buf, sem)
        x_new = _combine_tile(x_ref, mod_ref, pos_ref, wt_ref, buf, wait)
        ms = jnp.mean(x_new * x_new, axis=-1, keepdims=True)
        o_ref[0] = x_new * lax.rsqrt(ms + EPS) * gain_ref[...]


def _combine_fetch(cs_ref, csn_ref, yb_ref, buf, sem):
    step = pl.program_id(0)
    cur = step % 2
    nch = buf.shape[1] // ROW_ALIGN

    def chunk_copy(table, slot, j):
        src = pl.multiple_of(table[0, 0, j], ROW_ALIGN)
        return pltpu.make_async_copy(yb_ref.at[pl.ds(src, ROW_ALIGN), :],
                                     buf.at[slot, pl.ds(j * ROW_ALIGN, ROW_ALIGN), :], sem.at[slot])

    @pl.when(step == 0)
    def _():
        for j in range(nch):
            chunk_copy(cs_ref, cur, j).start()

    @pl.when(step + 1 < pl.num_programs(0))
    def _():
        for j in range(nch):
            chunk_copy(csn_ref, 1 - cur, j).start()

    def wait():
        for j in range(nch):
            chunk_copy(cs_ref, cur, j).wait()

    return wait


def _combine_tile(x_ref, mod_ref, pos_ref, wt_ref, buf, wait=None):
    tt = x_ref.shape[0]
    sorted_rows = buf.shape[1]
    r = lax.broadcasted_iota(jnp.int32, (sorted_rows, tt), 0)
    pos = pos_ref[0].astype(jnp.int32)
    wt = wt_ref[0]
    sel = jnp.zeros((sorted_rows, tt), F32)
    for k in reversed(range(TOP_K)):
        sel = jnp.where(r == pos[k:k + 1, :], wt[k:k + 1, :], sel)
    if wait is not None:
        wait()
    y2 = lax.dot_general(sel.astype(BF16), _unpack_bf16_pairs(buf[pl.program_id(0) % 2]), TN_DIMS,
                         preferred_element_type=F32)
    return x_ref[...] + mod_ref[0, 5:6, :] * y2


def _combine(xs, mods, pos, wt, chunk_src, yb, *, tm, tpb, n_batch, sorted_rows, inproj=None, final=None):
    t, d = xs.shape
    ntiles = t // tm
    nch = sorted_rows // ROW_ALIGN

    def seg(i):
        return jnp.where(i % tpb == 0, n_batch, i // tpb)

    def cur(s):
        return jnp.minimum(s, ntiles - 1)

    def prev(s):
        return jnp.maximum(s - 1, 0)

    in_specs = [
        pl.BlockSpec((1, 1, nch), lambda s: (cur(s), 0, 0), memory_space=pltpu.SMEM),
        pl.BlockSpec((1, 1, nch), lambda s: (cur(s + 1), 0, 0), memory_space=pltpu.SMEM),
        pl.BlockSpec((tm, d), lambda s: (cur(s), 0)),
        pl.BlockSpec((1, SUBLANES, d), lambda s: (seg(cur(s)), 0, 0)),
        pl.BlockSpec((1, SUBLANES, tm), lambda s: (cur(s), 0, 0)),
        pl.BlockSpec((1, SUBLANES, tm), lambda s: (cur(s), 0, 0)),
        pl.BlockSpec(memory_space=pl.ANY),
    ]
    args = [chunk_src.reshape(ntiles, 1, nch), chunk_src.reshape(ntiles, 1, nch), xs, mods, pos, wt, yb]
    scratch = [pltpu.VMEM((2, sorted_rows, yb.shape[1]), jnp.uint32), pltpu.SemaphoreType.DMA((2,))]
    if inproj is not None:
        mods2, gain, w_bf16, cos_t, sin_t, qk_width, qkv_width = inproj
        in_width = w_bf16.shape[1]
        pos_blocks = cos_t.shape[0] // tm - 1

        def rope_blk(i):
            j = i % tpb
            return jnp.where(j == 0, pos_blocks, j - 1)

        in_specs += [
            pl.BlockSpec((1, SUBLANES, d), lambda s: (seg(prev(s)), 0, 0)),
            pl.BlockSpec((1, d), lambda s: (0, 0)),
            pl.BlockSpec((d, in_width), lambda s: (0, 0)),
            pl.BlockSpec((tm, RET_DK), lambda s: (rope_blk(prev(s)), 0)),
            pl.BlockSpec((tm, RET_DK), lambda s: (rope_blk(prev(s)), 0)),
        ]
        args += [mods2, gain, w_bf16, cos_t, sin_t]
        out_specs = [pl.BlockSpec((tm, d), lambda s: (cur(s), 0)),
                     pl.BlockSpec((tm, qkv_width), lambda s: (prev(s), 0)),
                     pl.BlockSpec((tm, in_width - qkv_width), lambda s: (prev(s), 0))]
        out_shape = [jax.ShapeDtypeStruct((t, d), F32), jax.ShapeDtypeStruct((t, qkv_width), BF16),
                     jax.ShapeDtypeStruct((t, in_width - qkv_width), BF16)]
        scratch += [pltpu.VMEM((tm, d), F32)]
        steps, follow = ntiles + 1, "inproj"
    else:
        gain, n = final
        qk_width = None
        in_specs += [pl.BlockSpec((1, d), lambda s: (0, 0))]
        args += [gain]
        out_specs = pl.BlockSpec((1, tm, d), lambda s: (s // tpb, jnp.maximum(s % tpb - 1, 0), 0))
        out_shape = jax.ShapeDtypeStruct((n_batch, n, d), F32)
        steps, follow = ntiles, "final"

    return pl.pallas_call(
        functools.partial(_combine_kernel, qk_width=qk_width, follow=follow),
        grid=(steps,),
        in_specs=in_specs,
        out_specs=out_specs,
        out_shape=out_shape,
        scratch_shapes=scratch,
        compiler_params=_params(1),
        name="combine",
    )(*args)


def _rope_tables(n, tm):
    t = jnp.arange(n, dtype=jnp.int32)
    m = RET_DK // 4
    freqs = ROPE_BASE ** (-jnp.arange(m, dtype=F32) / m)
    ang_r = (t // GRID_W).astype(F32)[:, None] * freqs[None, :]
    ang_c = (t % GRID_W).astype(F32)[:, None] * freqs[None, :]
    cos = jnp.concatenate([jnp.cos(ang_r)] * 2 + [jnp.cos(ang_c)] * 2, axis=1)
    sin = jnp.concatenate([-jnp.sin(ang_r), jnp.sin(ang_r), -jnp.sin(ang_c), jnp.sin(ang_c)], axis=1)
    cos = jnp.concatenate([cos, jnp.ones((tm, RET_DK), F32)], axis=0)
    sin = jnp.concatenate([sin, jnp.zeros((tm, RET_DK), F32)], axis=0)
    return cos, sin


def kernel(x, c, ctx, c_ctx, w_mod, b_mod, norm_mix, norm_ffn, w_in, ret_decay_f, ret_decay_b, conv_w, w_out,
           router_w, router_b, w_up, b_up, w_down, b_down, norm_final):
    n_batch, n, d = x.shape
    lc = ctx.shape[1]
    depth = w_mod.shape[0]
    tm = lc
    assert tm % 256 == 0 and n % tm == 0 and n % GRID_W == 0 and tm % GRID_W == 0
    assert n % RET_CHUNK == 0 and lc % RET_CHUNK == 0 and n_batch + 1 <= SUBLANES
    tpb = (lc + n) // tm
    qk_width = RET_HEADS * RET_DK
    qkv_width = 2 * qk_width + RET_HEADS * RET_DV
    n_exp = router_w.shape[2]
    sorted_rows = TOP_K * tm + n_exp * ROW_ALIGN
    ntiles = n_batch * tpb
    nb_max = _max_blocks(ntiles, sorted_rows, n_exp)
    total_rows = nb_max * EXPERT_TILE + 2 * sorted_rows + n_exp * EXPERT_TILE

    xs = jnp.concatenate([ctx, x], axis=1).reshape(n_batch * (lc + n), d)
    cond = jnp.zeros((SUBLANES, d), F32).at[:n_batch].set(c).at[n_batch].set(c_ctx)
    mod_all = _modulation(cond, w_mod, b_mod).reshape(depth, SUBLANES, 6, d)
    mod_all = jnp.pad(mod_all, ((0, 0), (0, 0), (0, SUBLANES - 6), (0, 0)))
    cos_t, sin_t = _rope_tables(n, tm)

    tile_args = dict(tm=tm, tpb=tpb, n_batch=n_batch)
    qkv, rest = _inproj(xs, mod_all[0], norm_mix[0][None, :], w_in[0].astype(BF16), cos_t, sin_t,
                        qk_width=qk_width, qkv_width=qkv_width, **tile_args)
    for layer in range(depth):
        mods = mod_all[layer]
        dec, dmat, gc = _decay_tables(ret_decay_f[layer], ret_decay_b[layer])
        y = _retention(qkv, dec, dmat, gc, n_batch=n_batch, lc=lc, n=n, qk_width=qk_width)
        conv_w8 = jnp.pad(conv_w[layer], ((0, SUBLANES - conv_w.shape[1]), (0, 0)))
        gain2 = norm_ffn[layer][None, :]
        rb_col = jnp.broadcast_to(router_b[layer][:, None], (n_exp, LANES))
        xs, h2, ei, wt, cnt = _mix(xs, y, rest, mods, conv_w8, w_out[layer].astype(BF16), gain2, router_w[layer].T,
                                   rb_col, **tile_args)
        off, chunk_dst, chunk_src, fill_dst, sched, n_used = _routing_plan(
            cnt[:, :, 0], tm=tm, sorted_rows=sorted_rows)
        off_col = jnp.broadcast_to(off.astype(F32)[:, :, None], (ntiles, n_exp, LANES))
        xb, pos = _dispatch(h2, ei, off_col, chunk_dst, fill_dst, tm=tm, sorted_rows=sorted_rows,
                            total_rows=total_rows)
        yb = _experts(xb, sched, n_used, w_up, b_up, w_down, b_down, layer=layer, nb_max=nb_max)
        if layer + 1 == depth:
            return _combine(xs, mods, pos, wt, chunk_src, yb, sorted_rows=sorted_rows,
                            final=(norm_final[None, :], n), **tile_args)
        nxt = layer + 1
        xs, qkv, rest = _combine(
            xs, mods, pos, wt, chunk_src, yb, sorted_rows=sorted_rows,
            inproj=(mod_all[nxt], norm_mix[nxt][None, :], w_in[nxt].astype(BF16), cos_t, sin_t, qk_width, qkv_width),
            **tile_args)
```

```python
import functools

import jax
import jax.numpy as jnp
from jax import lax
from jax.experimental import pallas as pl
from jax.experimental.pallas import tpu as pltpu

GRID_W = 64
RET_HEADS = 4
RET_DK = 128
RET_DV = 256
RET_CHUNK = 256
N_EXPERTS = 32
TOP_K = 4
SWIGLU_LIMIT = 7.0
SWIGLU_ALPHA = 1.702
ROPE_BASE = 10000.0
EPS = 1e-6

SUBLANES = 8
LANES = 128
VMEM_LIMIT = 56 * 1024 * 1024

EXPERT_TILE = 512
ROW_ALIGN = SUBLANES
FILL_SIZES = tuple(EXPERT_TILE >> k for k in range(1, EXPERT_TILE.bit_length()) if EXPERT_TILE >> k >= ROW_ALIGN)

F32 = jnp.float32
BF16 = jnp.bfloat16
NT_DIMS = (((1,), (1,)), ((), ()))
TN_DIMS = (((0,), (0,)), ((), ()))


def _params(n_axes=1):
    return pltpu.CompilerParams(dimension_semantics=("arbitrary",) * n_axes, vmem_limit_bytes=VMEM_LIMIT)


def _modulated_norm(x, gain, shift, scale):
    ms = jnp.mean(x * x, axis=-1, keepdims=True)
    return x * lax.rsqrt(ms + EPS) * gain * (1.0 + scale) + shift


def _sigmoid(z):
    return 1.0 / (1.0 + jnp.exp(-z))


def _split_bf16(a):
    hi = a.astype(BF16)
    lo = (a - hi.astype(F32)).astype(BF16)
    return hi, lo


def _pack_bf16_pairs(a):
    w = a.shape[1] // 2
    lo = lax.bitcast_convert_type(a[:, :w], jnp.uint32)
    hi = lax.bitcast_convert_type(a[:, w:], jnp.uint32)
    return (lo >> 16) | (hi & jnp.uint32(0xFFFF0000))


def _unpack_bf16_pairs(p):
    lo = lax.bitcast_convert_type(p << 16, F32)
    hi = lax.bitcast_convert_type(p & jnp.uint32(0xFFFF0000), F32)
    return jnp.concatenate([lo, hi], axis=1).astype(BF16)


def _modulation_kernel(c_ref, w_ref, b_ref, o_ref):
    c = c_ref[...]
    s = c * _sigmoid(c)
    s_hi, s_lo = _split_bf16(s)
    w_hi, w_lo = _split_bf16(w_ref[0])
    acc = jnp.dot(s_hi, w_hi, preferred_element_type=F32)
    acc += jnp.dot(s_hi, w_lo, preferred_element_type=F32)
    acc += jnp.dot(s_lo, w_hi, preferred_element_type=F32)
    o_ref[0] = acc + b_ref[0]


def _modulation(cond, w_mod, b_mod):
    depth, d, six_d = w_mod.shape
    nblk = six_d // d
    return pl.pallas_call(
        _modulation_kernel,
        grid=(depth, nblk),
        in_specs=[
            pl.BlockSpec((SUBLANES, d), lambda l, j: (0, 0)),
            pl.BlockSpec((1, d, d), lambda l, j: (l, 0, j)),
            pl.BlockSpec((1, 1, d), lambda l, j: (l, 0, j)),
        ],
        out_specs=pl.BlockSpec((1, SUBLANES, d), lambda l, j: (l, 0, j)),
        out_shape=jax.ShapeDtypeStruct((depth, SUBLANES, six_d), F32),
        compiler_params=_params(2),
        name="modulation",
    )(cond, w_mod, b_mod.reshape(depth, 1, six_d))


def _inproj_kernel(ctx_ref, x_ref, mod_ref, gain_ref, w_ref, cos_ref, sin_ref, xs_ref, qkv_ref, rest_ref, *,
                   qk_width, tpb):
    is_ctx = (pl.program_id(0) % tpb) == 0
    x = jnp.where(is_ctx, ctx_ref[0], x_ref[0])
    xs_ref[...] = x
    _inproj_tile(x, mod_ref, gain_ref, w_ref, cos_ref, sin_ref, qkv_ref, rest_ref, qk_width)


def _inproj_tile(x, mod_ref, gain_ref, w_ref, cos_ref, sin_ref, qkv_ref, rest_ref, qk_width):
    q_scale = RET_DK ** -0.5
    h = _modulated_norm(x, gain_ref[...], mod_ref[0, 0:1, :], mod_ref[0, 1:2, :]).astype(BF16)
    reps = qk_width // RET_DK
    cos = jnp.concatenate([cos_ref[...]] * reps, axis=1)
    sin = jnp.concatenate([sin_ref[...]] * reps, axis=1)
    lane = lax.broadcasted_iota(jnp.int32, cos.shape, 1)
    first_half = (lane % (RET_DK // 2)) < (RET_DK // 4)
    for part, scale in ((0, q_scale), (1, 1.0)):
        c0 = part * qk_width
        p = jnp.dot(h, w_ref[:, c0:c0 + qk_width], preferred_element_type=F32)
        partner = jnp.where(first_half, pltpu.roll(p, qk_width - RET_DK // 4, 1), pltpu.roll(p, RET_DK // 4, 1))
        r = p * cos + partner * sin
        if scale != 1.0:
            r = r * scale
        qkv_ref[:, c0:c0 + qk_width] = r.astype(BF16)
    qkv_width = qkv_ref.shape[1]
    step = 512
    for c0 in range(2 * qk_width, qkv_width, step):
        qkv_ref[:, c0:c0 + step] = jnp.dot(h, w_ref[:, c0:c0 + step], preferred_element_type=F32).astype(BF16)
    for c0 in range(0, rest_ref.shape[1], step):
        rest_ref[:, c0:c0 + step] = jnp.dot(
            h, w_ref[:, qkv_width + c0:qkv_width + c0 + step], preferred_element_type=F32).astype(BF16)


def _inproj(ctx, x, mods, gain, w_bf16, cos_t, sin_t, *, tm, tpb, n_batch, qk_width, qkv_width):
    d = x.shape[2]
    t = n_batch * tpb * tm
    in_width = w_bf16.shape[1]
    rest_width = in_width - qkv_width
    pos_blocks = cos_t.shape[0] // tm - 1

    def seg(i):
        return jnp.where(i % tpb == 0, n_batch, i // tpb)

    def pos(i):
        j = i % tpb
        return jnp.where(j == 0, pos_blocks, j - 1)

    return pl.pallas_call(
        functools.partial(_inproj_kernel, qk_width=qk_width, tpb=tpb),
        grid=(t // tm,),
        in_specs=[
            pl.BlockSpec((1, tm, d), lambda i: (i // tpb, 0, 0)),
            pl.BlockSpec((1, tm, d), lambda i: (i // tpb, jnp.maximum(i % tpb - 1, 0), 0)),
            pl.BlockSpec((1, SUBLANES, d), lambda i: (seg(i), 0, 0)),
            pl.BlockSpec((1, d), lambda i: (0, 0)),
            pl.BlockSpec((d, in_width), lambda i: (0, 0)),
            pl.BlockSpec((tm, RET_DK), lambda i: (pos(i), 0)),
            pl.BlockSpec((tm, RET_DK), lambda i: (pos(i), 0)),
        ],
        out_specs=[
            pl.BlockSpec((tm, d), lambda i: (i, 0)),
            pl.BlockSpec((tm, qkv_width), lambda i: (i, 0)),
            pl.BlockSpec((tm, rest_width), lambda i: (i, 0)),
        ],
        out_shape=[
            jax.ShapeDtypeStruct((t, d), F32),
            jax.ShapeDtypeStruct((t, qkv_width), BF16),
            jax.ShapeDtypeStruct((t, rest_width), BF16),
        ],
        compiler_params=_params(1),
        name="inproj",
    )(ctx, x, mods, gain, w_bf16, cos_t, sin_t)


def _retention_kernel(gc_ref, q_ref, k_ref, v_ref, dec_ref, dmat_ref, y_ref, sf_ref, sb_ref, st_ref, *, lc, n):
    head = pl.program_id(1)
    c = RET_CHUNK
    qd = dec_ref[0, :, 0:2 * RET_DK]
    kdf = dec_ref[0, :, 2 * RET_DK:3 * RET_DK]
    kdb = dec_ref[0, :, 3 * RET_DK:4 * RET_DK]
    gcf = gc_ref[head, 0]
    gcb = gc_ref[head, 1]

    def rows(chunk):
        return pl.ds(pl.multiple_of(chunk * c, c), c)

    def advance(s_ref, chunk, kd, gc):
        kdec = (k_ref[rows(chunk), :].astype(F32) * kd).astype(BF16)
        kv = lax.dot_general(kdec, v_ref[rows(chunk), :], TN_DIMS, preferred_element_type=F32)
        s_ref[...] = gc * s_ref[...] + kv

    def sweep(chunk0, nchunks):
        def states(i, carry):
            cf = chunk0 + i
            cb = chunk0 + nchunks - 1 - i
            st_ref[cf, 0:RET_DK, :] = sf_ref[...].astype(BF16)
            st_ref[cb, RET_DK:2 * RET_DK, :] = sb_ref[...].astype(BF16)
            advance(sf_ref, cf, kdf, gcf)
            advance(sb_ref, cb, kdb, gcb)
            return carry

        def outputs(i, carry):
            r = rows(chunk0 + i)
            q, k, v = q_ref[r, :], k_ref[r, :], v_ref[r, :]
            scores = lax.dot_general(q, k, NT_DIMS, preferred_element_type=F32)
            qdec = (jnp.concatenate([q, q], axis=1).astype(F32) * qd).astype(BF16)
            y = jnp.dot((scores * dmat_ref[0]).astype(BF16), v, preferred_element_type=F32)
            y += jnp.dot(qdec, st_ref[chunk0 + i], preferred_element_type=F32)
            y_ref[r, :] = (y * lax.rsqrt(jnp.mean(y * y, axis=-1, keepdims=True) + EPS)).astype(BF16)
            return carry

        lax.fori_loop(0, nchunks, states, 0, unroll=4 if nchunks % 4 == 0 else 1)
        lax.fori_loop(0, nchunks, outputs, 0, unroll=8 if nchunks % 8 == 0 else 1)

    sf_ref[...] = jnp.zeros_like(sf_ref)
    sb_ref[...] = jnp.zeros_like(sb_ref)
    sweep(0, lc // c)
    sweep(lc // c, n // c)


def _retention(qkv, dec, dmat, gc, *, n_batch, lc, n, qk_width):
    s = lc + n
    t = qkv.shape[0]
    k_blk0 = qk_width // RET_DK
    v_blk0 = 2 * qk_width // RET_DV
    return pl.pallas_call(
        functools.partial(_retention_kernel, lc=lc, n=n),
        grid_spec=pltpu.PrefetchScalarGridSpec(
            num_scalar_prefetch=1,
            grid=(n_batch, RET_HEADS),
            in_specs=[
                pl.BlockSpec((s, RET_DK), lambda b, h, gc: (b, h)),
                pl.BlockSpec((s, RET_DK), lambda b, h, gc: (b, k_blk0 + h)),
                pl.BlockSpec((s, RET_DV), lambda b, h, gc: (b, v_blk0 + h)),
                pl.BlockSpec((1, RET_CHUNK, 4 * RET_DK), lambda b, h, gc: (h, 0, 0)),
                pl.BlockSpec((1, RET_CHUNK, RET_CHUNK), lambda b, h, gc: (h, 0, 0)),
            ],
            out_specs=pl.BlockSpec((s, RET_DV), lambda b, h, gc: (b, h)),
            scratch_shapes=[
                pltpu.VMEM((RET_DK, RET_DV), F32),
                pltpu.VMEM((RET_DK, RET_DV), F32),
                pltpu.VMEM((s // RET_CHUNK, 2 * RET_DK, RET_DV), BF16),
            ],
        ),
        out_shape=jax.ShapeDtypeStruct((t, RET_HEADS * RET_DV), BF16),
        compiler_params=_params(2),
        name="retention",
    )(gc, qkv, qkv, qkv, dec, dmat)


def _decay_tables(logit_f, logit_b):
    c = RET_CHUNK
    lg_f = jax.nn.log_sigmoid(logit_f.astype(F32))[:, None]
    lg_b = jax.nn.log_sigmoid(logit_b.astype(F32))[:, None]
    i = jnp.arange(c, dtype=F32)[None, :]
    vecs = jnp.stack([
        jnp.exp((i + 1) * lg_f), jnp.exp((c - i) * lg_b),
        jnp.exp((c - 1 - i) * lg_f), jnp.exp(i * lg_b)], axis=2)
    dec = jnp.broadcast_to(vecs[..., None], vecs.shape + (RET_DK,)).reshape(vecs.shape[0], c, 4 * RET_DK)
    diff = i[0][:, None] - i[0][None, :]
    lower = jnp.where(diff >= 0, jnp.exp(jnp.where(diff >= 0, diff, 0.0)[None] * lg_f[:, :, None]), 0.0)
    upper = jnp.where(diff < 0, jnp.exp(jnp.where(diff < 0, -diff, 0.0)[None] * lg_b[:, :, None]), 0.0)
    gc = jnp.concatenate([jnp.exp(c * lg_f), jnp.exp(c * lg_b)], axis=1)
    return dec, lower + upper, gc


def _mix_kernel(x_ref, y_ref, rest_ref, mod_ref, cw_ref, wo_ref, gain2_ref, rw_ref, rb_ref,
                o_ref, h2_ref, ei_ref, wt_ref, cnt_ref, h_prev, *, tpb, ntiles, ret_width, conv_width):
    step = pl.program_id(0)

    @pl.when(step == 0)
    def _():
        h_prev[...] = jnp.zeros_like(h_prev)

    _route(h_prev[...], rw_ref, rb_ref, ei_ref, wt_ref, cnt_ref)
    is_ctx = (jnp.minimum(step, ntiles - 1) % tpb) == 0
    tm = x_ref.shape[0]
    g = rest_ref[:, 0:ret_width].astype(F32)
    cb = rest_ref[:, ret_width:ret_width + conv_width].astype(F32)
    cc = rest_ref[:, ret_width + conv_width:ret_width + 2 * conv_width].astype(F32)
    cx = rest_ref[:, ret_width + 2 * conv_width:ret_width + 3 * conv_width].astype(F32)
    ret = (g * _sigmoid(g) * y_ref[...].astype(F32)).astype(BF16)
    u = cc * cx
    row = lax.broadcasted_iota(jnp.int32, u.shape, 0)
    pos = jnp.where(is_ctx, row, row % GRID_W)
    last = jnp.where(is_ctx, tm - 1, GRID_W - 1)
    u_prev = jnp.where(pos == 0, 0.0, pltpu.roll(u, 1, 0))
    u_next = jnp.where(pos == last, 0.0, pltpu.roll(u, tm - 1, 0))
    conv = (cb * (u_prev * cw_ref[0:1, :] + u * cw_ref[1:2, :] + u_next * cw_ref[2:3, :])).astype(BF16)
    mix = jnp.dot(ret, wo_ref[0:ret_width, :], preferred_element_type=F32)
    mix += jnp.dot(conv, wo_ref[ret_width:ret_width + conv_width, :], preferred_element_type=F32)
    x_new = x_ref[...] + mod_ref[0, 2:3, :] * mix
    o_ref[...] = x_new
    h2 = _modulated_norm(x_new, gain2_ref[...], mod_ref[0, 3:4, :], mod_ref[0, 4:5, :])
    h2_ref[...] = h2.astype(BF16)
    h_prev[...] = h2


def _mix(xs, y, rest, mods, conv_w8, wo_bf16, gain2, rw_t, rb_col, *, tm, tpb, n_batch):
    t, d = xs.shape
    ntiles = t // tm
    ret_width = y.shape[1]
    conv_width = conv_w8.shape[1]
    n_exp = rw_t.shape[0]

    def cur(s):
        return jnp.minimum(s, ntiles - 1)

    def prev(s):
        return jnp.maximum(s - 1, 0)

    def seg(i):
        return jnp.where(i % tpb == 0, n_batch, i // tpb)

    return pl.pallas_call(
        functools.partial(_mix_kernel, tpb=tpb, ntiles=ntiles, ret_width=ret_width, conv_width=conv_width),
        grid=(ntiles + 1,),
        in_specs=[
            pl.BlockSpec((tm, d), lambda s: (cur(s), 0)),
            pl.BlockSpec((tm, ret_width), lambda s: (cur(s), 0)),
            pl.BlockSpec((tm, rest.shape[1]), lambda s: (cur(s), 0)),
            pl.BlockSpec((1, SUBLANES, d), lambda s: (seg(cur(s)), 0, 0)),
            pl.BlockSpec((SUBLANES, conv_width), lambda s: (0, 0)),
            pl.BlockSpec(wo_bf16.shape, lambda s: (0, 0)),
            pl.BlockSpec((1, d), lambda s: (0, 0)),
            pl.BlockSpec((n_exp, d), lambda s: (0, 0)),
            pl.BlockSpec((n_exp, LANES), lambda s: (0, 0)),
        ],
        out_specs=[
            pl.BlockSpec((tm, d), lambda s: (cur(s), 0)),
            pl.BlockSpec((tm, d), lambda s: (cur(s), 0)),
            pl.BlockSpec((1, SUBLANES, tm), lambda s: (prev(s), 0, 0)),
            pl.BlockSpec((1, SUBLANES, tm), lambda s: (prev(s), 0, 0)),
            pl.BlockSpec((1, n_exp, LANES), lambda s: (prev(s), 0, 0)),
        ],
        out_shape=[
            jax.ShapeDtypeStruct((t, d), F32),
            jax.ShapeDtypeStruct((t, d), BF16),
            jax.ShapeDtypeStruct((ntiles, SUBLANES, tm), jnp.int32),
            jax.ShapeDtypeStruct((ntiles, SUBLANES, tm), F32),
            jax.ShapeDtypeStruct((ntiles, n_exp, LANES), jnp.int32),
        ],
        scratch_shapes=[pltpu.VMEM((tm, d), F32)],
        compiler_params=_params(1),
        name="mix",
    )(xs, y, rest, mods, conv_w8, wo_bf16, gain2, rw_t, rb_col)


def _route(h2, rw_ref, rb_ref, ei_ref, wt_ref, cnt_ref):
    h_hi, h_lo = _split_bf16(h2)
    w_hi, w_lo = _split_bf16(rw_ref[...])
    logits = lax.dot_general(w_hi, h_hi, NT_DIMS, preferred_element_type=F32)
    logits += lax.dot_general(w_hi, h_lo, NT_DIMS, preferred_element_type=F32)
    logits += lax.dot_general(w_lo, h_hi, NT_DIMS, preferred_element_type=F32)
    logits += rb_ref[:, 0:1]
    n_exp, tt = logits.shape
    expert = lax.broadcasted_iota(jnp.int32, (n_exp, tt), 0).astype(F32)
    slot = lax.broadcasted_iota(jnp.int32, (SUBLANES, tt), 0)
    ei = jnp.zeros((SUBLANES, tt), F32)
    ev = jnp.zeros((SUBLANES, tt), F32)
    taken = jnp.zeros((n_exp, tt), F32)
    top = None
    denom = jnp.zeros((1, tt), F32)
    for k in range(TOP_K):
        m = jnp.max(logits, axis=0, keepdims=True)
        idx = jnp.min(jnp.where(logits == m, expert, float(n_exp)), axis=0, keepdims=True)
        hit = expert == idx
        taken += hit.astype(F32)
        logits = jnp.where(hit, -jnp.inf, logits)
        if k == 0:
            top = m
        e = jnp.exp(m - top)
        denom += e
        ei = jnp.where(slot == k, idx, ei)
        ev = jnp.where(slot == k, e, ev)
    ei_ref[0] = ei.astype(jnp.int32)
    wt_ref[0] = jnp.where(slot < TOP_K, ev / denom, 0.0)
    cnt = jnp.sum(taken, axis=1, keepdims=True)
    cnt_ref[0] = jnp.broadcast_to(cnt, (n_exp, LANES)).astype(jnp.int32)


def _routing_plan(cnt, *, tm, sorted_rows):
    ntiles, n_exp = cnt.shape
    seg_len = (cnt + ROW_ALIGN - 1) // ROW_ALIGN * ROW_ALIGN
    off = jnp.cumsum(seg_len, axis=1) - seg_len
    tile_rows = jnp.sum(seg_len, axis=1)
    before = jnp.cumsum(seg_len, axis=0) - seg_len
    filled = jnp.sum(seg_len, axis=0)
    region = (filled + EXPERT_TILE - 1) // EXPERT_TILE * EXPERT_TILE
    region_end = jnp.cumsum(region)
    region_start = region_end - region
    base = region_start[None, :] + before

    nb_max = _max_blocks(ntiles, sorted_rows, n_exp)
    dump0 = nb_max * EXPERT_TILE
    nch = sorted_rows // ROW_ALIGN
    q = jnp.arange(nch, dtype=jnp.int32) * ROW_ALIGN
    seg_end = off + seg_len
    inside = jnp.logical_and(off[:, None, :] <= q[None, :, None], q[None, :, None] < seg_end[:, None, :])
    valid = q[None, :] < tile_rows[:, None]
    row = q[None, :] + jnp.sum(jnp.where(inside, (base - off)[:, None, :], 0), axis=2)
    parity = (jnp.arange(ntiles, dtype=jnp.int32) % 2)[:, None]
    chunk_dst = jnp.where(valid, row, dump0 + parity * sorted_rows + q[None, :]).astype(jnp.int32)
    chunk_src = jnp.where(valid, row, 0).astype(jnp.int32)

    sizes = jnp.asarray(FILL_SIZES, dtype=jnp.int32)[None, :]
    gap = (region - filled)[:, None]
    used = (gap & sizes) != 0
    fill_row = filled[:, None] + (gap & ~(2 * sizes - 1))
    fill_dump = dump0 + 2 * sorted_rows + jnp.arange(n_exp, dtype=jnp.int32)[:, None] * EXPERT_TILE + (
        EXPERT_TILE - 2 * sizes)
    fill_dst = jnp.where(used, region_start[:, None] + fill_row, fill_dump).reshape(-1).astype(jnp.int32)

    blk = jnp.arange(nb_max, dtype=jnp.int32) * EXPERT_TILE
    block_e = jnp.sum((region_end[None, :] <= blk[:, None]).astype(jnp.int32), axis=1)
    block_e = jnp.minimum(block_e, n_exp - 1).astype(jnp.int32)
    n_used = (region_end[-1] // EXPERT_TILE).astype(jnp.int32).reshape(1)
    used = region > 0
    eidx = jnp.arange(n_exp, dtype=jnp.int32)
    rank = (jnp.cumsum(used) - used).astype(jnp.int32)
    later = jnp.where(jnp.logical_and(used[None, :], eidx[None, :] > eidx[:, None]), eidx[None, :], n_exp)
    nxt = jnp.min(later, axis=1)
    nxt = jnp.where(nxt == n_exp, -1, nxt)
    mine = block_e[:, None] == eidx[None, :]
    sched = jnp.stack([block_e, jnp.sum(jnp.where(mine, (rank % 2)[None, :], 0), axis=1),
                       jnp.sum(jnp.where(mine, nxt[None, :], 0), axis=1)]).astype(jnp.int32)
    return off, chunk_dst, chunk_src, fill_dst, sched, n_used


def _max_blocks(ntiles, sorted_rows, n_exp):
    return -(-(ntiles * sorted_rows + n_exp * EXPERT_TILE) // EXPERT_TILE)


def _sorted_positions(ei, off_col, tt):
    n_exp = off_col.shape[0]
    expert = lax.broadcasted_iota(jnp.int32, (n_exp, tt), 0)
    hits = [expert == ei[k:k + 1, :] for k in range(TOP_K)]
    chosen = sum(h.astype(F32) for h in hits).astype(BF16)
    t_row = lax.broadcasted_iota(jnp.int32, (tt, tt), 0)
    t_col = lax.broadcasted_iota(jnp.int32, (tt, tt), 1)
    earlier = (t_row < t_col).astype(BF16)
    rank = jnp.dot(chosen, earlier, preferred_element_type=F32) + off_col
    return [jnp.sum(jnp.where(h, rank, 0.0), axis=0, keepdims=True) for h in hits]


def _dispatch_kernel(cd_ref, fd_ref, h2_ref, ei_ref, off_ref, xb_ref, pos_ref, buf, zeros, sem, fill_sem):
    step = pl.program_id(0)
    tt = h2_ref.shape[0]
    sorted_rows = buf.shape[1]
    nch = sorted_rows // ROW_ALIGN

    def fill_copy(e, s):
        size = FILL_SIZES[s]
        dst = pl.multiple_of(fd_ref[e * len(FILL_SIZES) + s], ROW_ALIGN)
        return pltpu.make_async_copy(zeros.at[pl.ds(0, size), :], xb_ref.at[pl.ds(dst, size), :], fill_sem)

    @pl.when(step == 0)
    def _():
        zeros[...] = jnp.zeros_like(zeros)

        def start(e, c):
            for s in range(len(FILL_SIZES)):
                fill_copy(e, s).start()
            return c

        def wait(e, c):
            for s in range(len(FILL_SIZES)):
                fill_copy(e, s).wait()
            return c

        n_regions = fd_ref.shape[0] // len(FILL_SIZES)
        lax.fori_loop(0, n_regions, start, 0)
        lax.fori_loop(0, n_regions, wait, 0)

    cur = step % 2
    last_step = pl.num_programs(0) - 1

    def chunk_copy(slot, j):
        dst = pl.multiple_of(cd_ref[0, 0, j], ROW_ALIGN)
        return pltpu.make_async_copy(buf.at[slot, pl.ds(j * ROW_ALIGN, ROW_ALIGN), :],
                                     xb_ref.at[pl.ds(dst, ROW_ALIGN), :], sem.at[slot])

    def drain(slot):
        for j in range(nch):
            chunk_copy(slot, j).wait()

    @pl.when(step >= 2)
    def _():
        drain(cur)

    pos = _sorted_positions(ei_ref[0], off_ref[0][:, 0:1], tt)
    slot_id = lax.broadcasted_iota(jnp.int32, (SUBLANES, tt), 0)
    pos8 = jnp.zeros((SUBLANES, tt), F32)
    for k in range(TOP_K):
        pos8 = jnp.where(slot_id == k, pos[k], pos8)
    pos_ref[0] = pos8
    r = lax.broadcasted_iota(jnp.int32, (sorted_rows, tt), 0)
    hit = r == pos[0].astype(jnp.int32)
    for k in range(1, TOP_K):
        hit = jnp.logical_or(hit, r == pos[k].astype(jnp.int32))
    perm = jnp.where(hit, 1.0, 0.0).astype(BF16)
    buf[cur] = _pack_bf16_pairs(jnp.dot(perm, h2_ref[...], preferred_element_type=F32))
    for j in range(nch):
        chunk_copy(cur, j).start()

    @pl.when(step == last_step)
    def _():
        drain(cur)

    @pl.when(jnp.logical_and(step == last_step, step >= 1))
    def _():
        drain(1 - cur)


def _dispatch(h2, ei, off_col, chunk_dst, fill_dst, *, tm, sorted_rows, total_rows):
    t, d = h2.shape
    ntiles = t // tm
    n_exp = off_col.shape[1]
    nch = sorted_rows // ROW_ALIGN

    return pl.pallas_call(
        _dispatch_kernel,
        grid=(ntiles,),
        in_specs=[
            pl.BlockSpec((1, 1, nch), lambda i: (i, 0, 0), memory_space=pltpu.SMEM),
            pl.BlockSpec(memory_space=pltpu.SMEM),
            pl.BlockSpec((tm, d), lambda i: (i, 0)),
            pl.BlockSpec((1, SUBLANES, tm), lambda i: (i, 0, 0)),
            pl.BlockSpec((1, n_exp, LANES), lambda i: (i, 0, 0)),
        ],
        out_specs=[
            pl.BlockSpec(memory_space=pl.ANY),
            pl.BlockSpec((1, SUBLANES, tm), lambda i: (i, 0, 0)),
        ],
        out_shape=[
            jax.ShapeDtypeStruct((total_rows, d // 2), jnp.uint32),
            jax.ShapeDtypeStruct((ntiles, SUBLANES, tm), F32),
        ],
        scratch_shapes=[
            pltpu.VMEM((2, sorted_rows, d // 2), jnp.uint32),
            pltpu.VMEM((FILL_SIZES[0], d // 2), jnp.uint32),
            pltpu.SemaphoreType.DMA((2,)),
            pltpu.SemaphoreType.DMA(()),
        ],
        compiler_params=_params(1),
        name="dispatch",
    )(chunk_dst.reshape(ntiles, 1, nch), fill_dst, h2, ei, off_col)


def _experts_kernel(sched_ref, nu_ref, xb_ref, wu_hbm, bu_ref, wd_hbm, bd_ref, yb_ref,
                    wu_f, wd_f, wu_s, wd_s, sem, *, d_ff, layer):
    i = pl.program_id(0)
    expert, slot, nxt = sched_ref[0, i], sched_ref[1, i], sched_ref[2, i]
    fresh = jnp.logical_or(i == 0, expert != sched_ref[0, jnp.maximum(i - 1, 0)])

    def fetch(e, s):
        return (pltpu.make_async_copy(wu_hbm.at[layer, e], wu_f.at[s], sem.at[0, s]),
                pltpu.make_async_copy(wd_hbm.at[layer, e], wd_f.at[s], sem.at[1, s]))

    @pl.when(i == 0)
    def _():
        for c in fetch(expert, slot):
            c.start()

    @pl.when(jnp.logical_and(fresh, i < nu_ref[0]))
    def _():
        for c in fetch(expert, slot):
            c.wait()

        @pl.when(nxt >= 0)
        def _():
            for c in fetch(nxt, 1 - slot):
                c.start()

        wu_s[...] = wu_f[slot].astype(BF16)
        wd_s[...] = wd_f[slot].astype(BF16)

    @pl.when(i < nu_ref[0])
    def _():
        u = jnp.dot(_unpack_bf16_pairs(xb_ref[...]), wu_s[...], preferred_element_type=F32) + bu_ref[0, 0]
        gate = jnp.minimum(u[:, :d_ff], SWIGLU_LIMIT)
        lin = jnp.clip(u[:, d_ff:], -SWIGLU_LIMIT, SWIGLU_LIMIT)
        act = (gate * _sigmoid(SWIGLU_ALPHA * gate) * (lin + 1.0)).astype(BF16)
        y = jnp.dot(act, wd_s[...], preferred_element_type=F32) + bd_ref[0, 0]
        yb_ref[...] = _pack_bf16_pairs(y.astype(BF16).astype(F32))


def _experts(xb, sched, n_used, w_up, b_up, w_down, b_down, *, layer, nb_max):
    depth, n_exp, d, two_ff = w_up.shape
    d_ff = two_ff // 2
    dp = xb.shape[1]

    def blk(i, sc, nu):
        return jnp.minimum(i, nu[0] - 1)

    def exp(i, sc, nu):
        return sc[0, jnp.minimum(i, nu[0] - 1)]

    return pl.pallas_call(
        functools.partial(_experts_kernel, d_ff=d_ff, layer=layer),
        grid_spec=pltpu.PrefetchScalarGridSpec(
            num_scalar_prefetch=2,
            grid=(nb_max,),
            in_specs=[
                pl.BlockSpec((EXPERT_TILE, dp), lambda i, sc, nu: (blk(i, sc, nu), 0)),
                pl.BlockSpec(memory_space=pl.ANY),
                pl.BlockSpec((1, 1, 1, two_ff), lambda i, sc, nu: (layer, exp(i, sc, nu), 0, 0)),
                pl.BlockSpec(memory_space=pl.ANY),
                pl.BlockSpec((1, 1, 1, d), lambda i, sc, nu: (layer, exp(i, sc, nu), 0, 0)),
            ],
            out_specs=pl.BlockSpec((EXPERT_TILE, dp), lambda i, sc, nu: (blk(i, sc, nu), 0)),
            scratch_shapes=[
                pltpu.VMEM((2, d, two_ff), F32),
                pltpu.VMEM((2, d_ff, d), F32),
                pltpu.VMEM((d, two_ff), BF16),
                pltpu.VMEM((d_ff, d), BF16),
                pltpu.SemaphoreType.DMA((2, 2)),
            ],
        ),
        out_shape=jax.ShapeDtypeStruct((nb_max * EXPERT_TILE, dp), jnp.uint32),
        compiler_params=_params(1),
        name="experts",
    )(sched, n_used, xb, w_up, b_up.reshape(depth, n_exp, 1, two_ff), w_down, b_down.reshape(depth, n_exp, 1, d))


def _combine_kernel(cs_ref, csn_ref, x_ref, mod_ref, pos_ref, wt_ref, yb_ref, *rest, qk_width, follow):
    if follow == "inproj":
        mod2_ref, gain_ref, w_ref, cos_ref, sin_ref, o_ref, qkv_ref, rest_ref, buf, sem, x_prev = rest

        @pl.when(pl.program_id(0) == 0)
        def _():
            x_prev[...] = jnp.zeros_like(x_prev)

        _combine_fetch(cs_ref, csn_ref, yb_ref, buf, sem)()
        _inproj_tile(x_prev[...], mod2_ref, gain_ref, w_ref, cos_ref, sin_ref, qkv_ref, rest_ref, qk_width)
        x_new = _combine_tile(x_ref, mod_ref, pos_ref, wt_ref, buf)
        o_ref[...] = x_new
        x_prev[...] = x_new
    else:
        gain_ref, o_ref, buf, sem = rest
        wait = _combine_fetch(cs_ref, csn_ref, yb_ref, buf, sem)
        x_new = _combine_tile(x_ref, mod_ref, pos_ref, wt_ref, buf, wait)
        ms = jnp.mean(x_new * x_new, axis=-1, keepdims=True)
        o_ref[0] = x_new * lax.rsqrt(ms + EPS) * gain_ref[...]


def _combine_fetch(cs_ref, csn_ref, yb_ref, buf, sem):
    step = pl.program_id(0)
    cur = step % 2
    nch = buf.shape[1] // ROW_ALIGN

    def chunk_copy(table, slot, j):
        src = pl.multiple_of(table[0, 0, j], ROW_ALIGN)
        return pltpu.make_async_copy(yb_ref.at[pl.ds(src, ROW_ALIGN), :],
                                     buf.at[slot, pl.ds(j * ROW_ALIGN, ROW_ALIGN), :], sem.at[slot])

    @pl.when(step == 0)
    def _():
        for j in range(nch):
            chunk_copy(cs_ref, cur, j).start()

    @pl.when(step + 1 < pl.num_programs(0))
    def _():
        for j in range(nch):
            chunk_copy(csn_ref, 1 - cur, j).start()

    def wait():
        for j in range(nch):
            chunk_copy(cs_ref, cur, j).wait()

    return wait


def _combine_tile(x_ref, mod_ref, pos_ref, wt_ref, buf, wait=None):
    tt = x_ref.shape[0]
    sorted_rows = buf.shape[1]
    r = lax.broadcasted_iota(jnp.int32, (sorted_rows, tt), 0)
    pos = pos_ref[0].astype(jnp.int32)
    wt = wt_ref[0]
    sel = jnp.zeros((sorted_rows, tt), F32)
    for k in reversed(range(TOP_K)):
        sel = jnp.where(r == pos[k:k + 1, :], wt[k:k + 1, :], sel)
    if wait is not None:
        wait()
    y2 = lax.dot_general(sel.astype(BF16), _unpack_bf16_pairs(buf[pl.program_id(0) % 2]), TN_DIMS,
                         preferred_element_type=F32)
    return x_ref[...] + mod_ref[0, 5:6, :] * y2


def _combine(xs, mods, pos, wt, chunk_src, yb, *, tm, tpb, n_batch, sorted_rows, inproj=None, final=None):
    t, d = xs.shape
    ntiles = t // tm
    nch = sorted_rows // ROW_ALIGN

    def seg(i):
        return jnp.where(i % tpb == 0, n_batch, i // tpb)

    def cur(s):
        return jnp.minimum(s, ntiles - 1)

    def prev(s):
        return jnp.maximum(s - 1, 0)

    in_specs = [
        pl.BlockSpec((1, 1, nch), lambda s: (cur(s), 0, 0), memory_space=pltpu.SMEM),
        pl.BlockSpec((1, 1, nch), lambda s: (cur(s + 1), 0, 0), memory_space=pltpu.SMEM),
        pl.BlockSpec((tm, d), lambda s: (cur(s), 0)),
        pl.BlockSpec((1, SUBLANES, d), lambda s: (seg(cur(s)), 0, 0)),
        pl.BlockSpec((1, SUBLANES, tm), lambda s: (cur(s), 0, 0)),
        pl.BlockSpec((1, SUBLANES, tm), lambda s: (cur(s), 0, 0)),
        pl.BlockSpec(memory_space=pl.ANY),
    ]
    args = [chunk_src.reshape(ntiles, 1, nch), chunk_src.reshape(ntiles, 1, nch), xs, mods, pos, wt, yb]
    scratch = [pltpu.VMEM((2, sorted_rows, yb.shape[1]), jnp.uint32), pltpu.SemaphoreType.DMA((2,))]
    if inproj is not None:
        mods2, gain, w_bf16, cos_t, sin_t, qk_width, qkv_width = inproj
        in_width = w_bf16.shape[1]
        pos_blocks = cos_t.shape[0] // tm - 1

        def rope_blk(i):
            j = i % tpb
            return jnp.where(j == 0, pos_blocks, j - 1)

        in_specs += [
            pl.BlockSpec((1, SUBLANES, d), lambda s: (seg(prev(s)), 0, 0)),
            pl.BlockSpec((1, d), lambda s: (0, 0)),
            pl.BlockSpec((d, in_width), lambda s: (0, 0)),
            pl.BlockSpec((tm, RET_DK), lambda s: (rope_blk(prev(s)), 0)),
            pl.BlockSpec((tm, RET_DK), lambda s: (rope_blk(prev(s)), 0)),
        ]
        args += [mods2, gain, w_bf16, cos_t, sin_t]
        out_specs = [pl.BlockSpec((tm, d), lambda s: (cur(s), 0)),
                     pl.BlockSpec((tm, qkv_width), lambda s: (prev(s), 0)),
                     pl.BlockSpec((tm, in_width - qkv_width), lambda s: (prev(s), 0))]
        out_shape = [jax.ShapeDtypeStruct((t, d), F32), jax.ShapeDtypeStruct((t, qkv_width), BF16),
                     jax.ShapeDtypeStruct((t, in_width - qkv_width), BF16)]
        scratch += [pltpu.VMEM((tm, d), F32)]
        steps, follow = ntiles + 1, "inproj"
    else:
        gain, n = final
        qk_width = None
        in_specs += [pl.BlockSpec((1, d), lambda s: (0, 0))]
        args += [gain]
        out_specs = pl.BlockSpec((1, tm, d), lambda s: (s // tpb, jnp.maximum(s % tpb - 1, 0), 0))
        out_shape = jax.ShapeDtypeStruct((n_batch, n, d), F32)
        steps, follow = ntiles, "final"

    return pl.pallas_call(
        functools.partial(_combine_kernel, qk_width=qk_width, follow=follow),
        grid=(steps,),
        in_specs=in_specs,
        out_specs=out_specs,
        out_shape=out_shape,
        scratch_shapes=scratch,
        compiler_params=_params(1),
        name="combine",
    )(*args)


def _rope_tables(n, tm):
    t = jnp.arange(n, dtype=jnp.int32)
    m = RET_DK // 4
    freqs = ROPE_BASE ** (-jnp.arange(m, dtype=F32) / m)
    ang_r = (t // GRID_W).astype(F32)[:, None] * freqs[None, :]
    ang_c = (t % GRID_W).astype(F32)[:, None] * freqs[None, :]
    cos = jnp.concatenate([jnp.cos(ang_r)] * 2 + [jnp.cos(ang_c)] * 2, axis=1)
    sin = jnp.concatenate([-jnp.sin(ang_r), jnp.sin(ang_r), -jnp.sin(ang_c), jnp.sin(ang_c)], axis=1)
    cos = jnp.concatenate([cos, jnp.ones((tm, RET_DK), F32)], axis=0)
    sin = jnp.concatenate([sin, jnp.zeros((tm, RET_DK), F32)], axis=0)
    return cos, sin


def kernel(x, c, ctx, c_ctx, w_mod, b_mod, norm_mix, norm_ffn, w_in, ret_decay_f, ret_decay_b, conv_w, w_out,
           router_w, router_b, w_up, b_up, w_down, b_down, norm_final):
    n_batch, n, d = x.shape
    lc = ctx.shape[1]
    depth = w_mod.shape[0]
    tm = lc
    assert tm % 256 == 0 and n % tm == 0 and n % GRID_W == 0 and tm % GRID_W == 0
    assert n % RET_CHUNK == 0 and lc % RET_CHUNK == 0 and n_batch + 1 <= SUBLANES
    tpb = (lc + n) // tm
    qk_width = RET_HEADS * RET_DK
    qkv_width = 2 * qk_width + RET_HEADS * RET_DV
    n_exp = router_w.shape[2]
    sorted_rows = TOP_K * tm + n_exp * ROW_ALIGN
    ntiles = n_batch * tpb
    nb_max = _max_blocks(ntiles, sorted_rows, n_exp)
    total_rows = nb_max * EXPERT_TILE + 2 * sorted_rows + n_exp * EXPERT_TILE

    cond = jnp.zeros((SUBLANES, d), F32).at[:n_batch].set(c).at[n_batch].set(c_ctx)
    mod_all = _modulation(cond, w_mod, b_mod).reshape(depth, SUBLANES, 6, d)
    mod_all = jnp.pad(mod_all, ((0, 0), (0, 0), (0, SUBLANES - 6), (0, 0)))
    cos_t, sin_t = _rope_tables(n, tm)

    tile_args = dict(tm=tm, tpb=tpb, n_batch=n_batch)
    xs, qkv, rest = _inproj(ctx, x, mod_all[0], norm_mix[0][None, :], w_in[0].astype(BF16), cos_t, sin_t,
                            qk_width=qk_width, qkv_width=qkv_width, **tile_args)
    for layer in range(depth):
        mods = mod_all[layer]
        dec, dmat, gc = _decay_tables(ret_decay_f[layer], ret_decay_b[layer])
        y = _retention(qkv, dec, dmat, gc, n_batch=n_batch, lc=lc, n=n, qk_width=qk_width)
        conv_w8 = jnp.pad(conv_w[layer], ((0, SUBLANES - conv_w.shape[1]), (0, 0)))
        gain2 = norm_ffn[layer][None, :]
        rb_col = jnp.broadcast_to(router_b[layer][:, None], (n_exp, LANES))
        xs, h2, ei, wt, cnt = _mix(xs, y, rest, mods, conv_w8, w_out[layer].astype(BF16), gain2, router_w[layer].T,
                                   rb_col, **tile_args)
        off, chunk_dst, chunk_src, fill_dst, sched, n_used = _routing_plan(
            cnt[:, :, 0], tm=tm, sorted_rows=sorted_rows)
        off_col = jnp.broadcast_to(off.astype(F32)[:, :, None], (ntiles, n_exp, LANES))
        xb, pos = _dispatch(h2, ei, off_col, chunk_dst, fill_dst, tm=tm, sorted_rows=sorted_rows,
                            total_rows=total_rows)
        yb = _experts(xb, sched, n_used, w_up, b_up, w_down, b_down, layer=layer, nb_max=nb_max)
        if layer + 1 == depth:
            return _combine(xs, mods, pos, wt, chunk_src, yb, sorted_rows=sorted_rows,
                            final=(norm_final[None, :], n), **tile_args)
        nxt = layer + 1
        xs, qkv, rest = _combine(
            xs, mods, pos, wt, chunk_src, yb, sorted_rows=sorted_rows,
            inproj=(mod_all[nxt], norm_mix[nxt][None, :], w_in[nxt].astype(BF16), cos_t, sin_t, qk_width, qkv_width),
            **tile_args)
```

```python
import functools

import jax
import jax.numpy as jnp
from jax import lax
from jax.experimental import pallas as pl
from jax.experimental.pallas import tpu as pltpu

GRID_W = 64
RET_HEADS = 4
RET_DK = 128
RET_DV = 256
RET_CHUNK = 256
TOP_K = 4
SWIGLU_LIMIT = 7.0
SWIGLU_ALPHA = 1.702
ROPE_BASE = 10000.0
EPS = 1e-6

SUBLANES = 8
LANES = 128
VMEM_LIMIT = 56 * 1024 * 1024

EXPERT_TILE = 512
ROW_ALIGN = SUBLANES
FILL_SIZES = tuple(EXPERT_TILE >> k for k in range(1, EXPERT_TILE.bit_length()) if EXPERT_TILE >> k >= ROW_ALIGN)

F32 = jnp.float32
BF16 = jnp.bfloat16
NT_DIMS = (((1,), (1,)), ((), ()))
TN_DIMS = (((0,), (0,)), ((), ()))


def _params(n_axes=1):
    return pltpu.CompilerParams(dimension_semantics=("arbitrary",) * n_axes, vmem_limit_bytes=VMEM_LIMIT)


def _modulated_norm(x, gain, shift, scale):
    ms = jnp.mean(x * x, axis=-1, keepdims=True)
    return x * lax.rsqrt(ms + EPS) * gain * (1.0 + scale) + shift


def _sigmoid(z):
    return 1.0 / (1.0 + jnp.exp(-z))


def _split_bf16(a):
    hi = a.astype(BF16)
    lo = (a - hi.astype(F32)).astype(BF16)
    return hi, lo


def _pack_bf16_pairs(a):
    w = a.shape[1] // 2
    lo = lax.bitcast_convert_type(a[:, :w], jnp.uint32)
    hi = lax.bitcast_convert_type(a[:, w:], jnp.uint32)
    return (lo >> 16) | (hi & jnp.uint32(0xFFFF0000))


def _unpack_bf16_pairs(p):
    lo = lax.bitcast_convert_type(p << 16, F32)
    hi = lax.bitcast_convert_type(p & jnp.uint32(0xFFFF0000), F32)
    return jnp.concatenate([lo, hi], axis=1).astype(BF16)


def _modulation_kernel(c_ref, w_ref, b_ref, o_ref):
    c = c_ref[...]
    s = c * _sigmoid(c)
    s_hi, s_lo = _split_bf16(s)
    w_hi, w_lo = _split_bf16(w_ref[0])
    acc = jnp.dot(s_hi, w_hi, preferred_element_type=F32)
    acc += jnp.dot(s_hi, w_lo, preferred_element_type=F32)
    acc += jnp.dot(s_lo, w_hi, preferred_element_type=F32)
    o_ref[0] = acc + b_ref[0]


def _modulation(cond, w_mod, b_mod):
    depth, d, six_d = w_mod.shape
    nblk = six_d // d
    return pl.pallas_call(
        _modulation_kernel,
        grid=(depth, nblk),
        in_specs=[
            pl.BlockSpec((SUBLANES, d), lambda l, j: (0, 0)),
            pl.BlockSpec((1, d, d), lambda l, j: (l, 0, j)),
            pl.BlockSpec((1, 1, d), lambda l, j: (l, 0, j)),
        ],
        out_specs=pl.BlockSpec((1, SUBLANES, d), lambda l, j: (l, 0, j)),
        out_shape=jax.ShapeDtypeStruct((depth, SUBLANES, six_d), F32),
        compiler_params=_params(2),
        name="modulation",
    )(cond, w_mod, b_mod.reshape(depth, 1, six_d))


def _inproj_kernel(ctx_ref, x_ref, mod_ref, gain_ref, w_ref, cos_ref, sin_ref, xs_ref, qkv_ref, rest_ref, *,
                   qk_width, tpb):
    is_ctx = (pl.program_id(0) % tpb) == 0
    x = jnp.where(is_ctx, ctx_ref[0], x_ref[0])
    xs_ref[...] = x
    _inproj_tile(x, mod_ref, gain_ref, w_ref, cos_ref, sin_ref, qkv_ref, rest_ref, qk_width)


def _inproj_tile(x, mod_ref, gain_ref, w_ref, cos_ref, sin_ref, qkv_ref, rest_ref, qk_width):
    q_scale = RET_DK ** -0.5
    h = _modulated_norm(x, gain_ref[...], mod_ref[0, 0:1, :], mod_ref[0, 1:2, :]).astype(BF16)
    reps = qk_width // RET_DK
    cos = jnp.concatenate([cos_ref[...]] * reps, axis=1)
    sin = jnp.concatenate([sin_ref[...]] * reps, axis=1)
    lane = lax.broadcasted_iota(jnp.int32, cos.shape, 1)
    first_half = (lane % (RET_DK // 2)) < (RET_DK // 4)
    for part, scale in ((0, q_scale), (1, 1.0)):
        c0 = part * qk_width
        p = jnp.dot(h, w_ref[:, c0:c0 + qk_width], preferred_element_type=F32)
        partner = jnp.where(first_half, pltpu.roll(p, qk_width - RET_DK // 4, 1), pltpu.roll(p, RET_DK // 4, 1))
        r = p * cos + partner * sin
        if scale != 1.0:
            r = r * scale
        qkv_ref[:, c0:c0 + qk_width] = r.astype(BF16)
    qkv_width = qkv_ref.shape[1]
    step = 512
    for c0 in range(2 * qk_width, qkv_width, step):
        qkv_ref[:, c0:c0 + step] = jnp.dot(h, w_ref[:, c0:c0 + step], preferred_element_type=F32).astype(BF16)
    for c0 in range(0, rest_ref.shape[1], step):
        rest_ref[:, c0:c0 + step] = jnp.dot(
            h, w_ref[:, qkv_width + c0:qkv_width + c0 + step], preferred_element_type=F32).astype(BF16)


def _inproj(ctx, x, mods, gain, w_bf16, cos_t, sin_t, *, tm, tpb, n_batch, qk_width, qkv_width):
    d = x.shape[2]
    t = n_batch * tpb * tm
    in_width = w_bf16.shape[1]
    rest_width = in_width - qkv_width
    pos_blocks = cos_t.shape[0] // tm - 1

    def seg(i):
        return jnp.where(i % tpb == 0, n_batch, i // tpb)

    def pos(i):
        j = i % tpb
        return jnp.where(j == 0, pos_blocks, j - 1)

    return pl.pallas_call(
        functools.partial(_inproj_kernel, qk_width=qk_width, tpb=tpb),
        grid=(t // tm,),
        in_specs=[
            pl.BlockSpec((1, tm, d), lambda i: (i // tpb, 0, 0)),
            pl.BlockSpec((1, tm, d), lambda i: (i // tpb, jnp.maximum(i % tpb - 1, 0), 0)),
            pl.BlockSpec((1, SUBLANES, d), lambda i: (seg(i), 0, 0)),
            pl.BlockSpec((1, d), lambda i: (0, 0)),
            pl.BlockSpec((d, in_width), lambda i: (0, 0)),
            pl.BlockSpec((tm, RET_DK), lambda i: (pos(i), 0)),
            pl.BlockSpec((tm, RET_DK), lambda i: (pos(i), 0)),
        ],
        out_specs=[
            pl.BlockSpec((tm, d), lambda i: (i, 0)),
            pl.BlockSpec((tm, qkv_width), lambda i: (i, 0)),
            pl.BlockSpec((tm, rest_width), lambda i: (i, 0)),
        ],
        out_shape=[
            jax.ShapeDtypeStruct((t, d), F32),
            jax.ShapeDtypeStruct((t, qkv_width), BF16),
            jax.ShapeDtypeStruct((t, rest_width), BF16),
        ],
        compiler_params=_params(1),
        name="inproj",
    )(ctx, x, mods, gain, w_bf16, cos_t, sin_t)


def _retention_kernel(gc_ref, q_ref, k_ref, v_ref, dec_ref, dmat_ref, y_ref, sf_ref, sb_ref, st_ref, *, lc, n):
    head = pl.program_id(1)
    c = RET_CHUNK
    qd = dec_ref[0, :, 0:2 * RET_DK]
    kdf = dec_ref[0, :, 2 * RET_DK:3 * RET_DK]
    kdb = dec_ref[0, :, 3 * RET_DK:4 * RET_DK]
    gcf = gc_ref[head, 0]
    gcb = gc_ref[head, 1]

    def rows(chunk):
        return pl.ds(pl.multiple_of(chunk * c, c), c)

    def advance(s_ref, chunk, kd, gc):
        kdec = (k_ref[rows(chunk), :].astype(F32) * kd).astype(BF16)
        kv = lax.dot_general(kdec, v_ref[rows(chunk), :], TN_DIMS, preferred_element_type=F32)
        s_ref[...] = gc * s_ref[...] + kv

    def sweep(chunk0, nchunks):
        def states(i, carry):
            cf = chunk0 + i
            cb = chunk0 + nchunks - 1 - i
            st_ref[cf, 0:RET_DK, :] = sf_ref[...].astype(BF16)
            st_ref[cb, RET_DK:2 * RET_DK, :] = sb_ref[...].astype(BF16)
            advance(sf_ref, cf, kdf, gcf)
            advance(sb_ref, cb, kdb, gcb)
            return carry

        def outputs(i, carry):
            r = rows(chunk0 + i)
            q, k, v = q_ref[r, :], k_ref[r, :], v_ref[r, :]
            scores = lax.dot_general(q, k, NT_DIMS, preferred_element_type=F32)
            qdec = (jnp.concatenate([q, q], axis=1).astype(F32) * qd).astype(BF16)
            y = jnp.dot((scores * dmat_ref[0]).astype(BF16), v, preferred_element_type=F32)
            y += jnp.dot(qdec, st_ref[chunk0 + i], preferred_element_type=F32)
            y_ref[r, :] = (y * lax.rsqrt(jnp.mean(y * y, axis=-1, keepdims=True) + EPS)).astype(BF16)
            return carry

        lax.fori_loop(0, nchunks, states, 0, unroll=4 if nchunks % 4 == 0 else 1)
        lax.fori_loop(0, nchunks, outputs, 0, unroll=8 if nchunks % 8 == 0 else 1)

    sf_ref[...] = jnp.zeros_like(sf_ref)
    sb_ref[...] = jnp.zeros_like(sb_ref)
    sweep(0, lc // c)
    sweep(lc // c, n // c)


def _retention(qkv, dec, dmat, gc, *, n_batch, lc, n, qk_width):
    s = lc + n
    t = qkv.shape[0]
    k_blk0 = qk_width // RET_DK
    v_blk0 = 2 * qk_width // RET_DV
    return pl.pallas_call(
        functools.partial(_retention_kernel, lc=lc, n=n),
        grid_spec=pltpu.PrefetchScalarGridSpec(
            num_scalar_prefetch=1,
            grid=(n_batch, RET_HEADS),
            in_specs=[
                pl.BlockSpec((s, RET_DK), lambda b, h, gc: (b, h)),
                pl.BlockSpec((s, RET_DK), lambda b, h, gc: (b, k_blk0 + h)),
                pl.BlockSpec((s, RET_DV), lambda b, h, gc: (b, v_blk0 + h)),
                pl.BlockSpec((1, RET_CHUNK, 4 * RET_DK), lambda b, h, gc: (h, 0, 0)),
                pl.BlockSpec((1, RET_CHUNK, RET_CHUNK), lambda b, h, gc: (h, 0, 0)),
            ],
            out_specs=pl.BlockSpec((s, RET_DV), lambda b, h, gc: (b, h)),
            scratch_shapes=[
                pltpu.VMEM((RET_DK, RET_DV), F32),
                pltpu.VMEM((RET_DK, RET_DV), F32),
                pltpu.VMEM((s // RET_CHUNK, 2 * RET_DK, RET_DV), BF16),
            ],
        ),
        out_shape=jax.ShapeDtypeStruct((t, RET_HEADS * RET_DV), BF16),
        compiler_params=_params(2),
        name="retention",
    )(gc, qkv, qkv, qkv, dec, dmat)


def _decay_tables(logit_f, logit_b):
    c = RET_CHUNK
    lg_f = jax.nn.log_sigmoid(logit_f.astype(F32))[:, None]
    lg_b = jax.nn.log_sigmoid(logit_b.astype(F32))[:, None]
    i = jnp.arange(c, dtype=F32)[None, :]
    vecs = jnp.stack([
        jnp.exp((i + 1) * lg_f), jnp.exp((c - i) * lg_b),
        jnp.exp((c - 1 - i) * lg_f), jnp.exp(i * lg_b)], axis=2)
    dec = jnp.broadcast_to(vecs[..., None], vecs.shape + (RET_DK,)).reshape(vecs.shape[0], c, 4 * RET_DK)
    diff = i[0][:, None] - i[0][None, :]
    lower = jnp.where(diff >= 0, jnp.exp(jnp.where(diff >= 0, diff, 0.0)[None] * lg_f[:, :, None]), 0.0)
    upper = jnp.where(diff < 0, jnp.exp(jnp.where(diff < 0, -diff, 0.0)[None] * lg_b[:, :, None]), 0.0)
    gc = jnp.concatenate([jnp.exp(c * lg_f), jnp.exp(c * lg_b)], axis=1)
    return dec, lower + upper, gc


def _mix_kernel(x_ref, y_ref, rest_ref, mod_ref, cw_ref, wo_ref, gain2_ref, rw_ref, rb_ref,
                o_ref, h2_ref, ei_ref, wt_ref, cnt_ref, h_prev, *, tpb, ntiles, ret_width, conv_width):
    step = pl.program_id(0)

    @pl.when(step == 0)
    def _():
        h_prev[...] = jnp.zeros_like(h_prev)

    _route(h_prev[...], rw_ref, rb_ref, ei_ref, wt_ref, cnt_ref)
    is_ctx = (jnp.minimum(step, ntiles - 1) % tpb) == 0
    tm = x_ref.shape[0]
    g = rest_ref[:, 0:ret_width].astype(F32)
    cb = rest_ref[:, ret_width:ret_width + conv_width].astype(F32)
    cc = rest_ref[:, ret_width + conv_width:ret_width + 2 * conv_width].astype(F32)
    cx = rest_ref[:, ret_width + 2 * conv_width:ret_width + 3 * conv_width].astype(F32)
    ret = (g * _sigmoid(g) * y_ref[...].astype(F32)).astype(BF16)
    u = cc * cx
    row = lax.broadcasted_iota(jnp.int32, u.shape, 0)
    pos = jnp.where(is_ctx, row, row % GRID_W)
    last = jnp.where(is_ctx, tm - 1, GRID_W - 1)
    u_prev = jnp.where(pos == 0, 0.0, pltpu.roll(u, 1, 0))
    u_next = jnp.where(pos == last, 0.0, pltpu.roll(u, tm - 1, 0))
    conv = (cb * (u_prev * cw_ref[0:1, :] + u * cw_ref[1:2, :] + u_next * cw_ref[2:3, :])).astype(BF16)
    mix = jnp.dot(ret, wo_ref[0:ret_width, :], preferred_element_type=F32)
    mix += jnp.dot(conv, wo_ref[ret_width:ret_width + conv_width, :], preferred_element_type=F32)
    x_new = x_ref[...] + mod_ref[0, 2:3, :] * mix
    o_ref[...] = x_new
    h2 = _modulated_norm(x_new, gain2_ref[...], mod_ref[0, 3:4, :], mod_ref[0, 4:5, :])
    h2_ref[...] = h2.astype(BF16)
    h_prev[...] = h2


def _mix(xs, y, rest, mods, conv_w8, wo_bf16, gain2, rw_t, rb_col, *, tm, tpb, n_batch):
    t, d = xs.shape
    ntiles = t // tm
    ret_width = y.shape[1]
    conv_width = conv_w8.shape[1]
    n_exp = rw_t.shape[0]

    def cur(s):
        return jnp.minimum(s, ntiles - 1)

    def prev(s):
        return jnp.maximum(s - 1, 0)

    def seg(i):
        return jnp.where(i % tpb == 0, n_batch, i // tpb)

    return pl.pallas_call(
        functools.partial(_mix_kernel, tpb=tpb, ntiles=ntiles, ret_width=ret_width, conv_width=conv_width),
        grid=(ntiles + 1,),
        in_specs=[
            pl.BlockSpec((tm, d), lambda s: (cur(s), 0)),
            pl.BlockSpec((tm, ret_width), lambda s: (cur(s), 0)),
            pl.BlockSpec((tm, rest.shape[1]), lambda s: (cur(s), 0)),
            pl.BlockSpec((1, SUBLANES, d), lambda s: (seg(cur(s)), 0, 0)),
            pl.BlockSpec((SUBLANES, conv_width), lambda s: (0, 0)),
            pl.BlockSpec(wo_bf16.shape, lambda s: (0, 0)),
            pl.BlockSpec((1, d), lambda s: (0, 0)),
            pl.BlockSpec((n_exp, d), lambda s: (0, 0)),
            pl.BlockSpec((n_exp, LANES), lambda s: (0, 0)),
        ],
        out_specs=[
            pl.BlockSpec((tm, d), lambda s: (cur(s), 0)),
            pl.BlockSpec((tm, d), lambda s: (cur(s), 0)),
            pl.BlockSpec((1, SUBLANES, tm), lambda s: (prev(s), 0, 0)),
            pl.BlockSpec((1, SUBLANES, tm), lambda s: (prev(s), 0, 0)),
            pl.BlockSpec((1, n_exp, LANES), lambda s: (prev(s), 0, 0)),
        ],
        out_shape=[
            jax.ShapeDtypeStruct((t, d), F32),
            jax.ShapeDtypeStruct((t, d), BF16),
            jax.ShapeDtypeStruct((ntiles, SUBLANES, tm), jnp.int32),
            jax.ShapeDtypeStruct((ntiles, SUBLANES, tm), F32),
            jax.ShapeDtypeStruct((ntiles, n_exp, LANES), jnp.int32),
        ],
        scratch_shapes=[pltpu.VMEM((tm, d), F32)],
        compiler_params=_params(1),
        name="mix",
    )(xs, y, rest, mods, conv_w8, wo_bf16, gain2, rw_t, rb_col)


def _route(h2, rw_ref, rb_ref, ei_ref, wt_ref, cnt_ref):
    h_hi, h_lo = _split_bf16(h2)
    w_hi, w_lo = _split_bf16(rw_ref[...])
    logits = lax.dot_general(w_hi, h_hi, NT_DIMS, preferred_element_type=F32)
    logits += lax.dot_general(w_hi, h_lo, NT_DIMS, preferred_element_type=F32)
    logits += lax.dot_general(w_lo, h_hi, NT_DIMS, preferred_element_type=F32)
    logits += rb_ref[:, 0:1]
    n_exp, tt = logits.shape
    expert = lax.broadcasted_iota(jnp.int32, (n_exp, tt), 0).astype(F32)
    slot = lax.broadcasted_iota(jnp.int32, (SUBLANES, tt), 0)
    ei = jnp.zeros((SUBLANES, tt), F32)
    ev = jnp.zeros((SUBLANES, tt), F32)
    taken = jnp.zeros((n_exp, tt), F32)
    top = None
    denom = jnp.zeros((1, tt), F32)
    for k in range(TOP_K):
        m = jnp.max(logits, axis=0, keepdims=True)
        idx = jnp.min(jnp.where(logits == m, expert, float(n_exp)), axis=0, keepdims=True)
        hit = expert == idx
        taken += hit.astype(F32)
        logits = jnp.where(hit, -jnp.inf, logits)
        if k == 0:
            top = m
        e = jnp.exp(m - top)
        denom += e
        ei = jnp.where(slot == k, idx, ei)
        ev = jnp.where(slot == k, e, ev)
    ei_ref[0] = ei.astype(jnp.int32)
    wt_ref[0] = jnp.where(slot < TOP_K, ev / denom, 0.0)
    cnt = jnp.sum(taken, axis=1, keepdims=True)
    cnt_ref[0] = jnp.broadcast_to(cnt, (n_exp, LANES)).astype(jnp.int32)


def _routing_plan(cnt, *, tm, sorted_rows):
    ntiles, n_exp = cnt.shape
    seg_len = (cnt + ROW_ALIGN - 1) // ROW_ALIGN * ROW_ALIGN
    off = jnp.cumsum(seg_len, axis=1) - seg_len
    tile_rows = jnp.sum(seg_len, axis=1)
    before = jnp.cumsum(seg_len, axis=0) - seg_len
    filled = jnp.sum(seg_len, axis=0)
    region = (filled + EXPERT_TILE - 1) // EXPERT_TILE * EXPERT_TILE
    region_end = jnp.cumsum(region)
    region_start = region_end - region
    base = region_start[None, :] + before

    nb_max = _max_blocks(ntiles, sorted_rows, n_exp)
    dump0 = nb_max * EXPERT_TILE
    nch = sorted_rows // ROW_ALIGN
    q = jnp.arange(nch, dtype=jnp.int32) * ROW_ALIGN
    seg_end = off + seg_len
    inside = jnp.logical_and(off[:, None, :] <= q[None, :, None], q[None, :, None] < seg_end[:, None, :])
    valid = q[None, :] < tile_rows[:, None]
    row = q[None, :] + jnp.sum(jnp.where(inside, (base - off)[:, None, :], 0), axis=2)
    parity = (jnp.arange(ntiles, dtype=jnp.int32) % 2)[:, None]
    chunk_dst = jnp.where(valid, row, dump0 + parity * sorted_rows + q[None, :]).astype(jnp.int32)
    chunk_src = jnp.where(valid, row, 0).astype(jnp.int32)

    sizes = jnp.asarray(FILL_SIZES, dtype=jnp.int32)[None, :]
    gap = (region - filled)[:, None]
    used = (gap & sizes) != 0
    fill_row = filled[:, None] + (gap & ~(2 * sizes - 1))
    fill_dump = dump0 + 2 * sorted_rows + jnp.arange(n_exp, dtype=jnp.int32)[:, None] * EXPERT_TILE + (
        EXPERT_TILE - 2 * sizes)
    fill_dst = jnp.where(used, region_start[:, None] + fill_row, fill_dump).reshape(-1).astype(jnp.int32)

    blk = jnp.arange(nb_max, dtype=jnp.int32) * EXPERT_TILE
    block_e = jnp.sum((region_end[None, :] <= blk[:, None]).astype(jnp.int32), axis=1)
    block_e = jnp.minimum(block_e, n_exp - 1).astype(jnp.int32)
    n_used = (region_end[-1] // EXPERT_TILE).astype(jnp.int32).reshape(1)
    used = region > 0
    eidx = jnp.arange(n_exp, dtype=jnp.int32)
    rank = (jnp.cumsum(used) - used).astype(jnp.int32)
    later = jnp.where(jnp.logical_and(used[None, :], eidx[None, :] > eidx[:, None]), eidx[None, :], n_exp)
    nxt = jnp.min(later, axis=1)
    nxt = jnp.where(nxt == n_exp, -1, nxt)
    mine = block_e[:, None] == eidx[None, :]
    sched = jnp.stack([block_e, jnp.sum(jnp.where(mine, (rank % 2)[None, :], 0), axis=1),
                       jnp.sum(jnp.where(mine, nxt[None, :], 0), axis=1)]).astype(jnp.int32)
    return off, chunk_dst, chunk_src, fill_dst, sched, n_used


def _max_blocks(ntiles, sorted_rows, n_exp):
    return -(-(ntiles * sorted_rows + n_exp * EXPERT_TILE) // EXPERT_TILE)


def _sorted_positions(ei, off_col, tt):
    n_exp = off_col.shape[0]
    expert = lax.broadcasted_iota(jnp.int32, (n_exp, tt), 0)
    hits = [expert == ei[k:k + 1, :] for k in range(TOP_K)]
    chosen = sum(h.astype(F32) for h in hits).astype(BF16)
    t_row = lax.broadcasted_iota(jnp.int32, (tt, tt), 0)
    t_col = lax.broadcasted_iota(jnp.int32, (tt, tt), 1)
    earlier = (t_row < t_col).astype(BF16)
    rank = jnp.dot(chosen, earlier, preferred_element_type=F32) + off_col
    return [jnp.sum(jnp.where(h, rank, 0.0), axis=0, keepdims=True) for h in hits]


def _dispatch_kernel(cd_ref, fd_ref, h2_ref, ei_ref, off_ref, xb_ref, pos_ref, buf, zeros, perm_s, sem, fill_sem):
    step = pl.program_id(0)
    tt = h2_ref.shape[0]
    sorted_rows = buf.shape[1]
    nch = sorted_rows // ROW_ALIGN

    def fill_copy(e, s):
        size = FILL_SIZES[s]
        dst = pl.multiple_of(fd_ref[e * len(FILL_SIZES) + s], ROW_ALIGN)
        return pltpu.make_async_copy(zeros.at[pl.ds(0, size), :], xb_ref.at[pl.ds(dst, size), :], fill_sem)

    @pl.when(step == 0)
    def _():
        zeros[...] = jnp.zeros_like(zeros)
        perm_s[...] = jnp.zeros_like(perm_s)

        def start(e, c):
            for s in range(len(FILL_SIZES)):
                fill_copy(e, s).start()
            return c

        def wait(e, c):
            for s in range(len(FILL_SIZES)):
                fill_copy(e, s).wait()
            return c

        n_regions = fd_ref.shape[0] // len(FILL_SIZES)
        lax.fori_loop(0, n_regions, start, 0)
        lax.fori_loop(0, n_regions, wait, 0)

    slot = (step + 1) % 2
    last_step = pl.num_programs(0) - 1

    def chunk_copy(which, j):
        dst = pl.multiple_of(cd_ref[0, 0, j], ROW_ALIGN)
        return pltpu.make_async_copy(buf.at[which, pl.ds(j * ROW_ALIGN, ROW_ALIGN), :],
                                     xb_ref.at[pl.ds(dst, ROW_ALIGN), :], sem.at[which])

    def drain(which):
        for j in range(nch):
            chunk_copy(which, j).wait()

    @pl.when(step >= 3)
    def _():
        drain(slot)

    buf[slot] = _pack_bf16_pairs(jnp.dot(perm_s[...], h2_ref[...], preferred_element_type=F32))

    pos = _sorted_positions(ei_ref[0], off_ref[0][:, 0:1], tt)
    slot_id = lax.broadcasted_iota(jnp.int32, (SUBLANES, tt), 0)
    pos8 = jnp.zeros((SUBLANES, tt), F32)
    for k in range(TOP_K):
        pos8 = jnp.where(slot_id == k, pos[k], pos8)
    pos_ref[0] = pos8
    r = lax.broadcasted_iota(jnp.int32, (sorted_rows, tt), 0)
    hit = r == pos[0].astype(jnp.int32)
    for k in range(1, TOP_K):
        hit = jnp.logical_or(hit, r == pos[k].astype(jnp.int32))
    perm_s[...] = jnp.where(hit, 1.0, 0.0).astype(BF16)

    @pl.when(step >= 1)
    def _():
        for j in range(nch):
            chunk_copy(slot, j).start()

    @pl.when(step == last_step)
    def _():
        drain(slot)

    @pl.when(jnp.logical_and(step == last_step, step >= 2))
    def _():
        drain(1 - slot)


def _dispatch(h2, ei, off_col, chunk_dst, fill_dst, *, tm, sorted_rows, total_rows):
    t, d = h2.shape
    ntiles = t // tm
    n_exp = off_col.shape[1]
    nch = sorted_rows // ROW_ALIGN

    def cur(s):
        return jnp.minimum(s, ntiles - 1)

    def prev(s):
        return jnp.maximum(s - 1, 0)

    return pl.pallas_call(
        _dispatch_kernel,
        grid=(ntiles + 1,),
        in_specs=[
            pl.BlockSpec((1, 1, nch), lambda s: (prev(s), 0, 0), memory_space=pltpu.SMEM),
            pl.BlockSpec(memory_space=pltpu.SMEM),
            pl.BlockSpec((tm, d), lambda s: (prev(s), 0)),
            pl.BlockSpec((1, SUBLANES, tm), lambda s: (cur(s), 0, 0)),
            pl.BlockSpec((1, n_exp, LANES), lambda s: (cur(s), 0, 0)),
        ],
        out_specs=[
            pl.BlockSpec(memory_space=pl.ANY),
            pl.BlockSpec((1, SUBLANES, tm), lambda s: (cur(s), 0, 0)),
        ],
        out_shape=[
            jax.ShapeDtypeStruct((total_rows, d // 2), jnp.uint32),
            jax.ShapeDtypeStruct((ntiles, SUBLANES, tm), F32),
        ],
        scratch_shapes=[
            pltpu.VMEM((2, sorted_rows, d // 2), jnp.uint32),
            pltpu.VMEM((FILL_SIZES[0], d // 2), jnp.uint32),
            pltpu.VMEM((sorted_rows, tm), BF16),
            pltpu.SemaphoreType.DMA((2,)),
            pltpu.SemaphoreType.DMA(()),
        ],
        compiler_params=_params(1),
        name="dispatch",
    )(chunk_dst.reshape(ntiles, 1, nch), fill_dst, h2, ei, off_col)


def _experts_kernel(sched_ref, nu_ref, xb_ref, wu_hbm, bu_ref, wd_hbm, bd_ref, yb_ref,
                    wu_f, wd_f, wu_s, wd_s, sem, *, d_ff, layer):
    i = pl.program_id(0)
    expert, slot, nxt = sched_ref[0, i], sched_ref[1, i], sched_ref[2, i]
    fresh = jnp.logical_or(i == 0, expert != sched_ref[0, jnp.maximum(i - 1, 0)])

    def fetch(e, s):
        return (pltpu.make_async_copy(wu_hbm.at[layer, e], wu_f.at[s], sem.at[0, s]),
                pltpu.make_async_copy(wd_hbm.at[layer, e], wd_f.at[s], sem.at[1, s]))

    @pl.when(i == 0)
    def _():
        for c in fetch(expert, slot):
            c.start()

    @pl.when(jnp.logical_and(fresh, i < nu_ref[0]))
    def _():
        for c in fetch(expert, slot):
            c.wait()

        @pl.when(nxt >= 0)
        def _():
            for c in fetch(nxt, 1 - slot):
                c.start()

        wu_s[...] = wu_f[slot].astype(BF16)
        wd_s[...] = wd_f[slot].astype(BF16)

    @pl.when(i < nu_ref[0])
    def _():
        u = jnp.dot(_unpack_bf16_pairs(xb_ref[...]), wu_s[...], preferred_element_type=F32) + bu_ref[0, 0]
        gate = jnp.minimum(u[:, :d_ff], SWIGLU_LIMIT)
        lin = jnp.clip(u[:, d_ff:], -SWIGLU_LIMIT, SWIGLU_LIMIT)
        act = (gate * _sigmoid(SWIGLU_ALPHA * gate) * (lin + 1.0)).astype(BF16)
        y = jnp.dot(act, wd_s[...], preferred_element_type=F32) + bd_ref[0, 0]
        yb_ref[...] = _pack_bf16_pairs(y.astype(BF16).astype(F32))


def _experts(xb, sched, n_used, w_up, b_up, w_down, b_down, *, layer, nb_max):
    depth, n_exp, d, two_ff = w_up.shape
    d_ff = two_ff // 2
    dp = xb.shape[1]

    def blk(i, sc, nu):
        return jnp.minimum(i, nu[0] - 1)

    def exp(i, sc, nu):
        return sc[0, jnp.minimum(i, nu[0] - 1)]

    return pl.pallas_call(
        functools.partial(_experts_kernel, d_ff=d_ff, layer=layer),
        grid_spec=pltpu.PrefetchScalarGridSpec(
            num_scalar_prefetch=2,
            grid=(nb_max,),
            in_specs=[
                pl.BlockSpec((EXPERT_TILE, dp), lambda i, sc, nu: (blk(i, sc, nu), 0)),
                pl.BlockSpec(memory_space=pl.ANY),
                pl.BlockSpec((1, 1, 1, two_ff), lambda i, sc, nu: (layer, exp(i, sc, nu), 0, 0)),
                pl.BlockSpec(memory_space=pl.ANY),
                pl.BlockSpec((1, 1, 1, d), lambda i, sc, nu: (layer, exp(i, sc, nu), 0, 0)),
            ],
            out_specs=pl.BlockSpec((EXPERT_TILE, dp), lambda i, sc, nu: (blk(i, sc, nu), 0)),
            scratch_shapes=[
                pltpu.VMEM((2, d, two_ff), F32),
                pltpu.VMEM((2, d_ff, d), F32),
                pltpu.VMEM((d, two_ff), BF16),
                pltpu.VMEM((d_ff, d), BF16),
                pltpu.SemaphoreType.DMA((2, 2)),
            ],
        ),
        out_shape=jax.ShapeDtypeStruct((nb_max * EXPERT_TILE, dp), jnp.uint32),
        compiler_params=_params(1),
        name="experts",
    )(sched, n_used, xb, w_up, b_up.reshape(depth, n_exp, 1, two_ff), w_down, b_down.reshape(depth, n_exp, 1, d))


def _combine_kernel(cs_ref, csn_ref, x_ref, mod_ref, pos_ref, wt_ref, yb_ref, *rest, qk_width, follow):
    if follow == "inproj":
        mod2_ref, gain_ref, w_ref, cos_ref, sin_ref, o_ref, qkv_ref, rest_ref, buf, sem, x_prev = rest

        @pl.when(pl.program_id(0) == 0)
        def _():
            x_prev[...] = jnp.zeros_like(x_prev)

        _combine_fetch(cs_ref, csn_ref, yb_ref, buf, sem)()
        _inproj_tile(x_prev[...], mod2_ref, gain_ref, w_ref, cos_ref, sin_ref, qkv_ref, rest_ref, qk_width)
        x_new = _combine_tile(x_ref, mod_ref, pos_ref, wt_ref, buf)
        o_ref[...] = x_new
        x_prev[...] = x_new
    else:
        gain_ref, o_ref, buf, sem = rest
        wait = _combine_fetch(cs_ref, csn_ref, yb_ref, buf, sem)
        x_new = _combine_tile(x_ref, mod_ref, pos_ref, wt_ref, buf, wait)
        ms = jnp.mean(x_new * x_new, axis=-1, keepdims=True)
        o_ref[0] = x_new * lax.rsqrt(ms + EPS) * gain_ref[...]


def _combine_fetch(cs_ref, csn_ref, yb_ref, buf, sem):
    step = pl.program_id(0)
    cur = step % 2
    nch = buf.shape[1] // ROW_ALIGN

    def chunk_copy(table, slot, j):
        src = pl.multiple_of(table[0, 0, j], ROW_ALIGN)
        return pltpu.make_async_copy(yb_ref.at[pl.ds(src, ROW_ALIGN), :],
                                     buf.at[slot, pl.ds(j * ROW_ALIGN, ROW_ALIGN), :], sem.at[slot])

    @pl.when(step == 0)
    def _():
        for j in range(nch):
            chunk_copy(cs_ref, cur, j).start()

    @pl.when(step + 1 < pl.num_programs(0))
    def _():
        for j in range(nch):
            chunk_copy(csn_ref, 1 - cur, j).start()

    def wait():
        for j in range(nch):
            chunk_copy(cs_ref, cur, j).wait()

    return wait


def _combine_tile(x_ref, mod_ref, pos_ref, wt_ref, buf, wait=None):
    tt = x_ref.shape[0]
    sorted_rows = buf.shape[1]
    r = lax.broadcasted_iota(jnp.int32, (sorted_rows, tt), 0)
    pos = pos_ref[0].astype(jnp.int32)
    wt = wt_ref[0]
    sel = jnp.zeros((sorted_rows, tt), F32)
    for k in reversed(range(TOP_K)):
        sel = jnp.where(r == pos[k:k + 1, :], wt[k:k + 1, :], sel)
    if wait is not None:
        wait()
    y2 = lax.dot_general(sel.astype(BF16), _unpack_bf16_pairs(buf[pl.program_id(0) % 2]), TN_DIMS,
                         preferred_element_type=F32)
    return x_ref[...] + mod_ref[0, 5:6, :] * y2


def _combine(xs, mods, pos, wt, chunk_src, yb, *, tm, tpb, n_batch, sorted_rows, inproj=None, final=None):
    t, d = xs.shape
    ntiles = t // tm
    nch = sorted_rows // ROW_ALIGN

    def seg(i):
        return jnp.where(i % tpb == 0, n_batch, i // tpb)

    def cur(s):
        return jnp.minimum(s, ntiles - 1)

    def prev(s):
        return jnp.maximum(s - 1, 0)

    in_specs = [
        pl.BlockSpec((1, 1, nch), lambda s: (cur(s), 0, 0), memory_space=pltpu.SMEM),
        pl.BlockSpec((1, 1, nch), lambda s: (cur(s + 1), 0, 0), memory_space=pltpu.SMEM),
        pl.BlockSpec((tm, d), lambda s: (cur(s), 0)),
        pl.BlockSpec((1, SUBLANES, d), lambda s: (seg(cur(s)), 0, 0)),
        pl.BlockSpec((1, SUBLANES, tm), lambda s: (cur(s), 0, 0)),
        pl.BlockSpec((1, SUBLANES, tm), lambda s: (cur(s), 0, 0)),
        pl.BlockSpec(memory_space=pl.ANY),
    ]
    args = [chunk_src.reshape(ntiles, 1, nch), chunk_src.reshape(ntiles, 1, nch), xs, mods, pos, wt, yb]
    scratch = [pltpu.VMEM((2, sorted_rows, yb.shape[1]), jnp.uint32), pltpu.SemaphoreType.DMA((2,))]
    if inproj is not None:
        mods2, gain, w_bf16, cos_t, sin_t, qk_width, qkv_width = inproj
        in_width = w_bf16.shape[1]
        pos_blocks = cos_t.shape[0] // tm - 1

        def rope_blk(i):
            j = i % tpb
            return jnp.where(j == 0, pos_blocks, j - 1)

        in_specs += [
            pl.BlockSpec((1, SUBLANES, d), lambda s: (seg(prev(s)), 0, 0)),
            pl.BlockSpec((1, d), lambda s: (0, 0)),
            pl.BlockSpec((d, in_width), lambda s: (0, 0)),
            pl.BlockSpec((tm, RET_DK), lambda s: (rope_blk(prev(s)), 0)),
            pl.BlockSpec((tm, RET_DK), lambda s: (rope_blk(prev(s)), 0)),
        ]
        args += [mods2, gain, w_bf16, cos_t, sin_t]
        out_specs = [pl.BlockSpec((tm, d), lambda s: (cur(s), 0)),
                     pl.BlockSpec((tm, qkv_width), lambda s: (prev(s), 0)),
                     pl.BlockSpec((tm, in_width - qkv_width), lambda s: (prev(s), 0))]
        out_shape = [jax.ShapeDtypeStruct((t, d), F32), jax.ShapeDtypeStruct((t, qkv_width), BF16),
                     jax.ShapeDtypeStruct((t, in_width - qkv_width), BF16)]
        scratch += [pltpu.VMEM((tm, d), F32)]
        steps, follow = ntiles + 1, "inproj"
    else:
        gain, n = final
        qk_width = None
        in_specs += [pl.BlockSpec((1, d), lambda s: (0, 0))]
        args += [gain]
        out_specs = pl.BlockSpec((1, tm, d), lambda s: (s // tpb, jnp.maximum(s % tpb - 1, 0), 0))
        out_shape = jax.ShapeDtypeStruct((n_batch, n, d), F32)
        steps, follow = ntiles, "final"

    return pl.pallas_call(
        functools.partial(_combine_kernel, qk_width=qk_width, follow=follow),
        grid=(steps,),
        in_specs=in_specs,
        out_specs=out_specs,
        out_shape=out_shape,
        scratch_shapes=scratch,
        compiler_params=_params(1),
        name="combine",
    )(*args)


def _rope_tables(n, tm):
    t = jnp.arange(n, dtype=jnp.int32)
    m = RET_DK // 4
    freqs = ROPE_BASE ** (-jnp.arange(m, dtype=F32) / m)
    ang_r = (t // GRID_W).astype(F32)[:, None] * freqs[None, :]
    ang_c = (t % GRID_W).astype(F32)[:, None] * freqs[None, :]
    cos = jnp.concatenate([jnp.cos(ang_r)] * 2 + [jnp.cos(ang_c)] * 2, axis=1)
    sin = jnp.concatenate([-jnp.sin(ang_r), jnp.sin(ang_r), -jnp.sin(ang_c), jnp.sin(ang_c)], axis=1)
    cos = jnp.concatenate([cos, jnp.ones((tm, RET_DK), F32)], axis=0)
    sin = jnp.concatenate([sin, jnp.zeros((tm, RET_DK), F32)], axis=0)
    return cos, sin


def kernel(x, c, ctx, c_ctx, w_mod, b_mod, norm_mix, norm_ffn, w_in, ret_decay_f, ret_decay_b, conv_w, w_out,
           router_w, router_b, w_up, b_up, w_down, b_down, norm_final):
    n_batch, n, d = x.shape
    lc = ctx.shape[1]
    depth = w_mod.shape[0]
    tm = lc
    assert tm % 256 == 0 and n % tm == 0 and n % GRID_W == 0 and tm % GRID_W == 0
    assert n % RET_CHUNK == 0 and lc % RET_CHUNK == 0 and n_batch + 1 <= SUBLANES
    tpb = (lc + n) // tm
    qk_width = RET_HEADS * RET_DK
    qkv_width = 2 * qk_width + RET_HEADS * RET_DV
    n_exp = router_w.shape[2]
    sorted_rows = TOP_K * tm + n_exp * ROW_ALIGN
    ntiles = n_batch * tpb
    nb_max = _max_blocks(ntiles, sorted_rows, n_exp)
    total_rows = nb_max * EXPERT_TILE + 2 * sorted_rows + n_exp * EXPERT_TILE

    cond = jnp.zeros((SUBLANES, d), F32).at[:n_batch].set(c).at[n_batch].set(c_ctx)
    mod_all = _modulation(cond, w_mod, b_mod).reshape(depth, SUBLANES, 6, d)
    mod_all = jnp.pad(mod_all, ((0, 0), (0, 0), (0, SUBLANES - 6), (0, 0)))
    cos_t, sin_t = _rope_tables(n, tm)

    tile_args = dict(tm=tm, tpb=tpb, n_batch=n_batch)
    xs, qkv, rest = _inproj(ctx, x, mod_all[0], norm_mix[0][None, :], w_in[0].astype(BF16), cos_t, sin_t,
                            qk_width=qk_width, qkv_width=qkv_width, **tile_args)
    for layer in range(depth):
        mods = mod_all[layer]
        dec, dmat, gc = _decay_tables(ret_decay_f[layer], ret_decay_b[layer])
        y = _retention(qkv, dec, dmat, gc, n_batch=n_batch, lc=lc, n=n, qk_width=qk_width)
        conv_w8 = jnp.pad(conv_w[layer], ((0, SUBLANES - conv_w.shape[1]), (0, 0)))
        gain2 = norm_ffn[layer][None, :]
        rb_col = jnp.broadcast_to(router_b[layer][:, None], (n_exp, LANES))
        xs, h2, ei, wt, cnt = _mix(xs, y, rest, mods, conv_w8, w_out[layer].astype(BF16), gain2, router_w[layer].T,
                                   rb_col, **tile_args)
        off, chunk_dst, chunk_src, fill_dst, sched, n_used = _routing_plan(
            cnt[:, :, 0], tm=tm, sorted_rows=sorted_rows)
        off_col = jnp.broadcast_to(off.astype(F32)[:, :, None], (ntiles, n_exp, LANES))
        xb, pos = _dispatch(h2, ei, off_col, chunk_dst, fill_dst, tm=tm, sorted_rows=sorted_rows,
                            total_rows=total_rows)
        yb = _experts(xb, sched, n_used, w_up, b_up, w_down, b_down, layer=layer, nb_max=nb_max)
        if layer + 1 == depth:
            return _combine(xs, mods, pos, wt, chunk_src, yb, sorted_rows=sorted_rows,
                            final=(norm_final[None, :], n), **tile_args)
        nxt = layer + 1
        xs, qkv, rest = _combine(
            xs, mods, pos, wt, chunk_src, yb, sorted_rows=sorted_rows,
            inproj=(mod_all[nxt], norm_mix[nxt][None, :], w_in[nxt].astype(BF16), cos_t, sin_t, qk_width, qkv_width),
            **tile_args)
```

```python
import functools

import jax
import jax.numpy as jnp
from jax import lax
from jax.experimental import pallas as pl
from jax.experimental.pallas import tpu as pltpu

GRID_W = 64
RET_HEADS = 4
RET_DK = 128
RET_DV = 256
RET_CHUNK = 256
TOP_K = 4
SWIGLU_LIMIT = 7.0
SWIGLU_ALPHA = 1.702
ROPE_BASE = 10000.0
EPS = 1e-6

SUBLANES = 8
LANES = 128
VMEM_LIMIT = 56 * 1024 * 1024

EXPERT_TILE = 512
ROW_ALIGN = SUBLANES
FILL_SIZES = tuple(EXPERT_TILE >> k for k in range(1, EXPERT_TILE.bit_length()) if EXPERT_TILE >> k >= ROW_ALIGN)

F32 = jnp.float32
BF16 = jnp.bfloat16
NT_DIMS = (((1,), (1,)), ((), ()))
TN_DIMS = (((0,), (0,)), ((), ()))


def _params(n_axes=1):
    return pltpu.CompilerParams(dimension_semantics=("arbitrary",) * n_axes, vmem_limit_bytes=VMEM_LIMIT)


def _modulated_norm(x, gain, shift, scale):
    ms = jnp.mean(x * x, axis=-1, keepdims=True)
    return x * lax.rsqrt(ms + EPS) * gain * (1.0 + scale) + shift


def _sigmoid(z):
    return 1.0 / (1.0 + jnp.exp(-z))


def _split_bf16(a):
    hi = a.astype(BF16)
    lo = (a - hi.astype(F32)).astype(BF16)
    return hi, lo


def _pack_bf16_pairs(a):
    w = a.shape[1] // 2
    lo = lax.bitcast_convert_type(a[:, :w], jnp.uint32)
    hi = lax.bitcast_convert_type(a[:, w:], jnp.uint32)
    return (lo >> 16) | (hi & jnp.uint32(0xFFFF0000))


def _unpack_bf16_pairs(p):
    lo = lax.bitcast_convert_type(p << 16, F32)
    hi = lax.bitcast_convert_type(p & jnp.uint32(0xFFFF0000), F32)
    return jnp.concatenate([lo, hi], axis=1).astype(BF16)


def _modulation_kernel(c_ref, w_ref, b_ref, o_ref):
    c = c_ref[...]
    s = c * _sigmoid(c)
    s_hi, s_lo = _split_bf16(s)
    w_hi, w_lo = _split_bf16(w_ref[0])
    acc = jnp.dot(s_hi, w_hi, preferred_element_type=F32)
    acc += jnp.dot(s_hi, w_lo, preferred_element_type=F32)
    acc += jnp.dot(s_lo, w_hi, preferred_element_type=F32)
    o_ref[0] = acc + b_ref[0]


def _modulation(cond, w_mod, b_mod):
    depth, d, six_d = w_mod.shape
    nblk = six_d // d
    return pl.pallas_call(
        _modulation_kernel,
        grid=(depth, nblk),
        in_specs=[
            pl.BlockSpec((SUBLANES, d), lambda l, j: (0, 0)),
            pl.BlockSpec((1, d, d), lambda l, j: (l, 0, j)),
            pl.BlockSpec((1, 1, d), lambda l, j: (l, 0, j)),
        ],
        out_specs=pl.BlockSpec((1, SUBLANES, d), lambda l, j: (l, 0, j)),
        out_shape=jax.ShapeDtypeStruct((depth, SUBLANES, six_d), F32),
        compiler_params=_params(2),
        name="modulation",
    )(cond, w_mod, b_mod.reshape(depth, 1, six_d))


def _inproj_kernel(ctx_ref, x_ref, mod_ref, gain_ref, w_ref, cos_ref, sin_ref, xs_ref, qkv_ref, rest_ref, *,
                   qk_width, tpb):
    is_ctx = (pl.program_id(0) % tpb) == 0
    x = jnp.where(is_ctx, ctx_ref[0], x_ref[0])
    xs_ref[...] = x
    _inproj_tile(x, mod_ref, gain_ref, w_ref, cos_ref, sin_ref, qkv_ref, rest_ref, qk_width)


def _inproj_tile(x, mod_ref, gain_ref, w_ref, cos_ref, sin_ref, qkv_ref, rest_ref, qk_width):
    q_scale = RET_DK ** -0.5
    h = _modulated_norm(x, gain_ref[...], mod_ref[0, 0:1, :], mod_ref[0, 1:2, :]).astype(BF16)
    reps = qk_width // RET_DK
    cos = jnp.concatenate([cos_ref[...]] * reps, axis=1)
    sin = jnp.concatenate([sin_ref[...]] * reps, axis=1)
    lane = lax.broadcasted_iota(jnp.int32, cos.shape, 1)
    first_half = (lane % (RET_DK // 2)) < (RET_DK // 4)
    for part, scale in ((0, q_scale), (1, 1.0)):
        c0 = part * qk_width
        p = jnp.dot(h, w_ref[:, c0:c0 + qk_width], preferred_element_type=F32)
        partner = jnp.where(first_half, pltpu.roll(p, qk_width - RET_DK // 4, 1), pltpu.roll(p, RET_DK // 4, 1))
        r = p * cos + partner * sin
        if scale != 1.0:
            r = r * scale
        qkv_ref[:, c0:c0 + qk_width] = r.astype(BF16)
    qkv_width = qkv_ref.shape[1]
    step = 512
    for c0 in range(2 * qk_width, qkv_width, step):
        qkv_ref[:, c0:c0 + step] = jnp.dot(h, w_ref[:, c0:c0 + step], preferred_element_type=F32).astype(BF16)
    for c0 in range(0, rest_ref.shape[1], step):
        rest_ref[:, c0:c0 + step] = jnp.dot(
            h, w_ref[:, qkv_width + c0:qkv_width + c0 + step], preferred_element_type=F32).astype(BF16)


def _inproj(ctx, x, mods, gain, w_bf16, cos_t, sin_t, *, tm, tpb, n_batch, qk_width, qkv_width):
    d = x.shape[2]
    t = n_batch * tpb * tm
    in_width = w_bf16.shape[1]
    rest_width = in_width - qkv_width
    pos_blocks = cos_t.shape[0] // tm - 1

    def seg(i):
        return jnp.where(i % tpb == 0, n_batch, i // tpb)

    def pos(i):
        j = i % tpb
        return jnp.where(j == 0, pos_blocks, j - 1)

    return pl.pallas_call(
        functools.partial(_inproj_kernel, qk_width=qk_width, tpb=tpb),
        grid=(t // tm,),
        in_specs=[
            pl.BlockSpec((1, tm, d), lambda i: (i // tpb, 0, 0)),
            pl.BlockSpec((1, tm, d), lambda i: (i // tpb, jnp.maximum(i % tpb - 1, 0), 0)),
            pl.BlockSpec((1, SUBLANES, d), lambda i: (seg(i), 0, 0)),
            pl.BlockSpec((1, d), lambda i: (0, 0)),
            pl.BlockSpec((d, in_width), lambda i: (0, 0)),
            pl.BlockSpec((tm, RET_DK), lambda i: (pos(i), 0)),
            pl.BlockSpec((tm, RET_DK), lambda i: (pos(i), 0)),
        ],
        out_specs=[
            pl.BlockSpec((tm, d), lambda i: (i, 0)),
            pl.BlockSpec((tm, qkv_width), lambda i: (i, 0)),
            pl.BlockSpec((tm, rest_width), lambda i: (i, 0)),
        ],
        out_shape=[
            jax.ShapeDtypeStruct((t, d), F32),
            jax.ShapeDtypeStruct((t, qkv_width), BF16),
            jax.ShapeDtypeStruct((t, rest_width), BF16),
        ],
        compiler_params=_params(1),
        name="inproj",
    )(ctx, x, mods, gain, w_bf16, cos_t, sin_t)


def _retention_kernel(gc_ref, q_ref, k_ref, v_ref, dec_ref, dmat_ref, y_ref, sf_ref, sb_ref, st_ref, *, lc, n):
    head = pl.program_id(1)
    c = RET_CHUNK
    qd = dec_ref[0, :, 0:2 * RET_DK]
    kdf = dec_ref[0, :, 2 * RET_DK:3 * RET_DK]
    kdb = dec_ref[0, :, 3 * RET_DK:4 * RET_DK]
    gcf = gc_ref[head, 0]
    gcb = gc_ref[head, 1]

    def rows(chunk):
        return pl.ds(pl.multiple_of(chunk * c, c), c)

    def advance(s_ref, chunk, kd, gc):
        kdec = (k_ref[rows(chunk), :].astype(F32) * kd).astype(BF16)
        kv = lax.dot_general(kdec, v_ref[rows(chunk), :], TN_DIMS, preferred_element_type=F32)
        s_ref[...] = gc * s_ref[...] + kv

    def sweep(chunk0, nchunks):
        def states(i, carry):
            cf = chunk0 + i
            cb = chunk0 + nchunks - 1 - i
            st_ref[cf, 0:RET_DK, :] = sf_ref[...].astype(BF16)
            st_ref[cb, RET_DK:2 * RET_DK, :] = sb_ref[...].astype(BF16)
            advance(sf_ref, cf, kdf, gcf)
            advance(sb_ref, cb, kdb, gcb)
            return carry

        def outputs(i, carry):
            r = rows(chunk0 + i)
            q, k, v = q_ref[r, :], k_ref[r, :], v_ref[r, :]
            scores = lax.dot_general(q, k, NT_DIMS, preferred_element_type=F32)
            qdec = (jnp.concatenate([q, q], axis=1).astype(F32) * qd).astype(BF16)
            y = jnp.dot((scores * dmat_ref[0]).astype(BF16), v, preferred_element_type=F32)
            y += jnp.dot(qdec, st_ref[chunk0 + i], preferred_element_type=F32)
            y_ref[r, :] = (y * lax.rsqrt(jnp.mean(y * y, axis=-1, keepdims=True) + EPS)).astype(BF16)
            return carry

        lax.fori_loop(0, nchunks, states, 0, unroll=4 if nchunks % 4 == 0 else 1)
        lax.fori_loop(0, nchunks, outputs, 0, unroll=8 if nchunks % 8 == 0 else 1)

    sf_ref[...] = jnp.zeros_like(sf_ref)
    sb_ref[...] = jnp.zeros_like(sb_ref)
    sweep(0, lc // c)
    sweep(lc // c, n // c)


def _retention(qkv, dec, dmat, gc, *, n_batch, lc, n, qk_width):
    s = lc + n
    t = qkv.shape[0]
    k_blk0 = qk_width // RET_DK
    v_blk0 = 2 * qk_width // RET_DV
    return pl.pallas_call(
        functools.partial(_retention_kernel, lc=lc, n=n),
        grid_spec=pltpu.PrefetchScalarGridSpec(
            num_scalar_prefetch=1,
            grid=(n_batch, RET_HEADS),
            in_specs=[
                pl.BlockSpec((s, RET_DK), lambda b, h, gc: (b, h)),
                pl.BlockSpec((s, RET_DK), lambda b, h, gc: (b, k_blk0 + h)),
                pl.BlockSpec((s, RET_DV), lambda b, h, gc: (b, v_blk0 + h)),
                pl.BlockSpec((1, RET_CHUNK, 4 * RET_DK), lambda b, h, gc: (h, 0, 0)),
                pl.BlockSpec((1, RET_CHUNK, RET_CHUNK), lambda b, h, gc: (h, 0, 0)),
            ],
            out_specs=pl.BlockSpec((s, RET_DV), lambda b, h, gc: (b, h)),
            scratch_shapes=[
                pltpu.VMEM((RET_DK, RET_DV), F32),
                pltpu.VMEM((RET_DK, RET_DV), F32),
                pltpu.VMEM((s // RET_CHUNK, 2 * RET_DK, RET_DV), BF16),
            ],
        ),
        out_shape=jax.ShapeDtypeStruct((t, RET_HEADS * RET_DV), BF16),
        compiler_params=_params(2),
        name="retention",
    )(gc, qkv, qkv, qkv, dec, dmat)


def _decay_tables(logit_f, logit_b):
    c = RET_CHUNK
    lg_f = jax.nn.log_sigmoid(logit_f.astype(F32))[:, None]
    lg_b = jax.nn.log_sigmoid(logit_b.astype(F32))[:, None]
    i = jnp.arange(c, dtype=F32)[None, :]
    vecs = jnp.stack([
        jnp.exp((i + 1) * lg_f), jnp.exp((c - i) * lg_b),
        jnp.exp((c - 1 - i) * lg_f), jnp.exp(i * lg_b)], axis=2)
    dec = jnp.broadcast_to(vecs[..., None], vecs.shape + (RET_DK,)).reshape(vecs.shape[0], c, 4 * RET_DK)
    diff = i[0][:, None] - i[0][None, :]
    lower = jnp.where(diff >= 0, jnp.exp(jnp.where(diff >= 0, diff, 0.0)[None] * lg_f[:, :, None]), 0.0)
    upper = jnp.where(diff < 0, jnp.exp(jnp.where(diff < 0, -diff, 0.0)[None] * lg_b[:, :, None]), 0.0)
    gc = jnp.concatenate([jnp.exp(c * lg_f), jnp.exp(c * lg_b)], axis=1)
    return dec, lower + upper, gc


def _mix_kernel(x_ref, y_ref, rest_ref, mod_ref, cw_ref, wo_ref, gain2_ref, rw_ref, rb_ref,
                o_ref, h2_ref, ei_ref, wt_ref, cnt_ref, h_prev, *, tpb, ntiles, ret_width, conv_width):
    step = pl.program_id(0)

    @pl.when(step == 0)
    def _():
        h_prev[...] = jnp.zeros_like(h_prev)

    _route(h_prev[...], rw_ref, rb_ref, ei_ref, wt_ref, cnt_ref)
    is_ctx = (jnp.minimum(step, ntiles - 1) % tpb) == 0
    tm = x_ref.shape[0]
    g = rest_ref[:, 0:ret_width].astype(F32)
    cb = rest_ref[:, ret_width:ret_width + conv_width].astype(F32)
    cc = rest_ref[:, ret_width + conv_width:ret_width + 2 * conv_width].astype(F32)
    cx = rest_ref[:, ret_width + 2 * conv_width:ret_width + 3 * conv_width].astype(F32)
    ret = (g * _sigmoid(g) * y_ref[...].astype(F32)).astype(BF16)
    u = cc * cx
    row = lax.broadcasted_iota(jnp.int32, u.shape, 0)
    pos = jnp.where(is_ctx, row, row % GRID_W)
    last = jnp.where(is_ctx, tm - 1, GRID_W - 1)
    u_prev = jnp.where(pos == 0, 0.0, pltpu.roll(u, 1, 0))
    u_next = jnp.where(pos == last, 0.0, pltpu.roll(u, tm - 1, 0))
    conv = (cb * (u_prev * cw_ref[0:1, :] + u * cw_ref[1:2, :] + u_next * cw_ref[2:3, :])).astype(BF16)
    mix = jnp.dot(ret, wo_ref[0:ret_width, :], preferred_element_type=F32)
    mix += jnp.dot(conv, wo_ref[ret_width:ret_width + conv_width, :], preferred_element_type=F32)
    x_new = x_ref[...] + mod_ref[0, 2:3, :] * mix
    o_ref[...] = x_new
    h2 = _modulated_norm(x_new, gain2_ref[...], mod_ref[0, 3:4, :], mod_ref[0, 4:5, :])
    h2_ref[...] = h2.astype(BF16)
    h_prev[...] = h2


def _mix(xs, y, rest, mods, conv_w8, wo_bf16, gain2, rw_t, rb_col, *, tm, tpb, n_batch):
    t, d = xs.shape
    ntiles = t // tm
    ret_width = y.shape[1]
    conv_width = conv_w8.shape[1]
    n_exp = rw_t.shape[0]

    def cur(s):
        return jnp.minimum(s, ntiles - 1)

    def prev(s):
        return jnp.maximum(s - 1, 0)

    def seg(i):
        return jnp.where(i % tpb == 0, n_batch, i // tpb)

    return pl.pallas_call(
        functools.partial(_mix_kernel, tpb=tpb, ntiles=ntiles, ret_width=ret_width, conv_width=conv_width),
        grid=(ntiles + 1,),
        in_specs=[
            pl.BlockSpec((tm, d), lambda s: (cur(s), 0)),
            pl.BlockSpec((tm, ret_width), lambda s: (cur(s), 0)),
            pl.BlockSpec((tm, rest.shape[1]), lambda s: (cur(s), 0)),
            pl.BlockSpec((1, SUBLANES, d), lambda s: (seg(cur(s)), 0, 0)),
            pl.BlockSpec((SUBLANES, conv_width), lambda s: (0, 0)),
            pl.BlockSpec(wo_bf16.shape, lambda s: (0, 0)),
            pl.BlockSpec((1, d), lambda s: (0, 0)),
            pl.BlockSpec((n_exp, d), lambda s: (0, 0)),
            pl.BlockSpec((n_exp, LANES), lambda s: (0, 0)),
        ],
        out_specs=[
            pl.BlockSpec((tm, d), lambda s: (cur(s), 0)),
            pl.BlockSpec((tm, d), lambda s: (cur(s), 0)),
            pl.BlockSpec((1, SUBLANES, tm), lambda s: (prev(s), 0, 0)),
            pl.BlockSpec((1, SUBLANES, tm), lambda s: (prev(s), 0, 0)),
            pl.BlockSpec((1, n_exp, LANES), lambda s: (prev(s), 0, 0)),
        ],
        out_shape=[
            jax.ShapeDtypeStruct((t, d), F32),
            jax.ShapeDtypeStruct((t, d), BF16),
            jax.ShapeDtypeStruct((ntiles, SUBLANES, tm), jnp.int32),
            jax.ShapeDtypeStruct((ntiles, SUBLANES, tm), F32),
            jax.ShapeDtypeStruct((ntiles, n_exp, LANES), jnp.int32),
        ],
        scratch_shapes=[pltpu.VMEM((tm, d), F32)],
        compiler_params=_params(1),
        name="mix",
    )(xs, y, rest, mods, conv_w8, wo_bf16, gain2, rw_t, rb_col)


def _route(h2, rw_ref, rb_ref, ei_ref, wt_ref, cnt_ref):
    h_hi, h_lo = _split_bf16(h2)
    w_hi, w_lo = _split_bf16(rw_ref[...])
    logits = lax.dot_general(w_hi, h_hi, NT_DIMS, preferred_element_type=F32)
    logits += lax.dot_general(w_hi, h_lo, NT_DIMS, preferred_element_type=F32)
    logits += lax.dot_general(w_lo, h_hi, NT_DIMS, preferred_element_type=F32)
    logits += rb_ref[:, 0:1]
    n_exp, tt = logits.shape
    expert = lax.broadcasted_iota(jnp.int32, (n_exp, tt), 0).astype(F32)
    slot = lax.broadcasted_iota(jnp.int32, (SUBLANES, tt), 0)
    ei = jnp.zeros((SUBLANES, tt), F32)
    ev = jnp.zeros((SUBLANES, tt), F32)
    taken = jnp.zeros((n_exp, tt), F32)
    top = None
    denom = jnp.zeros((1, tt), F32)
    for k in range(TOP_K):
        m = jnp.max(logits, axis=0, keepdims=True)
        idx = jnp.min(jnp.where(logits == m, expert, float(n_exp)), axis=0, keepdims=True)
        hit = expert == idx
        taken += hit.astype(F32)
        logits = jnp.where(hit, -jnp.inf, logits)
        if k == 0:
            top = m
        e = jnp.exp(m - top)
        denom += e
        ei = jnp.where(slot == k, idx, ei)
        ev = jnp.where(slot == k, e, ev)
    ei_ref[0] = ei.astype(jnp.int32)
    wt_ref[0] = jnp.where(slot < TOP_K, ev / denom, 0.0)
    cnt = jnp.sum(taken, axis=1, keepdims=True)
    cnt_ref[0] = jnp.broadcast_to(cnt, (n_exp, LANES)).astype(jnp.int32)


def _routing_plan(cnt, *, tm, sorted_rows):
    ntiles, n_exp = cnt.shape
    seg_len = (cnt + ROW_ALIGN - 1) // ROW_ALIGN * ROW_ALIGN
    off = jnp.cumsum(seg_len, axis=1) - seg_len
    tile_rows = jnp.sum(seg_len, axis=1)
    before = jnp.cumsum(seg_len, axis=0) - seg_len
    filled = jnp.sum(seg_len, axis=0)
    region = (filled + EXPERT_TILE - 1) // EXPERT_TILE * EXPERT_TILE
    region_end = jnp.cumsum(region)
    region_start = region_end - region
    base = region_start[None, :] + before

    nb_max = _max_blocks(ntiles, sorted_rows, n_exp)
    dump0 = nb_max * EXPERT_TILE
    nch = sorted_rows // ROW_ALIGN
    q = jnp.arange(nch, dtype=jnp.int32) * ROW_ALIGN
    seg_end = off + seg_len
    inside = jnp.logical_and(off[:, None, :] <= q[None, :, None], q[None, :, None] < seg_end[:, None, :])
    valid = q[None, :] < tile_rows[:, None]
    row = q[None, :] + jnp.sum(jnp.where(inside, (base - off)[:, None, :], 0), axis=2)
    parity = (jnp.arange(ntiles, dtype=jnp.int32) % 2)[:, None]
    chunk_dst = jnp.where(valid, row, dump0 + parity * sorted_rows + q[None, :]).astype(jnp.int32)
    chunk_src = jnp.where(valid, row, 0).astype(jnp.int32)

    sizes = jnp.asarray(FILL_SIZES, dtype=jnp.int32)[None, :]
    gap = (region - filled)[:, None]
    used = (gap & sizes) != 0
    fill_row = filled[:, None] + (gap & ~(2 * sizes - 1))
    fill_dump = dump0 + 2 * sorted_rows + jnp.arange(n_exp, dtype=jnp.int32)[:, None] * EXPERT_TILE + (
        EXPERT_TILE - 2 * sizes)
    fill_dst = jnp.where(used, region_start[:, None] + fill_row, fill_dump).reshape(-1).astype(jnp.int32)

    blk = jnp.arange(nb_max, dtype=jnp.int32) * EXPERT_TILE
    block_e = jnp.sum((region_end[None, :] <= blk[:, None]).astype(jnp.int32), axis=1)
    block_e = jnp.minimum(block_e, n_exp - 1).astype(jnp.int32)
    n_used = (region_end[-1] // EXPERT_TILE).astype(jnp.int32).reshape(1)
    used = region > 0
    eidx = jnp.arange(n_exp, dtype=jnp.int32)
    rank = (jnp.cumsum(used) - used).astype(jnp.int32)
    later = jnp.where(jnp.logical_and(used[None, :], eidx[None, :] > eidx[:, None]), eidx[None, :], n_exp)
    nxt = jnp.min(later, axis=1)
    nxt = jnp.where(nxt == n_exp, -1, nxt)
    mine = block_e[:, None] == eidx[None, :]
    data_end = jnp.sum(jnp.where(mine, (region_start + filled)[None, :], 0), axis=1)
    sched = jnp.stack([block_e, jnp.sum(jnp.where(mine, (rank % 2)[None, :], 0), axis=1),
                       jnp.sum(jnp.where(mine, nxt[None, :], 0), axis=1),
                       jnp.clip(data_end - blk, 0, EXPERT_TILE)]).astype(jnp.int32)
    return off, chunk_dst, chunk_src, fill_dst, sched, n_used


def _max_blocks(ntiles, sorted_rows, n_exp):
    return -(-(ntiles * sorted_rows + n_exp * EXPERT_TILE) // EXPERT_TILE)


def _sorted_positions(ei, off_col, tt):
    n_exp = off_col.shape[0]
    expert = lax.broadcasted_iota(jnp.int32, (n_exp, tt), 0)
    hits = [expert == ei[k:k + 1, :] for k in range(TOP_K)]
    chosen = sum(h.astype(F32) for h in hits).astype(BF16)
    t_row = lax.broadcasted_iota(jnp.int32, (tt, tt), 0)
    t_col = lax.broadcasted_iota(jnp.int32, (tt, tt), 1)
    earlier = (t_row < t_col).astype(BF16)
    rank = jnp.dot(chosen, earlier, preferred_element_type=F32) + off_col
    return [jnp.sum(jnp.where(h, rank, 0.0), axis=0, keepdims=True) for h in hits]


def _dispatch_kernel(cd_ref, fd_ref, h2_ref, ei_ref, off_ref, xb_ref, pos_ref, buf, zeros, perm_s, sem, fill_sem):
    step = pl.program_id(0)
    tt = h2_ref.shape[0]
    sorted_rows = buf.shape[1]
    nch = sorted_rows // ROW_ALIGN

    def fill_copy(e, s):
        size = FILL_SIZES[s]
        dst = pl.multiple_of(fd_ref[e * len(FILL_SIZES) + s], ROW_ALIGN)
        return pltpu.make_async_copy(zeros.at[pl.ds(0, size), :], xb_ref.at[pl.ds(dst, size), :], fill_sem)

    @pl.when(step == 0)
    def _():
        zeros[...] = jnp.zeros_like(zeros)
        perm_s[...] = jnp.zeros_like(perm_s)

        def start(e, c):
            for s in range(len(FILL_SIZES)):
                fill_copy(e, s).start()
            return c

        def wait(e, c):
            for s in range(len(FILL_SIZES)):
                fill_copy(e, s).wait()
            return c

        n_regions = fd_ref.shape[0] // len(FILL_SIZES)
        lax.fori_loop(0, n_regions, start, 0)
        lax.fori_loop(0, n_regions, wait, 0)

    slot = (step + 1) % 2
    last_step = pl.num_programs(0) - 1

    def chunk_copy(which, j):
        dst = pl.multiple_of(cd_ref[0, 0, j], ROW_ALIGN)
        return pltpu.make_async_copy(buf.at[which, pl.ds(j * ROW_ALIGN, ROW_ALIGN), :],
                                     xb_ref.at[pl.ds(dst, ROW_ALIGN), :], sem.at[which])

    def drain(which):
        for j in range(nch):
            chunk_copy(which, j).wait()

    @pl.when(step >= 3)
    def _():
        drain(slot)

    buf[slot] = _pack_bf16_pairs(jnp.dot(perm_s[...], h2_ref[...], preferred_element_type=F32))

    pos = _sorted_positions(ei_ref[0], off_ref[0][:, 0:1], tt)
    slot_id = lax.broadcasted_iota(jnp.int32, (SUBLANES, tt), 0)
    pos8 = jnp.zeros((SUBLANES, tt), F32)
    for k in range(TOP_K):
        pos8 = jnp.where(slot_id == k, pos[k], pos8)
    pos_ref[0] = pos8
    r = lax.broadcasted_iota(jnp.int32, (sorted_rows, tt), 0)
    hit = r == pos[0].astype(jnp.int32)
    for k in range(1, TOP_K):
        hit = jnp.logical_or(hit, r == pos[k].astype(jnp.int32))
    perm_s[...] = jnp.where(hit, 1.0, 0.0).astype(BF16)

    @pl.when(step >= 1)
    def _():
        for j in range(nch):
            chunk_copy(slot, j).start()

    @pl.when(step == last_step)
    def _():
        drain(slot)

    @pl.when(jnp.logical_and(step == last_step, step >= 2))
    def _():
        drain(1 - slot)


def _dispatch(h2, ei, off_col, chunk_dst, fill_dst, *, tm, sorted_rows, total_rows):
    t, d = h2.shape
    ntiles = t // tm
    n_exp = off_col.shape[1]
    nch = sorted_rows // ROW_ALIGN

    def cur(s):
        return jnp.minimum(s, ntiles - 1)

    def prev(s):
        return jnp.maximum(s - 1, 0)

    return pl.pallas_call(
        _dispatch_kernel,
        grid=(ntiles + 1,),
        in_specs=[
            pl.BlockSpec((1, 1, nch), lambda s: (prev(s), 0, 0), memory_space=pltpu.SMEM),
            pl.BlockSpec(memory_space=pltpu.SMEM),
            pl.BlockSpec((tm, d), lambda s: (prev(s), 0)),
            pl.BlockSpec((1, SUBLANES, tm), lambda s: (cur(s), 0, 0)),
            pl.BlockSpec((1, n_exp, LANES), lambda s: (cur(s), 0, 0)),
        ],
        out_specs=[
            pl.BlockSpec(memory_space=pl.ANY),
            pl.BlockSpec((1, SUBLANES, tm), lambda s: (cur(s), 0, 0)),
        ],
        out_shape=[
            jax.ShapeDtypeStruct((total_rows, d // 2), jnp.uint32),
            jax.ShapeDtypeStruct((ntiles, SUBLANES, tm), F32),
        ],
        scratch_shapes=[
            pltpu.VMEM((2, sorted_rows, d // 2), jnp.uint32),
            pltpu.VMEM((FILL_SIZES[0], d // 2), jnp.uint32),
            pltpu.VMEM((sorted_rows, tm), BF16),
            pltpu.SemaphoreType.DMA((2,)),
            pltpu.SemaphoreType.DMA(()),
        ],
        compiler_params=_params(1),
        name="dispatch",
    )(chunk_dst.reshape(ntiles, 1, nch), fill_dst, h2, ei, off_col)


def _experts_kernel(sched_ref, nu_ref, xb_ref, wu_hbm, bu_ref, wd_hbm, bd_ref, yb_ref,
                    wu_f, wd_f, wu_s, wd_s, sem, *, d_ff, layer):
    i = pl.program_id(0)
    expert, slot, nxt = sched_ref[0, i], sched_ref[1, i], sched_ref[2, i]
    fresh = jnp.logical_or(i == 0, expert != sched_ref[0, jnp.maximum(i - 1, 0)])

    def fetch(e, s):
        return (pltpu.make_async_copy(wu_hbm.at[layer, e], wu_f.at[s], sem.at[0, s]),
                pltpu.make_async_copy(wd_hbm.at[layer, e], wd_f.at[s], sem.at[1, s]))

    @pl.when(i == 0)
    def _():
        for c in fetch(expert, slot):
            c.start()

    @pl.when(jnp.logical_and(fresh, i < nu_ref[0]))
    def _():
        for c in fetch(expert, slot):
            c.wait()

        @pl.when(nxt >= 0)
        def _():
            for c in fetch(nxt, 1 - slot):
                c.start()

        wu_s[...] = wu_f[slot].astype(BF16)
        wd_s[...] = wd_f[slot].astype(BF16)

    def mlp(rows):
        u = jnp.dot(_unpack_bf16_pairs(xb_ref[0:rows, :]), wu_s[...], preferred_element_type=F32) + bu_ref[0, 0]
        gate = jnp.minimum(u[:, :d_ff], SWIGLU_LIMIT)
        lin = jnp.clip(u[:, d_ff:], -SWIGLU_LIMIT, SWIGLU_LIMIT)
        act = (gate * _sigmoid(SWIGLU_ALPHA * gate) * (lin + 1.0)).astype(BF16)
        y = jnp.dot(act, wd_s[...], preferred_element_type=F32) + bd_ref[0, 0]
        yb_ref[0:rows, :] = _pack_bf16_pairs(y.astype(BF16).astype(F32))

    rows_used = sched_ref[3, i]
    half = xb_ref.shape[0] // 2

    @pl.when(jnp.logical_and(i < nu_ref[0], rows_used > half))
    def _():
        mlp(xb_ref.shape[0])

    @pl.when(jnp.logical_and(i < nu_ref[0], rows_used <= half))
    def _():
        mlp(half)


def _experts(xb, sched, n_used, w_up, b_up, w_down, b_down, *, layer, nb_max):
    depth, n_exp, d, two_ff = w_up.shape
    d_ff = two_ff // 2
    dp = xb.shape[1]

    def blk(i, sc, nu):
        return jnp.minimum(i, nu[0] - 1)

    def exp(i, sc, nu):
        return sc[0, jnp.minimum(i, nu[0] - 1)]

    return pl.pallas_call(
        functools.partial(_experts_kernel, d_ff=d_ff, layer=layer),
        grid_spec=pltpu.PrefetchScalarGridSpec(
            num_scalar_prefetch=2,
            grid=(nb_max,),
            in_specs=[
                pl.BlockSpec((EXPERT_TILE, dp), lambda i, sc, nu: (blk(i, sc, nu), 0)),
                pl.BlockSpec(memory_space=pl.ANY),
                pl.BlockSpec((1, 1, 1, two_ff), lambda i, sc, nu: (layer, exp(i, sc, nu), 0, 0)),
                pl.BlockSpec(memory_space=pl.ANY),
                pl.BlockSpec((1, 1, 1, d), lambda i, sc, nu: (layer, exp(i, sc, nu), 0, 0)),
            ],
            out_specs=pl.BlockSpec((EXPERT_TILE, dp), lambda i, sc, nu: (blk(i, sc, nu), 0)),
            scratch_shapes=[
                pltpu.VMEM((2, d, two_ff), F32),
                pltpu.VMEM((2, d_ff, d), F32),
                pltpu.VMEM((d, two_ff), BF16),
                pltpu.VMEM((d_ff, d), BF16),
                pltpu.SemaphoreType.DMA((2, 2)),
            ],
        ),
        out_shape=jax.ShapeDtypeStruct((nb_max * EXPERT_TILE, dp), jnp.uint32),
        compiler_params=_params(1),
        name="experts",
    )(sched, n_used, xb, w_up, b_up.reshape(depth, n_exp, 1, two_ff), w_down, b_down.reshape(depth, n_exp, 1, d))


def _combine_kernel(cs_ref, csn_ref, x_ref, mod_ref, pos_ref, wt_ref, yb_ref, *rest, qk_width, follow):
    if follow == "inproj":
        mod2_ref, gain_ref, w_ref, cos_ref, sin_ref, o_ref, qkv_ref, rest_ref, buf, sem, x_prev = rest

        @pl.when(pl.program_id(0) == 0)
        def _():
            x_prev[...] = jnp.zeros_like(x_prev)

        _combine_fetch(cs_ref, csn_ref, yb_ref, buf, sem)()
        _inproj_tile(x_prev[...], mod2_ref, gain_ref, w_ref, cos_ref, sin_ref, qkv_ref, rest_ref, qk_width)
        x_new = _combine_tile(x_ref, mod_ref, pos_ref, wt_ref, buf)
        o_ref[...] = x_new
        x_prev[...] = x_new
    else:
        gain_ref, o_ref, buf, sem = rest
        wait = _combine_fetch(cs_ref, csn_ref, yb_ref, buf, sem)
        x_new = _combine_tile(x_ref, mod_ref, pos_ref, wt_ref, buf, wait)
        ms = jnp.mean(x_new * x_new, axis=-1, keepdims=True)
        o_ref[0] = x_new * lax.rsqrt(ms + EPS) * gain_ref[...]


def _combine_fetch(cs_ref, csn_ref, yb_ref, buf, sem):
    step = pl.program_id(0)
    cur = step % 2
    nch = buf.shape[1] // ROW_ALIGN

    def chunk_copy(table, slot, j):
        src = pl.multiple_of(table[0, 0, j], ROW_ALIGN)
        return pltpu.make_async_copy(yb_ref.at[pl.ds(src, ROW_ALIGN), :],
                                     buf.at[slot, pl.ds(j * ROW_ALIGN, ROW_ALIGN), :], sem.at[slot])

    @pl.when(step == 0)
    def _():
        for j in range(nch):
            chunk_copy(cs_ref, cur, j).start()

    @pl.when(step + 1 < pl.num_programs(0))
    def _():
        for j in range(nch):
            chunk_copy(csn_ref, 1 - cur, j).start()

    def wait():
        for j in range(nch):
            chunk_copy(cs_ref, cur, j).wait()

    return wait


def _combine_tile(x_ref, mod_ref, pos_ref, wt_ref, buf, wait=None):
    tt = x_ref.shape[0]
    sorted_rows = buf.shape[1]
    r = lax.broadcasted_iota(jnp.int32, (sorted_rows, tt), 0)
    pos = pos_ref[0].astype(jnp.int32)
    wt = wt_ref[0]
    sel = jnp.zeros((sorted_rows, tt), F32)
    for k in reversed(range(TOP_K)):
        sel = jnp.where(r == pos[k:k + 1, :], wt[k:k + 1, :], sel)
    if wait is not None:
        wait()
    y2 = lax.dot_general(sel.astype(BF16), _unpack_bf16_pairs(buf[pl.program_id(0) % 2]), TN_DIMS,
                         preferred_element_type=F32)
    return x_ref[...] + mod_ref[0, 5:6, :] * y2


def _combine(xs, mods, pos, wt, chunk_src, yb, *, tm, tpb, n_batch, sorted_rows, inproj=None, final=None):
    t, d = xs.shape
    ntiles = t // tm
    nch = sorted_rows // ROW_ALIGN

    def seg(i):
        return jnp.where(i % tpb == 0, n_batch, i // tpb)

    def cur(s):
        return jnp.minimum(s, ntiles - 1)

    def prev(s):
        return jnp.maximum(s - 1, 0)

    in_specs = [
        pl.BlockSpec((1, 1, nch), lambda s: (cur(s), 0, 0), memory_space=pltpu.SMEM),
        pl.BlockSpec((1, 1, nch), lambda s: (cur(s + 1), 0, 0), memory_space=pltpu.SMEM),
        pl.BlockSpec((tm, d), lambda s: (cur(s), 0)),
        pl.BlockSpec((1, SUBLANES, d), lambda s: (seg(cur(s)), 0, 0)),
        pl.BlockSpec((1, SUBLANES, tm), lambda s: (cur(s), 0, 0)),
        pl.BlockSpec((1, SUBLANES, tm), lambda s: (cur(s), 0, 0)),
        pl.BlockSpec(memory_space=pl.ANY),
    ]
    args = [chunk_src.reshape(ntiles, 1, nch), chunk_src.reshape(ntiles, 1, nch), xs, mods, pos, wt, yb]
    scratch = [pltpu.VMEM((2, sorted_rows, yb.shape[1]), jnp.uint32), pltpu.SemaphoreType.DMA((2,))]
    if inproj is not None:
        mods2, gain, w_bf16, cos_t, sin_t, qk_width, qkv_width = inproj
        in_width = w_bf16.shape[1]
        pos_blocks = cos_t.shape[0] // tm - 1

        def rope_blk(i):
            j = i % tpb
            return jnp.where(j == 0, pos_blocks, j - 1)

        in_specs += [
            pl.BlockSpec((1, SUBLANES, d), lambda s: (seg(prev(s)), 0, 0)),
            pl.BlockSpec((1, d), lambda s: (0, 0)),
            pl.BlockSpec((d, in_width), lambda s: (0, 0)),
            pl.BlockSpec((tm, RET_DK), lambda s: (rope_blk(prev(s)), 0)),
            pl.BlockSpec((tm, RET_DK), lambda s: (rope_blk(prev(s)), 0)),
        ]
        args += [mods2, gain, w_bf16, cos_t, sin_t]
        out_specs = [pl.BlockSpec((tm, d), lambda s: (cur(s), 0)),
                     pl.BlockSpec((tm, qkv_width), lambda s: (prev(s), 0)),
                     pl.BlockSpec((tm, in_width - qkv_width), lambda s: (prev(s), 0))]
        out_shape = [jax.ShapeDtypeStruct((t, d), F32), jax.ShapeDtypeStruct((t, qkv_width), BF16),
                     jax.ShapeDtypeStruct((t, in_width - qkv_width), BF16)]
        scratch += [pltpu.VMEM((tm, d), F32)]
        steps, follow = ntiles + 1, "inproj"
    else:
        gain, n = final
        qk_width = None
        in_specs += [pl.BlockSpec((1, d), lambda s: (0, 0))]
        args += [gain]
        out_specs = pl.BlockSpec((1, tm, d), lambda s: (s // tpb, jnp.maximum(s % tpb - 1, 0), 0))
        out_shape = jax.ShapeDtypeStruct((n_batch, n, d), F32)
        steps, follow = ntiles, "final"

    return pl.pallas_call(
        functools.partial(_combine_kernel, qk_width=qk_width, follow=follow),
        grid=(steps,),
        in_specs=in_specs,
        out_specs=out_specs,
        out_shape=out_shape,
        scratch_shapes=scratch,
        compiler_params=_params(1),
        name="combine",
    )(*args)


def _rope_tables(n, tm):
    t = jnp.arange(n, dtype=jnp.int32)
    m = RET_DK // 4
    freqs = ROPE_BASE ** (-jnp.arange(m, dtype=F32) / m)
    ang_r = (t // GRID_W).astype(F32)[:, None] * freqs[None, :]
    ang_c = (t % GRID_W).astype(F32)[:, None] * freqs[None, :]
    cos = jnp.concatenate([jnp.cos(ang_r)] * 2 + [jnp.cos(ang_c)] * 2, axis=1)
    sin = jnp.concatenate([-jnp.sin(ang_r), jnp.sin(ang_r), -jnp.sin(ang_c), jnp.sin(ang_c)], axis=1)
    cos = jnp.concatenate([cos, jnp.ones((tm, RET_DK), F32)], axis=0)
    sin = jnp.concatenate([sin, jnp.zeros((tm, RET_DK), F32)], axis=0)
    return cos, sin


def kernel(x, c, ctx, c_ctx, w_mod, b_mod, norm_mix, norm_ffn, w_in, ret_decay_f, ret_decay_b, conv_w, w_out,
           router_w, router_b, w_up, b_up, w_down, b_down, norm_final):
    n_batch, n, d = x.shape
    lc = ctx.shape[1]
    depth = w_mod.shape[0]
    tm = lc
    assert tm % 256 == 0 and n % tm == 0 and n % GRID_W == 0 and tm % GRID_W == 0
    assert n % RET_CHUNK == 0 and lc % RET_CHUNK == 0 and n_batch + 1 <= SUBLANES
    tpb = (lc + n) // tm
    qk_width = RET_HEADS * RET_DK
    qkv_width = 2 * qk_width + RET_HEADS * RET_DV
    n_exp = router_w.shape[2]
    sorted_rows = TOP_K * tm + n_exp * ROW_ALIGN
    ntiles = n_batch * tpb
    nb_max = _max_blocks(ntiles, sorted_rows, n_exp)
    total_rows = nb_max * EXPERT_TILE + 2 * sorted_rows + n_exp * EXPERT_TILE

    cond = jnp.zeros((SUBLANES, d), F32).at[:n_batch].set(c).at[n_batch].set(c_ctx)
    mod_all = _modulation(cond, w_mod, b_mod).reshape(depth, SUBLANES, 6, d)
    mod_all = jnp.pad(mod_all, ((0, 0), (0, 0), (0, SUBLANES - 6), (0, 0)))
    cos_t, sin_t = _rope_tables(n, tm)

    tile_args = dict(tm=tm, tpb=tpb, n_batch=n_batch)
    xs, qkv, rest = _inproj(ctx, x, mod_all[0], norm_mix[0][None, :], w_in[0].astype(BF16), cos_t, sin_t,
                            qk_width=qk_width, qkv_width=qkv_width, **tile_args)
    for layer in range(depth):
        mods = mod_all[layer]
        dec, dmat, gc = _decay_tables(ret_decay_f[layer], ret_decay_b[layer])
        y = _retention(qkv, dec, dmat, gc, n_batch=n_batch, lc=lc, n=n, qk_width=qk_width)
        conv_w8 = jnp.pad(conv_w[layer], ((0, SUBLANES - conv_w.shape[1]), (0, 0)))
        gain2 = norm_ffn[layer][None, :]
        rb_col = jnp.broadcast_to(router_b[layer][:, None], (n_exp, LANES))
        xs, h2, ei, wt, cnt = _mix(xs, y, rest, mods, conv_w8, w_out[layer].astype(BF16), gain2, router_w[layer].T,
                                   rb_col, **tile_args)
        off, chunk_dst, chunk_src, fill_dst, sched, n_used = _routing_plan(
            cnt[:, :, 0], tm=tm, sorted_rows=sorted_rows)
        off_col = jnp.broadcast_to(off.astype(F32)[:, :, None], (ntiles, n_exp, LANES))
        xb, pos = _dispatch(h2, ei, off_col, chunk_dst, fill_dst, tm=tm, sorted_rows=sorted_rows,
                            total_rows=total_rows)
        yb = _experts(xb, sched, n_used, w_up, b_up, w_down, b_down, layer=layer, nb_max=nb_max)
        if layer + 1 == depth:
            return _combine(xs, mods, pos, wt, chunk_src, yb, sorted_rows=sorted_rows,
                            final=(norm_final[None, :], n), **tile_args)
        nxt = layer + 1
        xs, qkv, rest = _combine(
            xs, mods, pos, wt, chunk_src, yb, sorted_rows=sorted_rows,
            inproj=(mod_all[nxt], norm_mix[nxt][None, :], w_in[nxt].astype(BF16), cos_t, sin_t, qk_width, qkv_width),
            **tile_args)
```

```python
import functools

import jax
import jax.numpy as jnp
from jax import lax
from jax.experimental import pallas as pl
from jax.experimental.pallas import tpu as pltpu

GRID_W = 64
RET_HEADS = 4
RET_DK = 128
RET_DV = 256
RET_CHUNK = 256
TOP_K = 4
SWIGLU_LIMIT = 7.0
SWIGLU_ALPHA = 1.702
ROPE_BASE = 10000.0
EPS = 1e-6

SUBLANES = 8
LANES = 128
VMEM_LIMIT = 56 * 1024 * 1024

EXPERT_TILE = 512
ROW_ALIGN = SUBLANES
FILL_SIZES = tuple(EXPERT_TILE >> k for k in range(1, EXPERT_TILE.bit_length()) if EXPERT_TILE >> k >= ROW_ALIGN)

F32 = jnp.float32
BF16 = jnp.bfloat16
NT_DIMS = (((1,), (1,)), ((), ()))
TN_DIMS = (((0,), (0,)), ((), ()))


def _params(n_axes=1):
    return pltpu.CompilerParams(dimension_semantics=("arbitrary",) * n_axes, vmem_limit_bytes=VMEM_LIMIT)


def _modulated_norm(x, gain, shift, scale):
    ms = jnp.mean(x * x, axis=-1, keepdims=True)
    return x * lax.rsqrt(ms + EPS) * (gain * (1.0 + scale)) + shift


def _sigmoid(z):
    return 1.0 / (1.0 + jnp.exp(-z))


def _split_bf16(a):
    hi = a.astype(BF16)
    lo = (a - hi.astype(F32)).astype(BF16)
    return hi, lo


def _pack_bf16_pairs(a):
    w = a.shape[1] // 2
    lo = lax.bitcast_convert_type(a[:, :w], jnp.uint32)
    hi = lax.bitcast_convert_type(a[:, w:], jnp.uint32)
    return (lo >> 16) | hi


def _unpack_bf16_pairs(p):
    lo = lax.bitcast_convert_type(p << 16, F32)
    hi = lax.bitcast_convert_type(p & jnp.uint32(0xFFFF0000), F32)
    return jnp.concatenate([lo, hi], axis=1).astype(BF16)


def _modulation_kernel(c_ref, w_ref, b_ref, o_ref):
    c = c_ref[...]
    s = c * _sigmoid(c)
    s_hi, s_lo = _split_bf16(s)
    w_hi, w_lo = _split_bf16(w_ref[0])
    acc = jnp.dot(s_hi, w_hi, preferred_element_type=F32)
    acc += jnp.dot(s_hi, w_lo, preferred_element_type=F32)
    acc += jnp.dot(s_lo, w_hi, preferred_element_type=F32)
    o_ref[0] = acc + b_ref[0]


def _modulation(cond, w_mod, b_mod):
    depth, d, six_d = w_mod.shape
    nblk = six_d // d
    return pl.pallas_call(
        _modulation_kernel,
        grid=(depth, nblk),
        in_specs=[
            pl.BlockSpec((SUBLANES, d), lambda l, j: (0, 0)),
            pl.BlockSpec((1, d, d), lambda l, j: (l, 0, j)),
            pl.BlockSpec((1, 1, d), lambda l, j: (l, 0, j)),
        ],
        out_specs=pl.BlockSpec((1, SUBLANES, d), lambda l, j: (l, 0, j)),
        out_shape=jax.ShapeDtypeStruct((depth, SUBLANES, six_d), F32),
        compiler_params=_params(2),
        name="modulation",
    )(cond, w_mod, b_mod.reshape(depth, 1, six_d))


def _inproj_kernel(ctx_ref, x_ref, mod_ref, gain_ref, w_ref, cos_ref, sin_ref, xs_ref, qkv_ref, rest_ref, *,
                   qk_width, tpb):
    is_ctx = (pl.program_id(0) % tpb) == 0
    x = jnp.where(is_ctx, ctx_ref[0], x_ref[0])
    xs_ref[...] = x
    _inproj_tile(x, mod_ref, gain_ref, w_ref, cos_ref, sin_ref, qkv_ref, rest_ref, qk_width)


def _inproj_tile(x, mod_ref, gain_ref, w_ref, cos_ref, sin_ref, qkv_ref, rest_ref, qk_width):
    q_scale = RET_DK ** -0.5
    h = _modulated_norm(x, gain_ref[...], mod_ref[0, 0:1, :], mod_ref[0, 1:2, :]).astype(BF16)
    reps = qk_width // RET_DK
    cos = jnp.concatenate([cos_ref[...]] * reps, axis=1)
    sin = jnp.concatenate([sin_ref[...]] * reps, axis=1)
    lane = lax.broadcasted_iota(jnp.int32, cos.shape, 1)
    first_half = (lane % (RET_DK // 2)) < (RET_DK // 4)
    for part, scale in ((0, q_scale), (1, 1.0)):
        c0 = part * qk_width
        p = jnp.dot(h, w_ref[:, c0:c0 + qk_width], preferred_element_type=F32)
        partner = jnp.where(first_half, pltpu.roll(p, qk_width - RET_DK // 4, 1), pltpu.roll(p, RET_DK // 4, 1))
        r = p * cos + partner * sin
        if scale != 1.0:
            r = r * scale
        qkv_ref[:, c0:c0 + qk_width] = r.astype(BF16)
    qkv_width = qkv_ref.shape[1]
    step = 512
    for c0 in range(2 * qk_width, qkv_width, step):
        qkv_ref[:, c0:c0 + step] = jnp.dot(h, w_ref[:, c0:c0 + step], preferred_element_type=F32).astype(BF16)
    for c0 in range(0, rest_ref.shape[1], step):
        rest_ref[:, c0:c0 + step] = jnp.dot(
            h, w_ref[:, qkv_width + c0:qkv_width + c0 + step], preferred_element_type=F32).astype(BF16)


def _inproj(ctx, x, mods, gain, w_bf16, cos_t, sin_t, *, tm, tpb, n_batch, qk_width, qkv_width):
    d = x.shape[2]
    t = n_batch * tpb * tm
    in_width = w_bf16.shape[1]
    rest_width = in_width - qkv_width
    pos_blocks = cos_t.shape[0] // tm - 1

    def seg(i):
        return jnp.where(i % tpb == 0, n_batch, i // tpb)

    def pos(i):
        j = i % tpb
        return jnp.where(j == 0, pos_blocks, j - 1)

    return pl.pallas_call(
        functools.partial(_inproj_kernel, qk_width=qk_width, tpb=tpb),
        grid=(t // tm,),
        in_specs=[
            pl.BlockSpec((1, tm, d), lambda i: (i // tpb, 0, 0)),
            pl.BlockSpec((1, tm, d), lambda i: (i // tpb, jnp.maximum(i % tpb - 1, 0), 0)),
            pl.BlockSpec((1, SUBLANES, d), lambda i: (seg(i), 0, 0)),
            pl.BlockSpec((1, d), lambda i: (0, 0)),
            pl.BlockSpec((d, in_width), lambda i: (0, 0)),
            pl.BlockSpec((tm, RET_DK), lambda i: (pos(i), 0)),
            pl.BlockSpec((tm, RET_DK), lambda i: (pos(i), 0)),
        ],
        out_specs=[
            pl.BlockSpec((tm, d), lambda i: (i, 0)),
            pl.BlockSpec((tm, qkv_width), lambda i: (i, 0)),
            pl.BlockSpec((tm, rest_width), lambda i: (i, 0)),
        ],
        out_shape=[
            jax.ShapeDtypeStruct((t, d), F32),
            jax.ShapeDtypeStruct((t, qkv_width), BF16),
            jax.ShapeDtypeStruct((t, rest_width), BF16),
        ],
        compiler_params=_params(1),
        name="inproj",
    )(ctx, x, mods, gain, w_bf16, cos_t, sin_t)


def _retention_kernel(gc_ref, q_ref, k_ref, v_ref, dec_ref, dmat_ref, y_ref, sf_ref, sb_ref, st_ref, *, lc, n):
    head = pl.program_id(1)
    c = RET_CHUNK
    qd = dec_ref[0, :, 0:2 * RET_DK]
    kdf = dec_ref[0, :, 2 * RET_DK:3 * RET_DK]
    kdb = dec_ref[0, :, 3 * RET_DK:4 * RET_DK]
    gcf = gc_ref[head, 0]
    gcb = gc_ref[head, 1]

    def rows(chunk):
        return pl.ds(pl.multiple_of(chunk * c, c), c)

    def advance(s_ref, chunk, kd, gc):
        kdec = (k_ref[rows(chunk), :].astype(F32) * kd).astype(BF16)
        kv = lax.dot_general(kdec, v_ref[rows(chunk), :], TN_DIMS, preferred_element_type=F32)
        s_ref[...] = gc * s_ref[...] + kv

    def sweep(chunk0, nchunks):
        def states(i, carry):
            cf = chunk0 + i
            cb = chunk0 + nchunks - 1 - i
            st_ref[cf, 0:RET_DK, :] = sf_ref[...].astype(BF16)
            st_ref[cb, RET_DK:2 * RET_DK, :] = sb_ref[...].astype(BF16)
            advance(sf_ref, cf, kdf, gcf)
            advance(sb_ref, cb, kdb, gcb)
            return carry

        def outputs(i, carry):
            r = rows(chunk0 + i)
            q, k, v = q_ref[r, :], k_ref[r, :], v_ref[r, :]
            scores = lax.dot_general(q, k, NT_DIMS, preferred_element_type=F32)
            qdec = (jnp.concatenate([q, q], axis=1).astype(F32) * qd).astype(BF16)
            y = jnp.dot((scores * dmat_ref[0]).astype(BF16), v, preferred_element_type=F32)
            y += jnp.dot(qdec, st_ref[chunk0 + i], preferred_element_type=F32)
            y_ref[r, :] = (y * lax.rsqrt(jnp.mean(y * y, axis=-1, keepdims=True) + EPS)).astype(BF16)
            return carry

        lax.fori_loop(0, nchunks, states, 0, unroll=4 if nchunks % 4 == 0 else 1)
        lax.fori_loop(0, nchunks, outputs, 0, unroll=8 if nchunks % 8 == 0 else 1)

    sf_ref[...] = jnp.zeros_like(sf_ref)
    sb_ref[...] = jnp.zeros_like(sb_ref)
    sweep(0, lc // c)
    sweep(lc // c, n // c)


def _retention(qkv, dec, dmat, gc, *, n_batch, lc, n, qk_width):
    s = lc + n
    t = qkv.shape[0]
    k_blk0 = qk_width // RET_DK
    v_blk0 = 2 * qk_width // RET_DV
    return pl.pallas_call(
        functools.partial(_retention_kernel, lc=lc, n=n),
        grid_spec=pltpu.PrefetchScalarGridSpec(
            num_scalar_prefetch=1,
            grid=(n_batch, RET_HEADS),
            in_specs=[
                pl.BlockSpec((s, RET_DK), lambda b, h, gc: (b, h)),
                pl.BlockSpec((s, RET_DK), lambda b, h, gc: (b, k_blk0 + h)),
                pl.BlockSpec((s, RET_DV), lambda b, h, gc: (b, v_blk0 + h)),
                pl.BlockSpec((1, RET_CHUNK, 4 * RET_DK), lambda b, h, gc: (h, 0, 0)),
                pl.BlockSpec((1, RET_CHUNK, RET_CHUNK), lambda b, h, gc: (h, 0, 0)),
            ],
            out_specs=pl.BlockSpec((s, RET_DV), lambda b, h, gc: (b, h)),
            scratch_shapes=[
                pltpu.VMEM((RET_DK, RET_DV), F32),
                pltpu.VMEM((RET_DK, RET_DV), F32),
                pltpu.VMEM((s // RET_CHUNK, 2 * RET_DK, RET_DV), BF16),
            ],
        ),
        out_shape=jax.ShapeDtypeStruct((t, RET_HEADS * RET_DV), BF16),
        compiler_params=_params(2),
        name="retention",
    )(gc, qkv, qkv, qkv, dec, dmat)


def _decay_tables(logit_f, logit_b):
    c = RET_CHUNK
    lg_f = jax.nn.log_sigmoid(logit_f.astype(F32))[:, None]
    lg_b = jax.nn.log_sigmoid(logit_b.astype(F32))[:, None]
    i = jnp.arange(c, dtype=F32)[None, :]
    vecs = jnp.stack([
        jnp.exp((i + 1) * lg_f), jnp.exp((c - i) * lg_b),
        jnp.exp((c - 1 - i) * lg_f), jnp.exp(i * lg_b)], axis=2)
    dec = jnp.broadcast_to(vecs[..., None], vecs.shape + (RET_DK,)).reshape(vecs.shape[0], c, 4 * RET_DK)
    diff = i[0][:, None] - i[0][None, :]
    lower = jnp.where(diff >= 0, jnp.exp(jnp.where(diff >= 0, diff, 0.0)[None] * lg_f[:, :, None]), 0.0)
    upper = jnp.where(diff < 0, jnp.exp(jnp.where(diff < 0, -diff, 0.0)[None] * lg_b[:, :, None]), 0.0)
    gc = jnp.concatenate([jnp.exp(c * lg_f), jnp.exp(c * lg_b)], axis=1)
    return dec, lower + upper, gc


def _mix_kernel(x_ref, y_ref, rest_ref, mod_ref, cw_ref, wo_ref, gain2_ref, rw_ref, rb_ref,
                o_ref, h2_ref, ei_ref, wt_ref, cnt_ref, h_prev, *, tpb, ntiles, ret_width, conv_width):
    step = pl.program_id(0)

    @pl.when(step == 0)
    def _():
        h_prev[...] = jnp.zeros_like(h_prev)

    _route(h_prev[...], rw_ref, rb_ref, ei_ref, wt_ref, cnt_ref)
    is_ctx = (jnp.minimum(step, ntiles - 1) % tpb) == 0
    tm = x_ref.shape[0]
    g = rest_ref[:, 0:ret_width].astype(F32)
    cb = rest_ref[:, ret_width:ret_width + conv_width].astype(F32)
    cc = rest_ref[:, ret_width + conv_width:ret_width + 2 * conv_width].astype(F32)
    cx = rest_ref[:, ret_width + 2 * conv_width:ret_width + 3 * conv_width].astype(F32)
    ret = (g * _sigmoid(g) * y_ref[...].astype(F32)).astype(BF16)
    u = cc * cx
    row = lax.broadcasted_iota(jnp.int32, u.shape, 0)
    pos = jnp.where(is_ctx, row, row % GRID_W)
    last = jnp.where(is_ctx, tm - 1, GRID_W - 1)
    u_prev = jnp.where(pos == 0, 0.0, pltpu.roll(u, 1, 0))
    u_next = jnp.where(pos == last, 0.0, pltpu.roll(u, tm - 1, 0))
    conv = (cb * (u_prev * cw_ref[0:1, :] + u * cw_ref[1:2, :] + u_next * cw_ref[2:3, :])).astype(BF16)
    halves = []
    n_half = wo_ref.shape[1] // 2
    for c0 in (0, n_half):
        part = jnp.dot(ret, wo_ref[0:ret_width, c0:c0 + n_half], preferred_element_type=F32)
        part += jnp.dot(conv, wo_ref[ret_width:ret_width + conv_width, c0:c0 + n_half], preferred_element_type=F32)
        halves.append(part)
    mix = jnp.concatenate(halves, axis=1)
    x_new = x_ref[...] + mod_ref[0, 2:3, :] * mix
    o_ref[...] = x_new
    h2 = _modulated_norm(x_new, gain2_ref[...], mod_ref[0, 3:4, :], mod_ref[0, 4:5, :])
    h2_ref[...] = h2.astype(BF16)
    h_prev[...] = h2


def _mix(xs, y, rest, mods, conv_w8, wo_bf16, gain2, rw_t, rb_col, *, tm, tpb, n_batch):
    t, d = xs.shape
    ntiles = t // tm
    ret_width = y.shape[1]
    conv_width = conv_w8.shape[1]
    n_exp = rw_t.shape[0]

    def cur(s):
        return jnp.minimum(s, ntiles - 1)

    def prev(s):
        return jnp.maximum(s - 1, 0)

    def seg(i):
        return jnp.where(i % tpb == 0, n_batch, i // tpb)

    return pl.pallas_call(
        functools.partial(_mix_kernel, tpb=tpb, ntiles=ntiles, ret_width=ret_width, conv_width=conv_width),
        grid=(ntiles + 1,),
        in_specs=[
            pl.BlockSpec((tm, d), lambda s: (cur(s), 0)),
            pl.BlockSpec((tm, ret_width), lambda s: (cur(s), 0)),
            pl.BlockSpec((tm, rest.shape[1]), lambda s: (cur(s), 0)),
            pl.BlockSpec((1, SUBLANES, d), lambda s: (seg(cur(s)), 0, 0)),
            pl.BlockSpec((SUBLANES, conv_width), lambda s: (0, 0)),
            pl.BlockSpec(wo_bf16.shape, lambda s: (0, 0)),
            pl.BlockSpec((1, d), lambda s: (0, 0)),
            pl.BlockSpec((n_exp, d), lambda s: (0, 0)),
            pl.BlockSpec((n_exp, LANES), lambda s: (0, 0)),
        ],
        out_specs=[
            pl.BlockSpec((tm, d), lambda s: (cur(s), 0)),
            pl.BlockSpec((tm, d), lambda s: (cur(s), 0)),
            pl.BlockSpec((1, SUBLANES, tm), lambda s: (prev(s), 0, 0)),
            pl.BlockSpec((1, SUBLANES, tm), lambda s: (prev(s), 0, 0)),
            pl.BlockSpec((1, n_exp, LANES), lambda s: (prev(s), 0, 0)),
        ],
        out_shape=[
            jax.ShapeDtypeStruct((t, d), F32),
            jax.ShapeDtypeStruct((t, d), BF16),
            jax.ShapeDtypeStruct((ntiles, SUBLANES, tm), jnp.int32),
            jax.ShapeDtypeStruct((ntiles, SUBLANES, tm), F32),
            jax.ShapeDtypeStruct((ntiles, n_exp, LANES), jnp.int32),
        ],
        scratch_shapes=[pltpu.VMEM((tm, d), F32)],
        compiler_params=_params(1),
        name="mix",
    )(xs, y, rest, mods, conv_w8, wo_bf16, gain2, rw_t, rb_col)


def _route(h2, rw_ref, rb_ref, ei_ref, wt_ref, cnt_ref):
    h_hi, h_lo = _split_bf16(h2)
    w_hi, w_lo = _split_bf16(rw_ref[...])
    logits = lax.dot_general(w_hi, h_hi, NT_DIMS, preferred_element_type=F32)
    logits += lax.dot_general(w_hi, h_lo, NT_DIMS, preferred_element_type=F32)
    logits += lax.dot_general(w_lo, h_hi, NT_DIMS, preferred_element_type=F32)
    logits += rb_ref[:, 0:1]
    n_exp, tt = logits.shape
    expert = lax.broadcasted_iota(jnp.int32, (n_exp, tt), 0).astype(F32)
    slot = lax.broadcasted_iota(jnp.int32, (SUBLANES, tt), 0)
    ei = jnp.zeros((SUBLANES, tt), F32)
    ev = jnp.zeros((SUBLANES, tt), F32)
    taken = jnp.zeros((n_exp, tt), F32)
    top = None
    denom = jnp.zeros((1, tt), F32)
    for k in range(TOP_K):
        m = jnp.max(logits, axis=0, keepdims=True)
        idx = jnp.min(jnp.where(logits == m, expert, float(n_exp)), axis=0, keepdims=True)
        hit = expert == idx
        taken += hit.astype(F32)
        logits = jnp.where(hit, -jnp.inf, logits)
        if k == 0:
            top = m
        e = jnp.exp(m - top)
        denom += e
        ei = jnp.where(slot == k, idx, ei)
        ev = jnp.where(slot == k, e, ev)
    ei_ref[0] = ei.astype(jnp.int32)
    wt_ref[0] = jnp.where(slot < TOP_K, ev / denom, 0.0)
    cnt = jnp.sum(taken, axis=1, keepdims=True)
    cnt_ref[0] = jnp.broadcast_to(cnt, (n_exp, LANES)).astype(jnp.int32)


def _routing_plan(cnt, *, tm, sorted_rows):
    ntiles, n_exp = cnt.shape
    seg_len = (cnt + ROW_ALIGN - 1) // ROW_ALIGN * ROW_ALIGN
    off = jnp.cumsum(seg_len, axis=1) - seg_len
    tile_rows = jnp.sum(seg_len, axis=1)
    before = jnp.cumsum(seg_len, axis=0) - seg_len
    filled = jnp.sum(seg_len, axis=0)
    region = (filled + EXPERT_TILE - 1) // EXPERT_TILE * EXPERT_TILE
    region_end = jnp.cumsum(region)
    region_start = region_end - region
    base = region_start[None, :] + before

    nb_max = _max_blocks(ntiles, sorted_rows, n_exp)
    dump0 = nb_max * EXPERT_TILE
    nch = sorted_rows // ROW_ALIGN
    q = jnp.arange(nch, dtype=jnp.int32) * ROW_ALIGN
    seg_end = off + seg_len
    inside = jnp.logical_and(off[:, None, :] <= q[None, :, None], q[None, :, None] < seg_end[:, None, :])
    valid = q[None, :] < tile_rows[:, None]
    row = q[None, :] + jnp.sum(jnp.where(inside, (base - off)[:, None, :], 0), axis=2)
    parity = (jnp.arange(ntiles, dtype=jnp.int32) % 2)[:, None]
    chunk_dst = jnp.where(valid, row, dump0 + parity * sorted_rows + q[None, :]).astype(jnp.int32)
    chunk_src = jnp.where(valid, row, 0).astype(jnp.int32)

    sizes = jnp.asarray(FILL_SIZES, dtype=jnp.int32)[None, :]
    gap = (region - filled)[:, None]
    used = (gap & sizes) != 0
    fill_row = filled[:, None] + (gap & ~(2 * sizes - 1))
    fill_dump = dump0 + 2 * sorted_rows + jnp.arange(n_exp, dtype=jnp.int32)[:, None] * EXPERT_TILE + (
        EXPERT_TILE - 2 * sizes)
    fill_dst = jnp.where(used, region_start[:, None] + fill_row, fill_dump).reshape(-1).astype(jnp.int32)

    blk = jnp.arange(nb_max, dtype=jnp.int32) * EXPERT_TILE
    block_e = jnp.sum((region_end[None, :] <= blk[:, None]).astype(jnp.int32), axis=1)
    block_e = jnp.minimum(block_e, n_exp - 1).astype(jnp.int32)
    n_used = (region_end[-1] // EXPERT_TILE).astype(jnp.int32).reshape(1)
    used = region > 0
    eidx = jnp.arange(n_exp, dtype=jnp.int32)
    rank = (jnp.cumsum(used) - used).astype(jnp.int32)
    later = jnp.where(jnp.logical_and(used[None, :], eidx[None, :] > eidx[:, None]), eidx[None, :], n_exp)
    nxt = jnp.min(later, axis=1)
    nxt = jnp.where(nxt == n_exp, -1, nxt)
    mine = block_e[:, None] == eidx[None, :]
    data_end = jnp.sum(jnp.where(mine, (region_start + filled)[None, :], 0), axis=1)
    sched = jnp.stack([block_e, jnp.sum(jnp.where(mine, (rank % 2)[None, :], 0), axis=1),
                       jnp.sum(jnp.where(mine, nxt[None, :], 0), axis=1),
                       jnp.clip(data_end - blk, 0, EXPERT_TILE)]).astype(jnp.int32)
    return off, chunk_dst, chunk_src, fill_dst, sched, n_used


def _max_blocks(ntiles, sorted_rows, n_exp):
    return -(-(ntiles * sorted_rows + n_exp * EXPERT_TILE) // EXPERT_TILE)


def _sorted_positions(ei, off_col, tt):
    n_exp = off_col.shape[0]
    expert = lax.broadcasted_iota(jnp.int32, (n_exp, tt), 0)
    hits = [expert == ei[k:k + 1, :] for k in range(TOP_K)]
    chosen = sum(h.astype(F32) for h in hits).astype(BF16)
    t_row = lax.broadcasted_iota(jnp.int32, (tt, tt), 0)
    t_col = lax.broadcasted_iota(jnp.int32, (tt, tt), 1)
    earlier = (t_row < t_col).astype(BF16)
    rank = jnp.dot(chosen, earlier, preferred_element_type=F32) + off_col
    return [jnp.sum(jnp.where(h, rank, 0.0), axis=0, keepdims=True) for h in hits]


def _dispatch_kernel(cd_ref, fd_ref, h2_ref, ei_ref, off_ref, xb_ref, pos_ref, buf, zeros, perm_s, sem, fill_sem):
    step = pl.program_id(0)
    tt = h2_ref.shape[0]
    sorted_rows = buf.shape[1]
    nch = sorted_rows // ROW_ALIGN

    def fill_copy(e, s):
        size = FILL_SIZES[s]
        dst = pl.multiple_of(fd_ref[e * len(FILL_SIZES) + s], ROW_ALIGN)
        return pltpu.make_async_copy(zeros.at[pl.ds(0, size), :], xb_ref.at[pl.ds(dst, size), :], fill_sem)

    @pl.when(step == 0)
    def _():
        zeros[...] = jnp.zeros_like(zeros)
        perm_s[...] = jnp.zeros_like(perm_s)

        def start(e, c):
            for s in range(len(FILL_SIZES)):
                fill_copy(e, s).start()
            return c

        def wait(e, c):
            for s in range(len(FILL_SIZES)):
                fill_copy(e, s).wait()
            return c

        n_regions = fd_ref.shape[0] // len(FILL_SIZES)
        lax.fori_loop(0, n_regions, start, 0)
        lax.fori_loop(0, n_regions, wait, 0)

    slot = (step + 1) % 2
    last_step = pl.num_programs(0) - 1

    def chunk_copy(which, j):
        dst = pl.multiple_of(cd_ref[0, 0, j], ROW_ALIGN)
        return pltpu.make_async_copy(buf.at[which, pl.ds(j * ROW_ALIGN, ROW_ALIGN), :],
                                     xb_ref.at[pl.ds(dst, ROW_ALIGN), :], sem.at[which])

    def drain(which):
        for j in range(nch):
            chunk_copy(which, j).wait()

    @pl.when(step >= 3)
    def _():
        drain(slot)

    buf[slot] = _pack_bf16_pairs(jnp.dot(perm_s[...], h2_ref[...], preferred_element_type=F32))

    pos = _sorted_positions(ei_ref[0], off_ref[0][:, 0:1], tt)
    slot_id = lax.broadcasted_iota(jnp.int32, (SUBLANES, tt), 0)
    pos8 = jnp.zeros((SUBLANES, tt), F32)
    for k in range(TOP_K):
        pos8 = jnp.where(slot_id == k, pos[k], pos8)
    pos_ref[0] = pos8
    r = lax.broadcasted_iota(jnp.int32, (sorted_rows, tt), 0)
    hit = r == pos[0].astype(jnp.int32)
    for k in range(1, TOP_K):
        hit = jnp.logical_or(hit, r == pos[k].astype(jnp.int32))
    perm_s[...] = jnp.where(hit, 1.0, 0.0).astype(BF16)

    @pl.when(step >= 1)
    def _():
        for j in range(nch):
            chunk_copy(slot, j).start()

    @pl.when(step == last_step)
    def _():
        drain(slot)

    @pl.when(jnp.logical_and(step == last_step, step >= 2))
    def _():
        drain(1 - slot)


def _dispatch(h2, ei, off_col, chunk_dst, fill_dst, *, tm, sorted_rows, total_rows):
    t, d = h2.shape
    ntiles = t // tm
    n_exp = off_col.shape[1]
    nch = sorted_rows // ROW_ALIGN

    def cur(s):
        return jnp.minimum(s, ntiles - 1)

    def prev(s):
        return jnp.maximum(s - 1, 0)

    return pl.pallas_call(
        _dispatch_kernel,
        grid=(ntiles + 1,),
        in_specs=[
            pl.BlockSpec((1, 1, nch), lambda s: (prev(s), 0, 0), memory_space=pltpu.SMEM),
            pl.BlockSpec(memory_space=pltpu.SMEM),
            pl.BlockSpec((tm, d), lambda s: (prev(s), 0)),
            pl.BlockSpec((1, SUBLANES, tm), lambda s: (cur(s), 0, 0)),
            pl.BlockSpec((1, n_exp, LANES), lambda s: (cur(s), 0, 0)),
        ],
        out_specs=[
            pl.BlockSpec(memory_space=pl.ANY),
            pl.BlockSpec((1, SUBLANES, tm), lambda s: (cur(s), 0, 0)),
        ],
        out_shape=[
            jax.ShapeDtypeStruct((total_rows, d // 2), jnp.uint32),
            jax.ShapeDtypeStruct((ntiles, SUBLANES, tm), F32),
        ],
        scratch_shapes=[
            pltpu.VMEM((2, sorted_rows, d // 2), jnp.uint32),
            pltpu.VMEM((FILL_SIZES[0], d // 2), jnp.uint32),
            pltpu.VMEM((sorted_rows, tm), BF16),
            pltpu.SemaphoreType.DMA((2,)),
            pltpu.SemaphoreType.DMA(()),
        ],
        compiler_params=_params(1),
        name="dispatch",
    )(chunk_dst.reshape(ntiles, 1, nch), fill_dst, h2, ei, off_col)


def _experts_kernel(sched_ref, nu_ref, xb_ref, wu_hbm, bu_ref, wd_hbm, bd_ref, yb_ref,
                    wu_f, wd_f, wu_s, wd_s, sem, *, d_ff, layer):
    i = pl.program_id(0)
    expert, slot, nxt = sched_ref[0, i], sched_ref[1, i], sched_ref[2, i]
    fresh = jnp.logical_or(i == 0, expert != sched_ref[0, jnp.maximum(i - 1, 0)])

    def fetch(e, s):
        return (pltpu.make_async_copy(wu_hbm.at[layer, e], wu_f.at[s], sem.at[0, s]),
                pltpu.make_async_copy(wd_hbm.at[layer, e], wd_f.at[s], sem.at[1, s]))

    @pl.when(i == 0)
    def _():
        for c in fetch(expert, slot):
            c.start()

    @pl.when(jnp.logical_and(fresh, i < nu_ref[0]))
    def _():
        for c in fetch(expert, slot):
            c.wait()

        @pl.when(nxt >= 0)
        def _():
            for c in fetch(nxt, 1 - slot):
                c.start()

        wu_s[...] = wu_f[slot].astype(BF16)
        wd_s[...] = wd_f[slot].astype(BF16)

    def mlp(rows):
        u = jnp.dot(_unpack_bf16_pairs(xb_ref[0:rows, :]), wu_s[...], preferred_element_type=F32) + bu_ref[0, 0]
        gate = jnp.minimum(u[:, :d_ff], SWIGLU_LIMIT)
        lin = jnp.clip(u[:, d_ff:], -SWIGLU_LIMIT, SWIGLU_LIMIT)
        act = (gate * _sigmoid(SWIGLU_ALPHA * gate) * (lin + 1.0)).astype(BF16)
        y = jnp.dot(act, wd_s[...], preferred_element_type=F32) + bd_ref[0, 0]
        yb_ref[0:rows, :] = _pack_bf16_pairs(y.astype(BF16).astype(F32))

    rows_used = sched_ref[3, i]
    quarter = xb_ref.shape[0] // 4
    for q in range(1, 5):
        in_range = jnp.logical_and(rows_used > (q - 1) * quarter, rows_used <= q * quarter)
        pl.when(jnp.logical_and(i < nu_ref[0], in_range))(functools.partial(mlp, q * quarter))


def _experts(xb, sched, n_used, w_up, b_up, w_down, b_down, *, layer, nb_max):
    depth, n_exp, d, two_ff = w_up.shape
    d_ff = two_ff // 2
    dp = xb.shape[1]

    def blk(i, sc, nu):
        return jnp.minimum(i, nu[0] - 1)

    def exp(i, sc, nu):
        return sc[0, jnp.minimum(i, nu[0] - 1)]

    return pl.pallas_call(
        functools.partial(_experts_kernel, d_ff=d_ff, layer=layer),
        grid_spec=pltpu.PrefetchScalarGridSpec(
            num_scalar_prefetch=2,
            grid=(nb_max,),
            in_specs=[
                pl.BlockSpec((EXPERT_TILE, dp), lambda i, sc, nu: (blk(i, sc, nu), 0)),
                pl.BlockSpec(memory_space=pl.ANY),
                pl.BlockSpec((1, 1, 1, two_ff), lambda i, sc, nu: (layer, exp(i, sc, nu), 0, 0)),
                pl.BlockSpec(memory_space=pl.ANY),
                pl.BlockSpec((1, 1, 1, d), lambda i, sc, nu: (layer, exp(i, sc, nu), 0, 0)),
            ],
            out_specs=pl.BlockSpec((EXPERT_TILE, dp), lambda i, sc, nu: (blk(i, sc, nu), 0)),
            scratch_shapes=[
                pltpu.VMEM((2, d, two_ff), F32),
                pltpu.VMEM((2, d_ff, d), F32),
                pltpu.VMEM((d, two_ff), BF16),
                pltpu.VMEM((d_ff, d), BF16),
                pltpu.SemaphoreType.DMA((2, 2)),
            ],
        ),
        out_shape=jax.ShapeDtypeStruct((nb_max * EXPERT_TILE, dp), jnp.uint32),
        compiler_params=_params(1),
        name="experts",
    )(sched, n_used, xb, w_up, b_up.reshape(depth, n_exp, 1, two_ff), w_down, b_down.reshape(depth, n_exp, 1, d))


def _combine_kernel(cs_ref, csn_ref, x_ref, mod_ref, pos_ref, wt_ref, yb_ref, *rest, qk_width, follow):
    if follow == "inproj":
        mod2_ref, gain_ref, w_ref, cos_ref, sin_ref, o_ref, qkv_ref, rest_ref, buf, sem, x_prev = rest

        @pl.when(pl.program_id(0) == 0)
        def _():
            x_prev[...] = jnp.zeros_like(x_prev)

        _combine_fetch(cs_ref, csn_ref, yb_ref, buf, sem)()
        _inproj_tile(x_prev[...], mod2_ref, gain_ref, w_ref, cos_ref, sin_ref, qkv_ref, rest_ref, qk_width)
        x_new = _combine_tile(x_ref, mod_ref, pos_ref, wt_ref, buf)
        o_ref[...] = x_new
        x_prev[...] = x_new
    else:
        gain_ref, o_ref, buf, sem = rest
        wait = _combine_fetch(cs_ref, csn_ref, yb_ref, buf, sem)
        x_new = _combine_tile(x_ref, mod_ref, pos_ref, wt_ref, buf, wait)
        ms = jnp.mean(x_new * x_new, axis=-1, keepdims=True)
        o_ref[0] = x_new * lax.rsqrt(ms + EPS) * gain_ref[...]


def _combine_fetch(cs_ref, csn_ref, yb_ref, buf, sem):
    step = pl.program_id(0)
    cur = step % 2
    nch = buf.shape[1] // ROW_ALIGN

    def chunk_copy(table, slot, j):
        src = pl.multiple_of(table[0, 0, j], ROW_ALIGN)
        return pltpu.make_async_copy(yb_ref.at[pl.ds(src, ROW_ALIGN), :],
                                     buf.at[slot, pl.ds(j * ROW_ALIGN, ROW_ALIGN), :], sem.at[slot])

    @pl.when(step == 0)
    def _():
        for j in range(nch):
            chunk_copy(cs_ref, cur, j).start()

    @pl.when(step + 1 < pl.num_programs(0))
    def _():
        for j in range(nch):
            chunk_copy(csn_ref, 1 - cur, j).start()

    def wait():
        for j in range(nch):
            chunk_copy(cs_ref, cur, j).wait()

    return wait


def _combine_tile(x_ref, mod_ref, pos_ref, wt_ref, buf, wait=None):
    tt = x_ref.shape[0]
    sorted_rows = buf.shape[1]
    r = lax.broadcasted_iota(jnp.int32, (sorted_rows, tt), 0)
    pos = pos_ref[0].astype(jnp.int32)
    wt = wt_ref[0]
    sel = jnp.zeros((sorted_rows, tt), F32)
    for k in reversed(range(TOP_K)):
        sel = jnp.where(r == pos[k:k + 1, :], wt[k:k + 1, :], sel)
    if wait is not None:
        wait()
    y2 = lax.dot_general(sel.astype(BF16), _unpack_bf16_pairs(buf[pl.program_id(0) % 2]), TN_DIMS,
                         preferred_element_type=F32)
    return x_ref[...] + mod_ref[0, 5:6, :] * y2


def _combine(xs, mods, pos, wt, chunk_src, yb, *, tm, tpb, n_batch, sorted_rows, inproj=None, final=None):
    t, d = xs.shape
    ntiles = t // tm
    nch = sorted_rows // ROW_ALIGN

    def seg(i):
        return jnp.where(i % tpb == 0, n_batch, i // tpb)

    def cur(s):
        return jnp.minimum(s, ntiles - 1)

    def prev(s):
        return jnp.maximum(s - 1, 0)

    in_specs = [
        pl.BlockSpec((1, 1, nch), lambda s: (cur(s), 0, 0), memory_space=pltpu.SMEM),
        pl.BlockSpec((1, 1, nch), lambda s: (cur(s + 1), 0, 0), memory_space=pltpu.SMEM),
        pl.BlockSpec((tm, d), lambda s: (cur(s), 0)),
        pl.BlockSpec((1, SUBLANES, d), lambda s: (seg(cur(s)), 0, 0)),
        pl.BlockSpec((1, SUBLANES, tm), lambda s: (cur(s), 0, 0)),
        pl.BlockSpec((1, SUBLANES, tm), lambda s: (cur(s), 0, 0)),
        pl.BlockSpec(memory_space=pl.ANY),
    ]
    args = [chunk_src.reshape(ntiles, 1, nch), chunk_src.reshape(ntiles, 1, nch), xs, mods, pos, wt, yb]
    scratch = [pltpu.VMEM((2, sorted_rows, yb.shape[1]), jnp.uint32), pltpu.SemaphoreType.DMA((2,))]
    if inproj is not None:
        mods2, gain, w_bf16, cos_t, sin_t, qk_width, qkv_width = inproj
        in_width = w_bf16.shape[1]
        pos_blocks = cos_t.shape[0] // tm - 1

        def rope_blk(i):
            j = i % tpb
            return jnp.where(j == 0, pos_blocks, j - 1)

        in_specs += [
            pl.BlockSpec((1, SUBLANES, d), lambda s: (seg(prev(s)), 0, 0)),
            pl.BlockSpec((1, d), lambda s: (0, 0)),
            pl.BlockSpec((d, in_width), lambda s: (0, 0)),
            pl.BlockSpec((tm, RET_DK), lambda s: (rope_blk(prev(s)), 0)),
            pl.BlockSpec((tm, RET_DK), lambda s: (rope_blk(prev(s)), 0)),
        ]
        args += [mods2, gain, w_bf16, cos_t, sin_t]
        out_specs = [pl.BlockSpec((tm, d), lambda s: (cur(s), 0)),
                     pl.BlockSpec((tm, qkv_width), lambda s: (prev(s), 0)),
                     pl.BlockSpec((tm, in_width - qkv_width), lambda s: (prev(s), 0))]
        out_shape = [jax.ShapeDtypeStruct((t, d), F32), jax.ShapeDtypeStruct((t, qkv_width), BF16),
                     jax.ShapeDtypeStruct((t, in_width - qkv_width), BF16)]
        scratch += [pltpu.VMEM((tm, d), F32)]
        steps, follow = ntiles + 1, "inproj"
    else:
        gain, n = final
        qk_width = None
        in_specs += [pl.BlockSpec((1, d), lambda s: (0, 0))]
        args += [gain]
        out_specs = pl.BlockSpec((1, tm, d), lambda s: (s // tpb, jnp.maximum(s % tpb - 1, 0), 0))
        out_shape = jax.ShapeDtypeStruct((n_batch, n, d), F32)
        steps, follow = ntiles, "final"

    return pl.pallas_call(
        functools.partial(_combine_kernel, qk_width=qk_width, follow=follow),
        grid=(steps,),
        in_specs=in_specs,
        out_specs=out_specs,
        out_shape=out_shape,
        scratch_shapes=scratch,
        compiler_params=_params(1),
        name="combine",
    )(*args)


def _rope_tables(n, tm):
    t = jnp.arange(n, dtype=jnp.int32)
    m = RET_DK // 4
    freqs = ROPE_BASE ** (-jnp.arange(m, dtype=F32) / m)
    ang_r = (t // GRID_W).astype(F32)[:, None] * freqs[None, :]
    ang_c = (t % GRID_W).astype(F32)[:, None] * freqs[None, :]
    cos = jnp.concatenate([jnp.cos(ang_r)] * 2 + [jnp.cos(ang_c)] * 2, axis=1)
    sin = jnp.concatenate([-jnp.sin(ang_r), jnp.sin(ang_r), -jnp.sin(ang_c), jnp.sin(ang_c)], axis=1)
    cos = jnp.concatenate([cos, jnp.ones((tm, RET_DK), F32)], axis=0)
    sin = jnp.concatenate([sin, jnp.zeros((tm, RET_DK), F32)], axis=0)
    return cos, sin


def kernel(x, c, ctx, c_ctx, w_mod, b_mod, norm_mix, norm_ffn, w_in, ret_decay_f, ret_decay_b, conv_w, w_out,
           router_w, router_b, w_up, b_up, w_down, b_down, norm_final):
    n_batch, n, d = x.shape
    lc = ctx.shape[1]
    depth = w_mod.shape[0]
    tm = lc
    assert tm % 256 == 0 and n % tm == 0 and n % GRID_W == 0 and tm % GRID_W == 0
    assert n % RET_CHUNK == 0 and lc % RET_CHUNK == 0 and n_batch + 1 <= SUBLANES
    tpb = (lc + n) // tm
    qk_width = RET_HEADS * RET_DK
    qkv_width = 2 * qk_width + RET_HEADS * RET_DV
    n_exp = router_w.shape[2]
    sorted_rows = TOP_K * tm + n_exp * ROW_ALIGN
    ntiles = n_batch * tpb
    nb_max = _max_blocks(ntiles, sorted_rows, n_exp)
    total_rows = nb_max * EXPERT_TILE + 2 * sorted_rows + n_exp * EXPERT_TILE

    cond = jnp.zeros((SUBLANES, d), F32).at[:n_batch].set(c).at[n_batch].set(c_ctx)
    mod_all = _modulation(cond, w_mod, b_mod).reshape(depth, SUBLANES, 6, d)
    mod_all = jnp.pad(mod_all, ((0, 0), (0, 0), (0, SUBLANES - 6), (0, 0)))
    cos_t, sin_t = _rope_tables(n, tm)

    tile_args = dict(tm=tm, tpb=tpb, n_batch=n_batch)
    xs, qkv, rest = _inproj(ctx, x, mod_all[0], norm_mix[0][None, :], w_in[0].astype(BF16), cos_t, sin_t,
                            qk_width=qk_width, qkv_width=qkv_width, **tile_args)
    for layer in range(depth):
        mods = mod_all[layer]
        dec, dmat, gc = _decay_tables(ret_decay_f[layer], ret_decay_b[layer])
        y = _retention(qkv, dec, dmat, gc, n_batch=n_batch, lc=lc, n=n, qk_width=qk_width)
        conv_w8 = jnp.pad(conv_w[layer], ((0, SUBLANES - conv_w.shape[1]), (0, 0)))
        gain2 = norm_ffn[layer][None, :]
        rb_col = jnp.broadcast_to(router_b[layer][:, None], (n_exp, LANES))
        xs, h2, ei, wt, cnt = _mix(xs, y, rest, mods, conv_w8, w_out[layer].astype(BF16), gain2, router_w[layer].T,
                                   rb_col, **tile_args)
        off, chunk_dst, chunk_src, fill_dst, sched, n_used = _routing_plan(
            cnt[:, :, 0], tm=tm, sorted_rows=sorted_rows)
        off_col = jnp.broadcast_to(off.astype(F32)[:, :, None], (ntiles, n_exp, LANES))
        xb, pos = _dispatch(h2, ei, off_col, chunk_dst, fill_dst, tm=tm, sorted_rows=sorted_rows,
                            total_rows=total_rows)
        yb = _experts(xb, sched, n_used, w_up, b_up, w_down, b_down, layer=layer, nb_max=nb_max)
        if layer + 1 == depth:
            return _combine(xs, mods, pos, wt, chunk_src, yb, sorted_rows=sorted_rows,
                            final=(norm_final[None, :], n), **tile_args)
        nxt = layer + 1
        xs, qkv, rest = _combine(
            xs, mods, pos, wt, chunk_src, yb, sorted_rows=sorted_rows,
            inproj=(mod_all[nxt], norm_mix[nxt][None, :], w_in[nxt].astype(BF16), cos_t, sin_t, qk_width, qkv_width),
            **tile_args)
```

```python
import functools

import jax
import jax.numpy as jnp
from jax import lax
from jax.experimental import pallas as pl
from jax.experimental.pallas import tpu as pltpu

GRID_W = 64
RET_HEADS = 4
RET_DK = 128
RET_DV = 256
RET_CHUNK = 256
TOP_K = 4
SWIGLU_LIMIT = 7.0
SWIGLU_ALPHA = 1.702
ROPE_BASE = 10000.0
EPS = 1e-6

SUBLANES = 8
LANES = 128
VMEM_LIMIT = 56 * 1024 * 1024

EXPERT_TILE = 512
ROW_ALIGN = SUBLANES
FILL_SIZES = tuple(EXPERT_TILE >> k for k in range(1, EXPERT_TILE.bit_length()) if EXPERT_TILE >> k >= ROW_ALIGN)

F32 = jnp.float32
BF16 = jnp.bfloat16
NT_DIMS = (((1,), (1,)), ((), ()))
TN_DIMS = (((0,), (0,)), ((), ()))


def _params(n_axes=1):
    return pltpu.CompilerParams(dimension_semantics=("arbitrary",) * n_axes, vmem_limit_bytes=VMEM_LIMIT)


def _modulated_norm(x, gain, shift, scale):
    ms = jnp.mean(x * x, axis=-1, keepdims=True)
    return x * lax.rsqrt(ms + EPS) * (gain * (1.0 + scale)) + shift


def _sigmoid(z):
    return 1.0 / (1.0 + jnp.exp(-z))


def _split_bf16(a):
    hi = a.astype(BF16)
    lo = (a - hi.astype(F32)).astype(BF16)
    return hi, lo


def _pack_bf16_pairs(a):
    w = a.shape[1] // 2
    lo = lax.bitcast_convert_type(a[:, :w], jnp.uint32)
    hi = lax.bitcast_convert_type(a[:, w:], jnp.uint32)
    return (lo >> 16) | hi


def _unpack_bf16_pairs(p):
    lo = lax.bitcast_convert_type(p << 16, F32)
    hi = lax.bitcast_convert_type(p & jnp.uint32(0xFFFF0000), F32)
    return jnp.concatenate([lo, hi], axis=1).astype(BF16)


def _modulation_kernel(c_ref, w_ref, b_ref, o_ref):
    c = c_ref[...]
    s = c * _sigmoid(c)
    s_hi, s_lo = _split_bf16(s)
    w_hi, w_lo = _split_bf16(w_ref[0])
    acc = jnp.dot(s_hi, w_hi, preferred_element_type=F32)
    acc += jnp.dot(s_hi, w_lo, preferred_element_type=F32)
    acc += jnp.dot(s_lo, w_hi, preferred_element_type=F32)
    o_ref[0] = acc + b_ref[0]


def _modulation(cond, w_mod, b_mod):
    depth, d, six_d = w_mod.shape
    nblk = six_d // d
    return pl.pallas_call(
        _modulation_kernel,
        grid=(depth, nblk),
        in_specs=[
            pl.BlockSpec((SUBLANES, d), lambda l, j: (0, 0)),
            pl.BlockSpec((1, d, d), lambda l, j: (l, 0, j)),
            pl.BlockSpec((1, 1, d), lambda l, j: (l, 0, j)),
        ],
        out_specs=pl.BlockSpec((1, SUBLANES, d), lambda l, j: (l, 0, j)),
        out_shape=jax.ShapeDtypeStruct((depth, SUBLANES, six_d), F32),
        compiler_params=_params(2),
        name="modulation",
    )(cond, w_mod, b_mod.reshape(depth, 1, six_d))


def _inproj_kernel(ctx_ref, x_ref, mod_ref, gain_ref, w_ref, cos_ref, sin_ref, xs_ref, qkv_ref, rest_ref, *,
                   qk_width, tpb):
    is_ctx = (pl.program_id(0) % tpb) == 0
    x = jnp.where(is_ctx, ctx_ref[0], x_ref[0])
    xs_ref[...] = x
    _inproj_tile(x, mod_ref, gain_ref, w_ref, cos_ref, sin_ref, qkv_ref, rest_ref, qk_width)


def _inproj_tile(x, mod_ref, gain_ref, w_ref, cos_ref, sin_ref, qkv_ref, rest_ref, qk_width):
    q_scale = RET_DK ** -0.5
    h = _modulated_norm(x, gain_ref[...], mod_ref[0, 0:1, :], mod_ref[0, 1:2, :]).astype(BF16)
    reps = qk_width // RET_DK
    cos = jnp.concatenate([cos_ref[...]] * reps, axis=1)
    sin = jnp.concatenate([sin_ref[...]] * reps, axis=1)
    lane = lax.broadcasted_iota(jnp.int32, cos.shape, 1)
    first_half = (lane % (RET_DK // 2)) < (RET_DK // 4)
    for part, scale in ((0, q_scale), (1, 1.0)):
        c0 = part * qk_width
        p = jnp.dot(h, w_ref[:, c0:c0 + qk_width], preferred_element_type=F32)
        partner = jnp.where(first_half, pltpu.roll(p, qk_width - RET_DK // 4, 1), pltpu.roll(p, RET_DK // 4, 1))
        r = p * cos + partner * sin
        if scale != 1.0:
            r = r * scale
        qkv_ref[:, c0:c0 + qk_width] = r.astype(BF16)
    qkv_width = qkv_ref.shape[1]
    step = 512
    for c0 in range(2 * qk_width, qkv_width, step):
        qkv_ref[:, c0:c0 + step] = jnp.dot(h, w_ref[:, c0:c0 + step], preferred_element_type=F32).astype(BF16)
    for c0 in range(0, rest_ref.shape[1], step):
        rest_ref[:, c0:c0 + step] = jnp.dot(
            h, w_ref[:, qkv_width + c0:qkv_width + c0 + step], preferred_element_type=F32).astype(BF16)


def _inproj(ctx, x, mods, gain, w_bf16, cos_t, sin_t, *, tm, tpb, n_batch, qk_width, qkv_width):
    d = x.shape[2]
    t = n_batch * tpb * tm
    in_width = w_bf16.shape[1]
    rest_width = in_width - qkv_width
    pos_blocks = cos_t.shape[0] // tm - 1

    def seg(i):
        return jnp.where(i % tpb == 0, n_batch, i // tpb)

    def pos(i):
        j = i % tpb
        return jnp.where(j == 0, pos_blocks, j - 1)

    return pl.pallas_call(
        functools.partial(_inproj_kernel, qk_width=qk_width, tpb=tpb),
        grid=(t // tm,),
        in_specs=[
            pl.BlockSpec((1, tm, d), lambda i: (i // tpb, 0, 0)),
            pl.BlockSpec((1, tm, d), lambda i: (i // tpb, jnp.maximum(i % tpb - 1, 0), 0)),
            pl.BlockSpec((1, SUBLANES, d), lambda i: (seg(i), 0, 0)),
            pl.BlockSpec((1, d), lambda i: (0, 0)),
            pl.BlockSpec((d, in_width), lambda i: (0, 0)),
            pl.BlockSpec((tm, RET_DK), lambda i: (pos(i), 0)),
            pl.BlockSpec((tm, RET_DK), lambda i: (pos(i), 0)),
        ],
        out_specs=[
            pl.BlockSpec((tm, d), lambda i: (i, 0)),
            pl.BlockSpec((tm, qkv_width), lambda i: (i, 0)),
            pl.BlockSpec((tm, rest_width), lambda i: (i, 0)),
        ],
        out_shape=[
            jax.ShapeDtypeStruct((t, d), F32),
            jax.ShapeDtypeStruct((t, qkv_width), BF16),
            jax.ShapeDtypeStruct((t, rest_width), BF16),
        ],
        compiler_params=_params(1),
        name="inproj",
    )(ctx, x, mods, gain, w_bf16, cos_t, sin_t)


def _retention_kernel(gc_ref, q_ref, k_ref, v_ref, dec_ref, dmat_ref, y_ref, sf_ref, sb_ref, st_ref, *, lc, n):
    head = pl.program_id(1)
    c = RET_CHUNK
    qd = dec_ref[0, :, 0:2 * RET_DK]
    kdf = dec_ref[0, :, 2 * RET_DK:3 * RET_DK]
    kdb = dec_ref[0, :, 3 * RET_DK:4 * RET_DK]
    gcf = gc_ref[head, 0]
    gcb = gc_ref[head, 1]

    def rows(chunk):
        return pl.ds(pl.multiple_of(chunk * c, c), c)

    def advance(s_ref, chunk, kd, gc):
        kdec = (k_ref[rows(chunk), :].astype(F32) * kd).astype(BF16)
        kv = lax.dot_general(kdec, v_ref[rows(chunk), :], TN_DIMS, preferred_element_type=F32)
        s_ref[...] = gc * s_ref[...] + kv

    def sweep(chunk0, nchunks):
        def states(i, carry):
            cf = chunk0 + i
            cb = chunk0 + nchunks - 1 - i
            st_ref[cf, 0:RET_DK, :] = sf_ref[...].astype(BF16)
            st_ref[cb, RET_DK:2 * RET_DK, :] = sb_ref[...].astype(BF16)
            advance(sf_ref, cf, kdf, gcf)
            advance(sb_ref, cb, kdb, gcb)
            return carry

        def outputs(i, carry):
            r = rows(chunk0 + i)
            q, k, v = q_ref[r, :], k_ref[r, :], v_ref[r, :]
            scores = lax.dot_general(q, k, NT_DIMS, preferred_element_type=F32)
            qdec = (jnp.concatenate([q, q], axis=1).astype(F32) * qd).astype(BF16)
            y = jnp.dot((scores * dmat_ref[0]).astype(BF16), v, preferred_element_type=F32)
            y += jnp.dot(qdec, st_ref[chunk0 + i], preferred_element_type=F32)
            y_ref[r, :] = (y * lax.rsqrt(jnp.mean(y * y, axis=-1, keepdims=True) + EPS)).astype(BF16)
            return carry

        lax.fori_loop(0, nchunks, states, 0, unroll=4 if nchunks % 4 == 0 else 1)
        lax.fori_loop(0, nchunks, outputs, 0, unroll=8 if nchunks % 8 == 0 else 1)

    sf_ref[...] = jnp.zeros_like(sf_ref)
    sb_ref[...] = jnp.zeros_like(sb_ref)
    sweep(0, lc // c)
    sweep(lc // c, n // c)


def _retention(qkv, dec, dmat, gc, *, n_batch, lc, n, qk_width):
    s = lc + n
    t = qkv.shape[0]
    k_blk0 = qk_width // RET_DK
    v_blk0 = 2 * qk_width // RET_DV
    return pl.pallas_call(
        functools.partial(_retention_kernel, lc=lc, n=n),
        grid_spec=pltpu.PrefetchScalarGridSpec(
            num_scalar_prefetch=1,
            grid=(n_batch, RET_HEADS),
            in_specs=[
                pl.BlockSpec((s, RET_DK), lambda b, h, gc: (b, h)),
                pl.BlockSpec((s, RET_DK), lambda b, h, gc: (b, k_blk0 + h)),
                pl.BlockSpec((s, RET_DV), lambda b, h, gc: (b, v_blk0 + h)),
                pl.BlockSpec((1, RET_CHUNK, 4 * RET_DK), lambda b, h, gc: (h, 0, 0)),
                pl.BlockSpec((1, RET_CHUNK, RET_CHUNK), lambda b, h, gc: (h, 0, 0)),
            ],
            out_specs=pl.BlockSpec((s, RET_DV), lambda b, h, gc: (b, h)),
            scratch_shapes=[
                pltpu.VMEM((RET_DK, RET_DV), F32),
                pltpu.VMEM((RET_DK, RET_DV), F32),
                pltpu.VMEM((s // RET_CHUNK, 2 * RET_DK, RET_DV), BF16),
            ],
        ),
        out_shape=jax.ShapeDtypeStruct((t, RET_HEADS * RET_DV), BF16),
        compiler_params=_params(2),
        name="retention",
    )(gc, qkv, qkv, qkv, dec, dmat)


def _decay_tables(logit_f, logit_b):
    c = RET_CHUNK
    lg_f = jax.nn.log_sigmoid(logit_f.astype(F32))[:, None]
    lg_b = jax.nn.log_sigmoid(logit_b.astype(F32))[:, None]
    i = jnp.arange(c, dtype=F32)[None, :]
    vecs = jnp.stack([
        jnp.exp((i + 1) * lg_f), jnp.exp((c - i) * lg_b),
        jnp.exp((c - 1 - i) * lg_f), jnp.exp(i * lg_b)], axis=2)
    dec = jnp.broadcast_to(vecs[..., None], vecs.shape + (RET_DK,)).reshape(vecs.shape[0], c, 4 * RET_DK)
    diff = i[0][:, None] - i[0][None, :]
    lower = jnp.where(diff >= 0, jnp.exp(jnp.where(diff >= 0, diff, 0.0)[None] * lg_f[:, :, None]), 0.0)
    upper = jnp.where(diff < 0, jnp.exp(jnp.where(diff < 0, -diff, 0.0)[None] * lg_b[:, :, None]), 0.0)
    gc = jnp.concatenate([jnp.exp(c * lg_f), jnp.exp(c * lg_b)], axis=1)
    return dec, lower + upper, gc


def _mix_kernel(x_ref, y_ref, rest_ref, mod_ref, cw_ref, wo_ref, gain2_ref, rw_ref, rb_ref,
                o_ref, h2_ref, ei_ref, wt_ref, cnt_ref, h_prev, *, tpb, ntiles, ret_width, conv_width):
    step = pl.program_id(0)

    @pl.when(step == 0)
    def _():
        h_prev[...] = jnp.zeros_like(h_prev)

    _route(h_prev[...], rw_ref, rb_ref, ei_ref, wt_ref, cnt_ref)
    is_ctx = (jnp.minimum(step, ntiles - 1) % tpb) == 0
    tm = x_ref.shape[0]
    g = rest_ref[:, 0:ret_width].astype(F32)
    cb = rest_ref[:, ret_width:ret_width + conv_width].astype(F32)
    cc = rest_ref[:, ret_width + conv_width:ret_width + 2 * conv_width].astype(F32)
    cx = rest_ref[:, ret_width + 2 * conv_width:ret_width + 3 * conv_width].astype(F32)
    ret = (g * _sigmoid(g) * y_ref[...].astype(F32)).astype(BF16)
    u = cc * cx
    row = lax.broadcasted_iota(jnp.int32, u.shape, 0)
    pos = jnp.where(is_ctx, row, row % GRID_W)
    last = jnp.where(is_ctx, tm - 1, GRID_W - 1)
    u_prev = jnp.where(pos == 0, 0.0, pltpu.roll(u, 1, 0))
    u_next = jnp.where(pos == last, 0.0, pltpu.roll(u, tm - 1, 0))
    conv = (cb * (u_prev * cw_ref[0:1, :] + u * cw_ref[1:2, :] + u_next * cw_ref[2:3, :])).astype(BF16)
    halves = []
    n_half = wo_ref.shape[1] // 2
    for c0 in (0, n_half):
        part = jnp.dot(ret, wo_ref[0:ret_width, c0:c0 + n_half], preferred_element_type=F32)
        part += jnp.dot(conv, wo_ref[ret_width:ret_width + conv_width, c0:c0 + n_half], preferred_element_type=F32)
        halves.append(part)
    mix = jnp.concatenate(halves, axis=1)
    x_new = x_ref[...] + mod_ref[0, 2:3, :] * mix
    o_ref[...] = x_new
    h2 = _modulated_norm(x_new, gain2_ref[...], mod_ref[0, 3:4, :], mod_ref[0, 4:5, :])
    h2_ref[...] = h2.astype(BF16)
    h_prev[...] = h2


def _mix(xs, y, rest, mods, conv_w8, wo_bf16, gain2, rw_t, rb_col, *, tm, tpb, n_batch):
    t, d = xs.shape
    ntiles = t // tm
    ret_width = y.shape[1]
    conv_width = conv_w8.shape[1]
    n_exp = rw_t.shape[0]

    def cur(s):
        return jnp.minimum(s, ntiles - 1)

    def prev(s):
        return jnp.maximum(s - 1, 0)

    def seg(i):
        return jnp.where(i % tpb == 0, n_batch, i // tpb)

    return pl.pallas_call(
        functools.partial(_mix_kernel, tpb=tpb, ntiles=ntiles, ret_width=ret_width, conv_width=conv_width),
        grid=(ntiles + 1,),
        in_specs=[
            pl.BlockSpec((tm, d), lambda s: (cur(s), 0)),
            pl.BlockSpec((tm, ret_width), lambda s: (cur(s), 0)),
            pl.BlockSpec((tm, rest.shape[1]), lambda s: (cur(s), 0)),
            pl.BlockSpec((1, SUBLANES, d), lambda s: (seg(cur(s)), 0, 0)),
            pl.BlockSpec((SUBLANES, conv_width), lambda s: (0, 0)),
            pl.BlockSpec(wo_bf16.shape, lambda s: (0, 0)),
            pl.BlockSpec((1, d), lambda s: (0, 0)),
            pl.BlockSpec((n_exp, d), lambda s: (0, 0)),
            pl.BlockSpec((n_exp, LANES), lambda s: (0, 0)),
        ],
        out_specs=[
            pl.BlockSpec((tm, d), lambda s: (cur(s), 0)),
            pl.BlockSpec((tm, d), lambda s: (cur(s), 0)),
            pl.BlockSpec((1, SUBLANES, tm), lambda s: (prev(s), 0, 0)),
            pl.BlockSpec((1, SUBLANES, tm), lambda s: (prev(s), 0, 0)),
            pl.BlockSpec((1, n_exp, LANES), lambda s: (prev(s), 0, 0)),
        ],
        out_shape=[
            jax.ShapeDtypeStruct((t, d), F32),
            jax.ShapeDtypeStruct((t, d), BF16),
            jax.ShapeDtypeStruct((ntiles, SUBLANES, tm), jnp.int32),
            jax.ShapeDtypeStruct((ntiles, SUBLANES, tm), F32),
            jax.ShapeDtypeStruct((ntiles, n_exp, LANES), jnp.int32),
        ],
        scratch_shapes=[pltpu.VMEM((tm, d), F32)],
        compiler_params=_params(1),
        name="mix",
    )(xs, y, rest, mods, conv_w8, wo_bf16, gain2, rw_t, rb_col)


def _route(h2, rw_ref, rb_ref, ei_ref, wt_ref, cnt_ref):
    h_hi, h_lo = _split_bf16(h2)
    w_hi, w_lo = _split_bf16(rw_ref[...])
    logits = lax.dot_general(w_hi, h_hi, NT_DIMS, preferred_element_type=F32)
    logits += lax.dot_general(w_hi, h_lo, NT_DIMS, preferred_element_type=F32)
    logits += lax.dot_general(w_lo, h_hi, NT_DIMS, preferred_element_type=F32)
    logits += rb_ref[:, 0:1]
    n_exp, tt = logits.shape
    expert = lax.broadcasted_iota(jnp.int32, (n_exp, tt), 0).astype(F32)
    slot = lax.broadcasted_iota(jnp.int32, (SUBLANES, tt), 0)
    ei = jnp.zeros((SUBLANES, tt), F32)
    ev = jnp.zeros((SUBLANES, tt), F32)
    taken = jnp.zeros((n_exp, tt), F32)
    top = None
    denom = jnp.zeros((1, tt), F32)
    for k in range(TOP_K):
        m = jnp.max(logits, axis=0, keepdims=True)
        idx = jnp.min(jnp.where(logits == m, expert, float(n_exp)), axis=0, keepdims=True)
        hit = expert == idx
        taken += hit.astype(F32)
        logits = jnp.where(hit, -jnp.inf, logits)
        if k == 0:
            top = m
        e = jnp.exp(m - top)
        denom += e
        ei = jnp.where(slot == k, idx, ei)
        ev = jnp.where(slot == k, e, ev)
    ei_ref[0] = ei.astype(jnp.int32)
    wt_ref[0] = jnp.where(slot < TOP_K, ev / denom, 0.0)
    cnt = jnp.sum(taken, axis=1, keepdims=True)
    cnt_ref[0] = jnp.broadcast_to(cnt, (n_exp, LANES)).astype(jnp.int32)


def _routing_plan(cnt, *, tm, sorted_rows):
    ntiles, n_exp = cnt.shape
    seg_len = (cnt + ROW_ALIGN - 1) // ROW_ALIGN * ROW_ALIGN
    off = jnp.cumsum(seg_len, axis=1) - seg_len
    tile_rows = jnp.sum(seg_len, axis=1)
    before = jnp.cumsum(seg_len, axis=0) - seg_len
    filled = jnp.sum(seg_len, axis=0)
    region = (filled + EXPERT_TILE - 1) // EXPERT_TILE * EXPERT_TILE
    region_end = jnp.cumsum(region)
    region_start = region_end - region
    base = region_start[None, :] + before

    nb_max = _max_blocks(ntiles, sorted_rows, n_exp)
    dump0 = nb_max * EXPERT_TILE
    nch = sorted_rows // ROW_ALIGN
    q = jnp.arange(nch, dtype=jnp.int32) * ROW_ALIGN
    seg_end = off + seg_len
    inside = jnp.logical_and(off[:, None, :] <= q[None, :, None], q[None, :, None] < seg_end[:, None, :])
    valid = q[None, :] < tile_rows[:, None]
    row = q[None, :] + jnp.sum(jnp.where(inside, (base - off)[:, None, :], 0), axis=2)
    parity = (jnp.arange(ntiles, dtype=jnp.int32) % 2)[:, None]
    chunk_dst = jnp.where(valid, row, dump0 + parity * sorted_rows + q[None, :]).astype(jnp.int32)
    chunk_src = jnp.where(valid, row, 0).astype(jnp.int32)

    sizes = jnp.asarray(FILL_SIZES, dtype=jnp.int32)[None, :]
    gap = (region - filled)[:, None]
    used = (gap & sizes) != 0
    fill_row = filled[:, None] + (gap & ~(2 * sizes - 1))
    fill_dump = dump0 + 2 * sorted_rows + jnp.arange(n_exp, dtype=jnp.int32)[:, None] * EXPERT_TILE + (
        EXPERT_TILE - 2 * sizes)
    fill_dst = jnp.where(used, region_start[:, None] + fill_row, fill_dump).reshape(-1).astype(jnp.int32)

    blk = jnp.arange(nb_max, dtype=jnp.int32) * EXPERT_TILE
    block_e = jnp.sum((region_end[None, :] <= blk[:, None]).astype(jnp.int32), axis=1)
    block_e = jnp.minimum(block_e, n_exp - 1).astype(jnp.int32)
    n_used = (region_end[-1] // EXPERT_TILE).astype(jnp.int32).reshape(1)
    used = region > 0
    eidx = jnp.arange(n_exp, dtype=jnp.int32)
    rank = (jnp.cumsum(used) - used).astype(jnp.int32)
    later = jnp.where(jnp.logical_and(used[None, :], eidx[None, :] > eidx[:, None]), eidx[None, :], n_exp)
    nxt = jnp.min(later, axis=1)
    nxt = jnp.where(nxt == n_exp, -1, nxt)
    mine = block_e[:, None] == eidx[None, :]
    data_end = jnp.sum(jnp.where(mine, (region_start + filled)[None, :], 0), axis=1)
    sched = jnp.stack([block_e, jnp.sum(jnp.where(mine, (rank % 2)[None, :], 0), axis=1),
                       jnp.sum(jnp.where(mine, nxt[None, :], 0), axis=1),
                       jnp.clip(data_end - blk, 0, EXPERT_TILE)]).astype(jnp.int32)
    return off, chunk_dst, chunk_src, fill_dst, sched, n_used


def _max_blocks(ntiles, sorted_rows, n_exp):
    return -(-(ntiles * sorted_rows + n_exp * EXPERT_TILE) // EXPERT_TILE)


def _sorted_positions(ei, off_col, tt):
    n_exp = off_col.shape[0]
    expert = lax.broadcasted_iota(jnp.int32, (n_exp, tt), 0)
    hits = [expert == ei[k:k + 1, :] for k in range(TOP_K)]
    chosen = sum(h.astype(F32) for h in hits).astype(BF16)
    t_row = lax.broadcasted_iota(jnp.int32, (tt, tt), 0)
    t_col = lax.broadcasted_iota(jnp.int32, (tt, tt), 1)
    earlier = (t_row < t_col).astype(BF16)
    rank = jnp.dot(chosen, earlier, preferred_element_type=F32) + off_col
    return [jnp.sum(jnp.where(h, rank, 0.0), axis=0, keepdims=True) for h in hits]


def _dispatch_kernel(cd_ref, fd_ref, h2_ref, ei_ref, off_ref, xb_ref, pos_ref, buf, zeros, perm_s, sem, fill_sem):
    step = pl.program_id(0)
    tt = h2_ref.shape[0]
    sorted_rows = buf.shape[1]
    nch = sorted_rows // ROW_ALIGN

    def fill_copy(e, s):
        size = FILL_SIZES[s]
        dst = pl.multiple_of(fd_ref[e * len(FILL_SIZES) + s], ROW_ALIGN)
        return pltpu.make_async_copy(zeros.at[pl.ds(0, size), :], xb_ref.at[pl.ds(dst, size), :], fill_sem)

    @pl.when(step == 0)
    def _():
        zeros[...] = jnp.zeros_like(zeros)
        perm_s[...] = jnp.zeros_like(perm_s)

        def start(e, c):
            for s in range(len(FILL_SIZES)):
                fill_copy(e, s).start()
            return c

        def wait(e, c):
            for s in range(len(FILL_SIZES)):
                fill_copy(e, s).wait()
            return c

        n_regions = fd_ref.shape[0] // len(FILL_SIZES)
        lax.fori_loop(0, n_regions, start, 0)
        lax.fori_loop(0, n_regions, wait, 0)

    slot = (step + 1) % 2
    last_step = pl.num_programs(0) - 1

    def chunk_copy(which, j):
        dst = pl.multiple_of(cd_ref[0, 0, j], ROW_ALIGN)
        return pltpu.make_async_copy(buf.at[which, pl.ds(j * ROW_ALIGN, ROW_ALIGN), :],
                                     xb_ref.at[pl.ds(dst, ROW_ALIGN), :], sem.at[which])

    def drain(which):
        for j in range(nch):
            chunk_copy(which, j).wait()

    @pl.when(step >= 3)
    def _():
        drain(slot)

    buf[slot] = _pack_bf16_pairs(jnp.dot(perm_s[...], h2_ref[...], preferred_element_type=F32))

    pos = _sorted_positions(ei_ref[0], off_ref[0][:, 0:1], tt)
    slot_id = lax.broadcasted_iota(jnp.int32, (SUBLANES, tt), 0)
    pos8 = jnp.zeros((SUBLANES, tt), F32)
    for k in range(TOP_K):
        pos8 = jnp.where(slot_id == k, pos[k], pos8)
    pos_ref[0] = pos8
    r = lax.broadcasted_iota(jnp.int32, (sorted_rows, tt), 0)
    hit = r == pos[0].astype(jnp.int32)
    for k in range(1, TOP_K):
        hit = jnp.logical_or(hit, r == pos[k].astype(jnp.int32))
    perm_s[...] = jnp.where(hit, 1.0, 0.0).astype(BF16)

    @pl.when(step >= 1)
    def _():
        for j in range(nch):
            chunk_copy(slot, j).start(priority=j % 2)

    @pl.when(step == last_step)
    def _():
        drain(slot)

    @pl.when(jnp.logical_and(step == last_step, step >= 2))
    def _():
        drain(1 - slot)


def _dispatch(h2, ei, off_col, chunk_dst, fill_dst, *, tm, sorted_rows, total_rows):
    t, d = h2.shape
    ntiles = t // tm
    n_exp = off_col.shape[1]
    nch = sorted_rows // ROW_ALIGN

    def cur(s):
        return jnp.minimum(s, ntiles - 1)

    def prev(s):
        return jnp.maximum(s - 1, 0)

    return pl.pallas_call(
        _dispatch_kernel,
        grid=(ntiles + 1,),
        in_specs=[
            pl.BlockSpec((1, 1, nch), lambda s: (prev(s), 0, 0), memory_space=pltpu.SMEM),
            pl.BlockSpec(memory_space=pltpu.SMEM),
            pl.BlockSpec((tm, d), lambda s: (prev(s), 0)),
            pl.BlockSpec((1, SUBLANES, tm), lambda s: (cur(s), 0, 0)),
            pl.BlockSpec((1, n_exp, LANES), lambda s: (cur(s), 0, 0)),
        ],
        out_specs=[
            pl.BlockSpec(memory_space=pl.ANY),
            pl.BlockSpec((1, SUBLANES, tm), lambda s: (cur(s), 0, 0)),
        ],
        out_shape=[
            jax.ShapeDtypeStruct((total_rows, d // 2), jnp.uint32),
            jax.ShapeDtypeStruct((ntiles, SUBLANES, tm), F32),
        ],
        scratch_shapes=[
            pltpu.VMEM((2, sorted_rows, d // 2), jnp.uint32),
            pltpu.VMEM((FILL_SIZES[0], d // 2), jnp.uint32),
            pltpu.VMEM((sorted_rows, tm), BF16),
            pltpu.SemaphoreType.DMA((2,)),
            pltpu.SemaphoreType.DMA(()),
        ],
        compiler_params=_params(1),
        name="dispatch",
    )(chunk_dst.reshape(ntiles, 1, nch), fill_dst, h2, ei, off_col)


def _experts_kernel(sched_ref, nu_ref, xb_ref, wu_hbm, bu_ref, wd_hbm, bd_ref, yb_ref,
                    wu_f, wd_f, wu_s, wd_s, sem, *, d_ff, layer):
    i = pl.program_id(0)
    expert, slot, nxt = sched_ref[0, i], sched_ref[1, i], sched_ref[2, i]
    fresh = jnp.logical_or(i == 0, expert != sched_ref[0, jnp.maximum(i - 1, 0)])

    def fetch(e, s):
        return (pltpu.make_async_copy(wu_hbm.at[layer, e], wu_f.at[s], sem.at[0, s]),
                pltpu.make_async_copy(wd_hbm.at[layer, e], wd_f.at[s], sem.at[1, s]))

    @pl.when(i == 0)
    def _():
        for c in fetch(expert, slot):
            c.start()

    @pl.when(jnp.logical_and(fresh, i < nu_ref[0]))
    def _():
        for c in fetch(expert, slot):
            c.wait()

        @pl.when(nxt >= 0)
        def _():
            for c in fetch(nxt, 1 - slot):
                c.start()

        wu_s[...] = wu_f[slot].astype(BF16)
        wd_s[...] = wd_f[slot].astype(BF16)

    def mlp(rows):
        u = jnp.dot(_unpack_bf16_pairs(xb_ref[0:rows, :]), wu_s[...], preferred_element_type=F32) + bu_ref[0, 0]
        gate = jnp.minimum(u[:, :d_ff], SWIGLU_LIMIT)
        lin = jnp.clip(u[:, d_ff:], -SWIGLU_LIMIT, SWIGLU_LIMIT)
        act = (gate * _sigmoid(SWIGLU_ALPHA * gate) * (lin + 1.0)).astype(BF16)
        y = jnp.dot(act, wd_s[...], preferred_element_type=F32) + bd_ref[0, 0]
        yb_ref[0:rows, :] = _pack_bf16_pairs(y.astype(BF16).astype(F32))

    rows_used = sched_ref[3, i]
    quarter = xb_ref.shape[0] // 4
    for q in range(1, 5):
        in_range = jnp.logical_and(rows_used > (q - 1) * quarter, rows_used <= q * quarter)
        pl.when(jnp.logical_and(i < nu_ref[0], in_range))(functools.partial(mlp, q * quarter))


def _experts(xb, sched, n_used, w_up, b_up, w_down, b_down, *, layer, nb_max):
    depth, n_exp, d, two_ff = w_up.shape
    d_ff = two_ff // 2
    dp = xb.shape[1]

    def blk(i, sc, nu):
        return jnp.minimum(i, nu[0] - 1)

    def exp(i, sc, nu):
        return sc[0, jnp.minimum(i, nu[0] - 1)]

    return pl.pallas_call(
        functools.partial(_experts_kernel, d_ff=d_ff, layer=layer),
        grid_spec=pltpu.PrefetchScalarGridSpec(
            num_scalar_prefetch=2,
            grid=(nb_max,),
            in_specs=[
                pl.BlockSpec((EXPERT_TILE, dp), lambda i, sc, nu: (blk(i, sc, nu), 0)),
                pl.BlockSpec(memory_space=pl.ANY),
                pl.BlockSpec((1, 1, 1, two_ff), lambda i, sc, nu: (layer, exp(i, sc, nu), 0, 0)),
                pl.BlockSpec(memory_space=pl.ANY),
                pl.BlockSpec((1, 1, 1, d), lambda i, sc, nu: (layer, exp(i, sc, nu), 0, 0)),
            ],
            out_specs=pl.BlockSpec((EXPERT_TILE, dp), lambda i, sc, nu: (blk(i, sc, nu), 0)),
            scratch_shapes=[
                pltpu.VMEM((2, d, two_ff), F32),
                pltpu.VMEM((2, d_ff, d), F32),
                pltpu.VMEM((d, two_ff), BF16),
                pltpu.VMEM((d_ff, d), BF16),
                pltpu.SemaphoreType.DMA((2, 2)),
            ],
        ),
        out_shape=jax.ShapeDtypeStruct((nb_max * EXPERT_TILE, dp), jnp.uint32),
        compiler_params=_params(1),
        name="experts",
    )(sched, n_used, xb, w_up, b_up.reshape(depth, n_exp, 1, two_ff), w_down, b_down.reshape(depth, n_exp, 1, d))


def _combine_kernel(cs_ref, csn_ref, x_ref, mod_ref, pos_ref, wt_ref, yb_ref, *rest, qk_width, follow):
    if follow == "inproj":
        mod2_ref, gain_ref, w_ref, cos_ref, sin_ref, o_ref, qkv_ref, rest_ref, buf, sem, x_prev = rest

        @pl.when(pl.program_id(0) == 0)
        def _():
            x_prev[...] = jnp.zeros_like(x_prev)

        _combine_fetch(cs_ref, csn_ref, yb_ref, buf, sem)()
        _inproj_tile(x_prev[...], mod2_ref, gain_ref, w_ref, cos_ref, sin_ref, qkv_ref, rest_ref, qk_width)
        x_new = _combine_tile(x_ref, mod_ref, pos_ref, wt_ref, buf)
        o_ref[...] = x_new
        x_prev[...] = x_new
    else:
        gain_ref, o_ref, buf, sem = rest
        wait = _combine_fetch(cs_ref, csn_ref, yb_ref, buf, sem)
        x_new = _combine_tile(x_ref, mod_ref, pos_ref, wt_ref, buf, wait)
        ms = jnp.mean(x_new * x_new, axis=-1, keepdims=True)
        o_ref[0] = x_new * lax.rsqrt(ms + EPS) * gain_ref[...]


def _combine_fetch(cs_ref, csn_ref, yb_ref, buf, sem):
    step = pl.program_id(0)
    cur = step % 2
    nch = buf.shape[1] // ROW_ALIGN

    def chunk_copy(table, slot, j):
        src = pl.multiple_of(table[0, 0, j], ROW_ALIGN)
        return pltpu.make_async_copy(yb_ref.at[pl.ds(src, ROW_ALIGN), :],
                                     buf.at[slot, pl.ds(j * ROW_ALIGN, ROW_ALIGN), :], sem.at[slot])

    @pl.when(step == 0)
    def _():
        for j in range(nch):
            chunk_copy(cs_ref, cur, j).start(priority=j % 2)

    @pl.when(step + 1 < pl.num_programs(0))
    def _():
        for j in range(nch):
            chunk_copy(csn_ref, 1 - cur, j).start(priority=j % 2)

    def wait():
        for j in range(nch):
            chunk_copy(cs_ref, cur, j).wait()

    return wait


def _combine_tile(x_ref, mod_ref, pos_ref, wt_ref, buf, wait=None):
    tt = x_ref.shape[0]
    sorted_rows = buf.shape[1]
    r = lax.broadcasted_iota(jnp.int32, (sorted_rows, tt), 0)
    pos = pos_ref[0].astype(jnp.int32)
    wt = wt_ref[0]
    sel = jnp.zeros((sorted_rows, tt), F32)
    for k in reversed(range(TOP_K)):
        sel = jnp.where(r == pos[k:k + 1, :], wt[k:k + 1, :], sel)
    if wait is not None:
        wait()
    y2 = lax.dot_general(sel.astype(BF16), _unpack_bf16_pairs(buf[pl.program_id(0) % 2]), TN_DIMS,
                         preferred_element_type=F32)
    return x_ref[...] + mod_ref[0, 5:6, :] * y2


def _combine(xs, mods, pos, wt, chunk_src, yb, *, tm, tpb, n_batch, sorted_rows, inproj=None, final=None):
    t, d = xs.shape
    ntiles = t // tm
    nch = sorted_rows // ROW_ALIGN

    def seg(i):
        return jnp.where(i % tpb == 0, n_batch, i // tpb)

    def cur(s):
        return jnp.minimum(s, ntiles - 1)

    def prev(s):
        return jnp.maximum(s - 1, 0)

    in_specs = [
        pl.BlockSpec((1, 1, nch), lambda s: (cur(s), 0, 0), memory_space=pltpu.SMEM),
        pl.BlockSpec((1, 1, nch), lambda s: (cur(s + 1), 0, 0), memory_space=pltpu.SMEM),
        pl.BlockSpec((tm, d), lambda s: (cur(s), 0)),
        pl.BlockSpec((1, SUBLANES, d), lambda s: (seg(cur(s)), 0, 0)),
        pl.BlockSpec((1, SUBLANES, tm), lambda s: (cur(s), 0, 0)),
        pl.BlockSpec((1, SUBLANES, tm), lambda s: (cur(s), 0, 0)),
        pl.BlockSpec(memory_space=pl.ANY),
    ]
    args = [chunk_src.reshape(ntiles, 1, nch), chunk_src.reshape(ntiles, 1, nch), xs, mods, pos, wt, yb]
    scratch = [pltpu.VMEM((2, sorted_rows, yb.shape[1]), jnp.uint32), pltpu.SemaphoreType.DMA((2,))]
    if inproj is not None:
        mods2, gain, w_bf16, cos_t, sin_t, qk_width, qkv_width = inproj
        in_width = w_bf16.shape[1]
        pos_blocks = cos_t.shape[0] // tm - 1

        def rope_blk(i):
            j = i % tpb
            return jnp.where(j == 0, pos_blocks, j - 1)

        in_specs += [
            pl.BlockSpec((1, SUBLANES, d), lambda s: (seg(prev(s)), 0, 0)),
            pl.BlockSpec((1, d), lambda s: (0, 0)),
            pl.BlockSpec((d, in_width), lambda s: (0, 0)),
            pl.BlockSpec((tm, RET_DK), lambda s: (rope_blk(prev(s)), 0)),
            pl.BlockSpec((tm, RET_DK), lambda s: (rope_blk(prev(s)), 0)),
        ]
        args += [mods2, gain, w_bf16, cos_t, sin_t]
        out_specs = [pl.BlockSpec((tm, d), lambda s: (cur(s), 0)),
                     pl.BlockSpec((tm, qkv_width), lambda s: (prev(s), 0)),
                     pl.BlockSpec((tm, in_width - qkv_width), lambda s: (prev(s), 0))]
        out_shape = [jax.ShapeDtypeStruct((t, d), F32), jax.ShapeDtypeStruct((t, qkv_width), BF16),
                     jax.ShapeDtypeStruct((t, in_width - qkv_width), BF16)]
        scratch += [pltpu.VMEM((tm, d), F32)]
        steps, follow = ntiles + 1, "inproj"
    else:
        gain, n = final
        qk_width = None
        in_specs += [pl.BlockSpec((1, d), lambda s: (0, 0))]
        args += [gain]
        out_specs = pl.BlockSpec((1, tm, d), lambda s: (s // tpb, jnp.maximum(s % tpb - 1, 0), 0))
        out_shape = jax.ShapeDtypeStruct((n_batch, n, d), F32)
        steps, follow = ntiles, "final"

    return pl.pallas_call(
        functools.partial(_combine_kernel, qk_width=qk_width, follow=follow),
        grid=(steps,),
        in_specs=in_specs,
        out_specs=out_specs,
        out_shape=out_shape,
        scratch_shapes=scratch,
        compiler_params=_params(1),
        name="combine",
    )(*args)


def _rope_tables(n, tm):
    t = jnp.arange(n, dtype=jnp.int32)
    m = RET_DK // 4
    freqs = ROPE_BASE ** (-jnp.arange(m, dtype=F32) / m)
    ang_r = (t // GRID_W).astype(F32)[:, None] * freqs[None, :]
    ang_c = (t % GRID_W).astype(F32)[:, None] * freqs[None, :]
    cos = jnp.concatenate([jnp.cos(ang_r)] * 2 + [jnp.cos(ang_c)] * 2, axis=1)
    sin = jnp.concatenate([-jnp.sin(ang_r), jnp.sin(ang_r), -jnp.sin(ang_c), jnp.sin(ang_c)], axis=1)
    cos = jnp.concatenate([cos, jnp.ones((tm, RET_DK), F32)], axis=0)
    sin = jnp.concatenate([sin, jnp.zeros((tm, RET_DK), F32)], axis=0)
    return cos, sin


def kernel(x, c, ctx, c_ctx, w_mod, b_mod, norm_mix, norm_ffn, w_in, ret_decay_f, ret_decay_b, conv_w, w_out,
           router_w, router_b, w_up, b_up, w_down, b_down, norm_final):
    n_batch, n, d = x.shape
    lc = ctx.shape[1]
    depth = w_mod.shape[0]
    tm = lc
    assert tm % 256 == 0 and n % tm == 0 and n % GRID_W == 0 and tm % GRID_W == 0
    assert n % RET_CHUNK == 0 and lc % RET_CHUNK == 0 and n_batch + 1 <= SUBLANES
    tpb = (lc + n) // tm
    qk_width = RET_HEADS * RET_DK
    qkv_width = 2 * qk_width + RET_HEADS * RET_DV
    n_exp = router_w.shape[2]
    sorted_rows = TOP_K * tm + n_exp * ROW_ALIGN
    ntiles = n_batch * tpb
    nb_max = _max_blocks(ntiles, sorted_rows, n_exp)
    total_rows = nb_max * EXPERT_TILE + 2 * sorted_rows + n_exp * EXPERT_TILE

    cond = jnp.zeros((SUBLANES, d), F32).at[:n_batch].set(c).at[n_batch].set(c_ctx)
    mod_all = _modulation(cond, w_mod, b_mod).reshape(depth, SUBLANES, 6, d)
    mod_all = jnp.pad(mod_all, ((0, 0), (0, 0), (0, SUBLANES - 6), (0, 0)))
    cos_t, sin_t = _rope_tables(n, tm)

    tile_args = dict(tm=tm, tpb=tpb, n_batch=n_batch)
    xs, qkv, rest = _inproj(ctx, x, mod_all[0], norm_mix[0][None, :], w_in[0].astype(BF16), cos_t, sin_t,
                            qk_width=qk_width, qkv_width=qkv_width, **tile_args)
    for layer in range(depth):
        mods = mod_all[layer]
        dec, dmat, gc = _decay_tables(ret_decay_f[layer], ret_decay_b[layer])
        y = _retention(qkv, dec, dmat, gc, n_batch=n_batch, lc=lc, n=n, qk_width=qk_width)
        conv_w8 = jnp.pad(conv_w[layer], ((0, SUBLANES - conv_w.shape[1]), (0, 0)))
        gain2 = norm_ffn[layer][None, :]
        rb_col = jnp.broadcast_to(router_b[layer][:, None], (n_exp, LANES))
        xs, h2, ei, wt, cnt = _mix(xs, y, rest, mods, conv_w8, w_out[layer].astype(BF16), gain2, router_w[layer].T,
                                   rb_col, **tile_args)
        off, chunk_dst, chunk_src, fill_dst, sched, n_used = _routing_plan(
            cnt[:, :, 0], tm=tm, sorted_rows=sorted_rows)
        off_col = jnp.broadcast_to(off.astype(F32)[:, :, None], (ntiles, n_exp, LANES))
        xb, pos = _dispatch(h2, ei, off_col, chunk_dst, fill_dst, tm=tm, sorted_rows=sorted_rows,
                            total_rows=total_rows)
        yb = _experts(xb, sched, n_used, w_up, b_up, w_down, b_down, layer=layer, nb_max=nb_max)
        if layer + 1 == depth:
            return _combine(xs, mods, pos, wt, chunk_src, yb, sorted_rows=sorted_rows,
                            final=(norm_final[None, :], n), **tile_args)
        nxt = layer + 1
        xs, qkv, rest = _combine(
            xs, mods, pos, wt, chunk_src, yb, sorted_rows=sorted_rows,
            inproj=(mod_all[nxt], norm_mix[nxt][None, :], w_in[nxt].astype(BF16), cos_t, sin_t, qk_width, qkv_width),
            **tile_args)
```
